```python
import math
import jax, jax.numpy as jnp
from jax import lax
import numpy as np

D_MODEL = 1024
BATCH = 8
SEQ = 8192
DEPTH = 4

N_MIXERS = 3
N_ATTN = (DEPTH + 2) // 3
N_HGRN = (DEPTH + 1) // 3
N_LRU = DEPTH // 3
N_DENSE = (DEPTH + 1) // 2
N_MOE = DEPTH // 2

DA_HEADS = 8
DA_HEAD_DIM = 64
DA_V_DIM = 2 * DA_HEAD_DIM
DA_QK_WIDTH = DA_HEADS * 2 * DA_HEAD_DIM
DA_V_WIDTH = DA_HEADS * DA_V_DIM
ROPE_THETA = 10000.0
Q_BLOCK = 128

HG_HEADS = 8
HG_KEY_DIM = D_MODEL // HG_HEADS
HG_VAL_DIM = D_MODEL // HG_HEADS
HG_K_WIDTH = HG_HEADS * HG_KEY_DIM
HG_V_WIDTH = HG_HEADS * HG_VAL_DIM
HG_CHUNK = 64

LRU_WIDTH = 1280
LRU_BLOCKS = 10
LRU_BLOCK_W = LRU_WIDTH // LRU_BLOCKS
CONV_WIDTH = 4
CONV_PAD = (1, 2)
LRU_C = 8.0

D_FF = 2816
N_EXPERTS = 8
TOP_K = 2
D_FF_EXPERT = 3584
EPS = 1e-6

kernel_name = 'hybrid_diffattn_hgrn2_rglru_moe_encoder'


def rms_norm(x, g):
    xf = x.astype(jnp.float32)
    y = xf * lax.rsqrt(jnp.mean(xf * xf, axis=-1, keepdims=True) + EPS)
    return (y * g.astype(jnp.float32)).astype(x.dtype)


def modulate(x, g, shift, scale):
    return rms_norm(x, g) * (1.0 + scale[:, None, :]) + shift[:, None, :]


def rotary_tables(seq, dim):
    inv = 1.0 / (ROPE_THETA ** (jnp.arange(0, dim, 2, dtype=jnp.float32) / dim))
    ang = jnp.arange(seq, dtype=jnp.float32)[:, None] * inv[None, :]
    return jnp.cos(ang), jnp.sin(ang)


def apply_rope(x, cos, sin):
    half = x.shape[-1] // 2
    xf = x.astype(jnp.float32)
    x1, x2 = xf[..., :half], xf[..., half:]
    c = cos[:, None, None, :]
    s = sin[:, None, None, :]
    return jnp.concatenate([x1 * c - x2 * s, x2 * c + x1 * s], axis=-1).astype(x.dtype)


def diff_attention(h, w_in, q_norm_g, k_norm_g, lam_vecs, subln_g, w_out, lambda_init, cos, sin):
    B, S, _ = h.shape
    q, k, v = jnp.split(h @ w_in, [DA_QK_WIDTH, 2 * DA_QK_WIDTH], axis=-1)
    q = q.reshape(B, S, DA_HEADS, 2, DA_HEAD_DIM)
    k = k.reshape(B, S, DA_HEADS, 2, DA_HEAD_DIM)
    v = v.reshape(B, S, DA_HEADS, DA_V_DIM)
    q = apply_rope(rms_norm(q, q_norm_g), cos, sin)
    k = apply_rope(rms_norm(k, k_norm_g), cos, sin)
    lf = lam_vecs.astype(jnp.float32)
    lam = jnp.exp(jnp.sum(lf[0] * lf[1])) - jnp.exp(jnp.sum(lf[2] * lf[3])) + lambda_init
    scale = DA_HEAD_DIM ** -0.5
    nb = S // Q_BLOCK
    q_blocks = jnp.moveaxis(q.reshape(B, nb, Q_BLOCK, DA_HEADS, 2, DA_HEAD_DIM), 1, 0)

    def block(qb):
        s = jnp.einsum('bqhjd,bkhjd->bhjqk', qb, k).astype(jnp.float32) * scale
        p = jax.nn.softmax(s, axis=-1)
        w = p[:, :, 0] - lam * p[:, :, 1]
        return jnp.einsum('bhqk,bkhe->bqhe', w.astype(v.dtype), v)

    o = lax.map(block, q_blocks)
    o = jnp.moveaxis(o, 0, 1).reshape(B, S, DA_HEADS, DA_V_DIM)
    o = rms_norm(o, subln_g) * (1.0 - lambda_init)
    return o.reshape(B, S, DA_V_WIDTH) @ w_out


def gla_chunk_scan(q, k, log_f, v):
    B, S, H, K = q.shape
    V = v.shape[-1]
    n = S // HG_CHUNK

    def to_chunks(t):
        return jnp.swapaxes(t.reshape(B, n, HG_CHUNK, H, t.shape[-1]), 0, 1)

    causal = jnp.tril(jnp.ones((HG_CHUNK, HG_CHUNK), dtype=bool))[None, :, :, None, None]

    def step(state, inp):
        qc, kc, fc, vc = inp
        b = jnp.cumsum(fc, axis=1)
        o_inter = jnp.einsum('bthk,bhkv->bthv', qc * jnp.exp(b), state)
        rel = jnp.where(causal, b[:, :, None] - b[:, None, :], -jnp.inf)
        scores = jnp.einsum('bthk,bshk,btshk->bths', qc, kc, jnp.exp(rel))
        o_intra = jnp.einsum('bths,bshv->bthv', scores, vc)
        b_last = b[:, -1]
        new_state = jnp.exp(b_last)[..., None] * state + jnp.einsum(
            'bshk,bshv->bhkv', kc * jnp.exp(b_last[:, None] - b), vc)
        return new_state, o_inter + o_intra

    state0 = jnp.zeros((B, H, K, V), jnp.float32)
    _, o = lax.scan(step, state0, (to_chunks(q), to_chunks(k), to_chunks(log_f), to_chunks(v)))
    return jnp.swapaxes(o, 0, 1).reshape(B, S, H, V)


def hgrn2(h, w_in, lower_bounds, norm_g, w_out, layer):
    B, S, _ = h.shape
    q, f_fw, f_bw, i_in, g = jnp.split(
        h @ w_in, [HG_K_WIDTH, 2 * HG_K_WIDTH, 3 * HG_K_WIDTH, 3 * HG_K_WIDTH + HG_V_WIDTH], axis=-1)
    lb_all = jnp.cumsum(jax.nn.softmax(lower_bounds.astype(jnp.float32), axis=1), axis=1)
    lb = lb_all[:, layer] - lb_all[:, 0]
    qh = jax.nn.silu(q.astype(jnp.float32)).reshape(B, S, HG_HEADS, HG_KEY_DIM)
    vh = i_in.astype(jnp.float32).reshape(B, S, HG_HEADS, HG_VAL_DIM)
    o_sum = jnp.zeros((B, S, HG_HEADS, HG_VAL_DIM), jnp.float32)
    for d, f_raw in enumerate((f_fw, f_bw)):
        forget = lb[d] + (1.0 - lb[d]) * jax.nn.sigmoid(f_raw.astype(jnp.float32))
        kh = (1.0 - forget).reshape(B, S, HG_HEADS, HG_KEY_DIM)
        lfh = jnp.log(forget).reshape(B, S, HG_HEADS, HG_KEY_DIM)
        if d == 0:
            o_sum = o_sum + gla_chunk_scan(qh, kh, lfh, vh)
        else:
            o_b = gla_chunk_scan(jnp.flip(qh, 1), jnp.flip(kh, 1), jnp.flip(lfh, 1), jnp.flip(vh, 1))
            o_sum = o_sum + jnp.flip(o_b, 1)
    gate = jax.nn.silu(g.astype(jnp.float32)).reshape(B, S, HG_HEADS, HG_VAL_DIM)
    o = rms_norm(o_sum, norm_g) * gate
    return o.reshape(B, S, HG_V_WIDTH).astype(h.dtype) @ w_out


def linear_combine(e1, e2):
    a1, u1 = e1
    a2, u2 = e2
    return a1 * a2, a2 * u1 + u2


def rglru_block(h, w_in, conv_w, conv_b, w_a, b_a, w_x, b_x, lam, w_out):
    B, S, _ = h.shape
    y_br, x_br = jnp.split(h @ w_in, 2, axis=-1)
    y_br = jax.nn.gelu(y_br)
    xc = lax.conv_general_dilated(
        x_br, conv_w[:, None, :], window_strides=(1,), padding=[CONV_PAD],
        dimension_numbers=('NWC', 'WIO', 'NWC'), feature_group_count=LRU_WIDTH) + conv_b
    xf = xc.astype(jnp.float32)
    xb = xc.reshape(B, S, LRU_BLOCKS, LRU_BLOCK_W)
    h_sum = jnp.zeros((B, S, LRU_WIDTH), jnp.float32)
    for d in range(2):
        r = jax.nn.sigmoid(jnp.einsum('bsni,nij->bsnj', xb, w_a[d]).reshape(B, S, LRU_WIDTH).astype(jnp.float32)
                           + b_a[d].astype(jnp.float32))
        ig = jax.nn.sigmoid(jnp.einsum('bsni,nij->bsnj', xb, w_x[d]).reshape(B, S, LRU_WIDTH).astype(jnp.float32)
                            + b_x[d].astype(jnp.float32))
        log_a = -LRU_C * r * jax.nn.softplus(-lam[d].astype(jnp.float32))
        a = jnp.exp(log_a)
        u = jnp.sqrt(-jnp.expm1(2.0 * log_a)) * (ig * xf)
        _, hd = lax.associative_scan(linear_combine, (a, u), reverse=(d == 1), axis=1)
        h_sum = h_sum + hd
    return (h_sum.astype(h.dtype) * y_br) @ w_out


def swiglu(h, w_gu, w_down):
    gate, up = jnp.split(h @ w_gu, 2, axis=-1)
    return (jax.nn.silu(gate) * up) @ w_down


def moe(h, router, w_gu, w_down):
    B, S, D = h.shape
    t = h.reshape(B * S, D)
    logits = (t @ router).astype(jnp.float32)
    top_v, top_i = lax.top_k(logits, TOP_K)
    gates = jax.nn.softmax(top_v, axis=-1)
    combine = jnp.sum(jax.nn.one_hot(top_i, N_EXPERTS, dtype=jnp.float32) * gates[..., None], axis=1)
    y = jnp.zeros_like(t)
    for e in range(N_EXPERTS):
        y = y + combine[:, e:e + 1].astype(t.dtype) * swiglu(t, w_gu[e], w_down[e])
    return y.reshape(B, S, D)


def setup_inputs(seed: int = 0) -> dict:
    key = jax.random.key(seed)
    ks = jax.random.split(key, 32)
    f32 = jnp.float32
    D = D_MODEL

    def nrm(k, shape, scale):
        return jax.random.normal(k, shape, f32) * scale

    u = jax.random.uniform(ks[23], (N_LRU, 2, LRU_WIDTH), f32, 0.9, 0.999)
    p = u ** (1.0 / LRU_C)
    lru_lambda = jnp.log(p) - jnp.log1p(-p)
    return {
        'x': nrm(ks[0], (BATCH, SEQ, D), 1.0),
        'c': nrm(ks[1], (BATCH, D), 1.0),
        'ada_w': nrm(ks[2], (DEPTH, D, 6 * D), 0.5 * D ** -0.5),
        'ada_b': nrm(ks[3], (DEPTH, 6 * D), 0.02),
        'norm1_g': 1.0 + nrm(ks[4], (DEPTH, D), 0.02),
        'norm2_g': 1.0 + nrm(ks[5], (DEPTH, D), 0.02),
        'at_w_in': nrm(ks[6], (N_ATTN, D, 2 * DA_QK_WIDTH + DA_V_WIDTH), D ** -0.5),
        'at_q_norm': 1.0 + nrm(ks[7], (N_ATTN, DA_HEAD_DIM), 0.02),
        'at_k_norm': 1.0 + nrm(ks[8], (N_ATTN, DA_HEAD_DIM), 0.02),
        'at_lam': nrm(ks[9], (N_ATTN, 4, DA_HEAD_DIM), 0.1),
        'at_subln': 1.0 + nrm(ks[10], (N_ATTN, DA_V_DIM), 0.02),
        'at_w_out': nrm(ks[11], (N_ATTN, DA_V_WIDTH, D), DA_V_WIDTH ** -0.5),
        'hg_w_in': nrm(ks[12], (N_HGRN, D, 3 * HG_K_WIDTH + 2 * HG_V_WIDTH), D ** -0.5),
        'hg_lower_bounds': nrm(ks[13], (2, DEPTH, HG_K_WIDTH), 0.1),
        'hg_norm_g': 1.0 + nrm(ks[14], (N_HGRN, HG_VAL_DIM), 0.02),
        'hg_w_out': nrm(ks[15], (N_HGRN, HG_V_WIDTH, D), HG_V_WIDTH ** -0.5),
        'lru_w_in': nrm(ks[16], (N_LRU, D, 2 * LRU_WIDTH), D ** -0.5),
        'lru_conv_w': nrm(ks[17], (N_LRU, CONV_WIDTH, LRU_WIDTH), CONV_WIDTH ** -0.5),
        'lru_conv_b': nrm(ks[18], (N_LRU, LRU_WIDTH), 0.02),
        'lru_w_a': nrm(ks[19], (N_LRU, 2, LRU_BLOCKS, LRU_BLOCK_W, LRU_BLOCK_W), LRU_BLOCK_W ** -0.5),
        'lru_b_a': nrm(ks[20], (N_LRU, 2, LRU_WIDTH), 0.02),
        'lru_w_x': nrm(ks[21], (N_LRU, 2, LRU_BLOCKS, LRU_BLOCK_W, LRU_BLOCK_W), LRU_BLOCK_W ** -0.5),
        'lru_b_x': nrm(ks[22], (N_LRU, 2, LRU_WIDTH), 0.02),
        'lru_lambda': lru_lambda,
        'lru_w_out': nrm(ks[24], (N_LRU, LRU_WIDTH, D), LRU_WIDTH ** -0.5),
        'ff_w_gu': nrm(ks[25], (N_DENSE, D, 2 * D_FF), D ** -0.5),
        'ff_w_down': nrm(ks[26], (N_DENSE, D_FF, D), D_FF ** -0.5),
        'moe_router': nrm(ks[27], (N_MOE, D, N_EXPERTS), D ** -0.5),
        'moe_w_gu': nrm(ks[28], (N_MOE, N_EXPERTS, D, 2 * D_FF_EXPERT), D ** -0.5),
        'moe_w_down': nrm(ks[29], (N_MOE, N_EXPERTS, D_FF_EXPERT, D), D_FF_EXPERT ** -0.5),
    }


def reference(x, c, ada_w, ada_b, norm1_g, norm2_g,
              at_w_in, at_q_norm, at_k_norm, at_lam, at_subln, at_w_out,
              hg_w_in, hg_lower_bounds, hg_norm_g, hg_w_out,
              lru_w_in, lru_conv_w, lru_conv_b, lru_w_a, lru_b_a, lru_w_x, lru_b_x, lru_lambda, lru_w_out,
              ff_w_gu, ff_w_down, moe_router, moe_w_gu, moe_w_down):
    B, S, _ = x.shape
    cos, sin = rotary_tables(S, DA_HEAD_DIM)
    mods = jnp.einsum('bd,lde->lbe', jax.nn.silu(c), ada_w) + ada_b[:, None, :]
    for i in range(DEPTH):
        sh1, sc1, g1, sh2, sc2, g2 = jnp.split(mods[i], 6, axis=-1)
        hm = modulate(x, norm1_g[i], sh1, sc1)
        kind, j = i % N_MIXERS, i // N_MIXERS
        if kind == 0:
            lambda_init = 0.8 - 0.6 * math.exp(-0.3 * i)
            mix = diff_attention(hm, at_w_in[j], at_q_norm[j], at_k_norm[j], at_lam[j], at_subln[j],
                                 at_w_out[j], lambda_init, cos, sin)
        elif kind == 1:
            mix = hgrn2(hm, hg_w_in[j], hg_lower_bounds, hg_norm_g[j], hg_w_out[j], i)
        else:
            mix = rglru_block(hm, lru_w_in[j], lru_conv_w[j], lru_conv_b[j], lru_w_a[j], lru_b_a[j],
                              lru_w_x[j], lru_b_x[j], lru_lambda[j], lru_w_out[j])
        x = x + g1[:, None, :] * mix
        hf = modulate(x, norm2_g[i], sh2, sc2)
        m = i // 2
        if i % 2 == 0:
            ff = swiglu(hf, ff_w_gu[m], ff_w_down[m])
        else:
            ff = moe(hf, moe_router[m], moe_w_gu[m], moe_w_down[m])
        x = x + g2[:, None, :] * ff
    return x
```

```python
import functools
import math

import numpy as np
import jax
import jax.numpy as jnp
from jax import lax
from jax.experimental import pallas as pl
from jax.experimental.pallas import tpu as pltpu

F32 = jnp.float32
BF16 = jnp.bfloat16
HIGHEST = lax.Precision.HIGHEST

EPS = 1e-6
LANES = 128
VMEM_LIMIT_BYTES = 56 * 2**20

N_MIXERS = 3
DA_HEADS = 8
DA_HEAD_DIM = 64
DA_V_DIM = 2 * DA_HEAD_DIM
ROPE_THETA = 10000.0
HG_HEADS = 8
HG_CHUNK = 64
HG_SUB = 16
LRU_BLOCK_W = 128
CONV_WIDTH = 4
LRU_C = 8.0
N_EXPERTS = 8
TOP_K = 2


def _params(*sem):
    return pltpu.CompilerParams(dimension_semantics=sem, vmem_limit_bytes=VMEM_LIMIT_BYTES)


def _sigmoid(x):
    return 1.0 / (1.0 + jnp.exp(-x))


def _silu(x):
    return x * _sigmoid(x)


def _modulated(x, g, shift, scale):
    ms = jnp.mean(x * x, axis=-1, keepdims=True)
    return x * lax.rsqrt(ms + EPS) * g * (1.0 + scale) + shift


def _mods_kernel(c_ref, w_ref, b_ref, o_ref):
    c = c_ref[...]
    o_ref[...] = jnp.dot(_silu(c), w_ref[...], preferred_element_type=F32, precision=HIGHEST) + b_ref[...]


def _mods(c, ada_w, ada_b):
    depth, d, n = ada_w.shape
    b = c.shape[0]
    tn = 2048
    return pl.pallas_call(
        _mods_kernel,
        grid=(depth, n // tn),
        in_specs=[
            pl.BlockSpec((b, d), lambda l, j: (0, 0)),
            pl.BlockSpec((None, d, tn), lambda l, j: (l, 0, j)),
            pl.BlockSpec((None, 1, tn), lambda l, j: (l, 0, j)),
        ],
        out_specs=pl.BlockSpec((None, b, tn), lambda l, j: (l, 0, j)),
        out_shape=jax.ShapeDtypeStruct((depth, b, n), F32),
        compiler_params=_params("parallel", "parallel"),
        name="adaln_mods",
    )(c, ada_w, ada_b.reshape(depth, 1, n))


class _RowVec:
    def __init__(self, vec, tm, seq, seq_major):
        b, d = vec.shape
        if seq_major:
            assert tm % b == 0
            self.array = jnp.tile(vec, (tm // b, 1))
            self.spec = pl.BlockSpec((tm, d), lambda i, *_: (0, 0))
        else:
            assert seq % tm == 0
            tiles_per_batch = seq // tm
            self.array = vec.reshape(b, 1, d)
            self.spec = pl.BlockSpec((None, 1, d), lambda i, *_: (i // tiles_per_batch, 0, 0))


def _const_spec(shape):
    nd = len(shape)
    return pl.BlockSpec(shape, lambda *_: (0,) * nd)


def _nm_matmul_kernel(x_ref, g_ref, sh_ref, sc_ref, w_ref, o_ref, h_ref):
    @pl.when(pl.program_id(1) == 0)
    def _():
        h_ref[...] = _modulated(x_ref[...], g_ref[...], sh_ref[...], sc_ref[...]).astype(BF16)

    o_ref[...] = jnp.dot(h_ref[...], w_ref[...], preferred_element_type=F32).astype(o_ref.dtype)


def _nm_matmul(x, g, shift, scale, w, *, seq, seq_major, tm, tn, out_dtype):
    t, d = x.shape
    n = w.shape[1]
    assert t % tm == 0 and n % tn == 0
    sh = _RowVec(shift, tm, seq, seq_major)
    sc = _RowVec(scale, tm, seq, seq_major)
    return pl.pallas_call(
        _nm_matmul_kernel,
        grid=(t // tm, n // tn),
        in_specs=[
            pl.BlockSpec((tm, d), lambda i, j: (i, 0)),
            _const_spec((1, d)),
            sh.spec, sc.spec,
            pl.BlockSpec((d, tn), lambda i, j: (0, j)),
        ],
        out_specs=pl.BlockSpec((tm, tn), lambda i, j: (i, j)),
        out_shape=jax.ShapeDtypeStruct((t, n), out_dtype),
        scratch_shapes=[pltpu.VMEM((tm, d), BF16)],
        compiler_params=_params("parallel", "arbitrary"),
        name="modulate_matmul",
    )(x, g.reshape(1, d), sh.array, sc.array, w)


def _out_proj_kernel(*refs, prologue, n_in):
    ins = refs[:n_in]
    w_ref, x_ref, gate_ref, o_ref = refs[n_in:]
    a = prologue(*ins)
    mix = jnp.dot(a, w_ref[...], preferred_element_type=F32)
    o_ref[...] = x_ref[...] + gate_ref[...] * mix


def _out_proj(prologue, ins, in_specs, w, x, gate, *, seq, seq_major, tm):
    t, d = x.shape
    k = w.shape[0]
    gv = _RowVec(gate, tm, seq, seq_major)
    return pl.pallas_call(
        functools.partial(_out_proj_kernel, prologue=prologue, n_in=len(ins)),
        grid=(t // tm,),
        in_specs=list(in_specs) + [
            _const_spec((k, d)),
            pl.BlockSpec((tm, d), lambda i: (i, 0)),
            gv.spec,
        ],
        out_specs=pl.BlockSpec((tm, d), lambda i: (i, 0)),
        out_shape=jax.ShapeDtypeStruct((t, d), F32),
        compiler_params=_params("parallel"),
        name="out_proj_residual",
    )(*ins, w, x, gv.array)


def _rope_constants(seq):
    half = DA_HEAD_DIM // 2
    inv = 1.0 / (ROPE_THETA ** (jnp.arange(0, DA_HEAD_DIM, 2, dtype=F32) / DA_HEAD_DIM))
    ang = jnp.arange(seq, dtype=F32)[:, None] * inv[None, :]
    reps = LANES // half
    cos = jnp.tile(jnp.cos(ang), (1, reps))
    sin = jnp.tile(jnp.sin(ang), (1, reps))
    lane = np.arange(LANES)
    first = (lane % DA_HEAD_DIM) < half
    rot = np.zeros((LANES, LANES), np.float32)
    rot[(lane + half)[first], lane[first]] = -1.0
    rot[(lane - half)[~first], lane[~first]] = 1.0
    group = (lane[:, None] // DA_HEAD_DIM == lane[None, :] // DA_HEAD_DIM).astype(np.float32)
    return cos, sin, jnp.asarray(rot, BF16), jnp.asarray(group, BF16)


def _qk_prep_kernel(in_ref, gain_ref, cos_ref, sin_ref, rot_ref, grp_ref, o_ref):
    is_q = pl.program_id(1) == 0
    out_scale = jnp.where(is_q, DA_HEAD_DIM ** -0.5, 1.0).astype(F32)
    cos = cos_ref[...]
    sin = sin_ref[...]
    gain = gain_ref[...]
    for h in range(DA_HEADS):
        sl = slice(h * LANES, (h + 1) * LANES)
        y = in_ref[:, sl].astype(F32)
        ss = jnp.dot((y * y).astype(BF16), grp_ref[...], preferred_element_type=F32)
        n = y * lax.rsqrt(ss * (1.0 / DA_HEAD_DIM) + EPS) * gain
        r = jnp.dot(n.astype(BF16), rot_ref[...], preferred_element_type=F32)
        o_ref[:, sl] = ((n * cos + r * sin) * out_scale).astype(BF16)


def _qk_prep(qkv, q_gain, k_gain, cos, sin, rot, grp, *, seq, tm):
    t = qkv.shape[0]
    width = DA_HEADS * LANES
    gains = jnp.stack([jnp.tile(q_gain, 2), jnp.tile(k_gain, 2)]).reshape(2, 1, LANES).astype(F32)
    tiles_per_seq = seq // tm
    return pl.pallas_call(
        _qk_prep_kernel,
        grid=(t // tm, 2),
        in_specs=[
            pl.BlockSpec((tm, width), lambda i, j: (i, j)),
            pl.BlockSpec((None, 1, LANES), lambda i, j: (j, 0, 0)),
            pl.BlockSpec((tm, LANES), lambda i, j: (i % tiles_per_seq, 0)),
            pl.BlockSpec((tm, LANES), lambda i, j: (i % tiles_per_seq, 0)),
            _const_spec((LANES, LANES)),
            _const_spec((LANES, LANES)),
        ],
        out_specs=pl.BlockSpec((None, tm, width), lambda i, j: (j, i, 0)),
        out_shape=jax.ShapeDtypeStruct((2, t, width), BF16),
        compiler_params=_params("parallel", "parallel"),
        name="qk_norm_rope",
    )(qkv, gains, cos, sin, rot, grp)


def _flash_kernel(q_ref, k_ref, v_ref, lam_ref, subln_ref, o_ref, m_ref, l_ref, acc_ref, *,
                  tq, tk, lambda_init):
    q = q_ref[...]
    lane = lax.broadcasted_iota(jnp.int32, q.shape, 1)
    zero = jnp.zeros_like(q)
    qq = jnp.concatenate([jnp.where(lane < DA_HEAD_DIM, q, zero),
                          jnp.where(lane >= DA_HEAD_DIM, q, zero)], axis=0)
    m_ref[...] = jnp.full(m_ref.shape, -jnp.inf, F32)
    l_ref[...] = jnp.zeros(l_ref.shape, F32)
    acc_ref[...] = jnp.zeros(acc_ref.shape, F32)
    n_kv = k_ref.shape[0] // tk

    def body(j, carry):
        r0 = pl.multiple_of(j * tk, tk)
        k = k_ref[pl.ds(r0, tk), :]
        v = v_ref[pl.ds(r0, tk), :]
        s = lax.dot_general(qq, k, (((1,), (1,)), ((), ())), preferred_element_type=F32)
        m_old = m_ref[...]
        m_new = jnp.maximum(m_old, jnp.max(s, axis=1, keepdims=True))
        alpha = jnp.exp(m_old - m_new)
        p = jnp.exp(s - m_new)
        l_ref[...] = alpha * l_ref[...] + jnp.sum(p, axis=1, keepdims=True)
        acc_ref[...] = alpha * acc_ref[...] + jnp.dot(p.astype(BF16), v, preferred_element_type=F32)
        m_ref[...] = m_new
        return carry

    lax.fori_loop(0, n_kv, body, 0)

    lf = lam_ref[...]
    lam = (jnp.exp(jnp.sum(lf[0:1] * lf[1:2], axis=1, keepdims=True))
           - jnp.exp(jnp.sum(lf[2:3] * lf[3:4], axis=1, keepdims=True)) + lambda_init)
    o = acc_ref[0:tq] / l_ref[0:tq] - lam * (acc_ref[tq:2 * tq] / l_ref[tq:2 * tq])
    ms = jnp.mean(o * o, axis=-1, keepdims=True)
    o = o * lax.rsqrt(ms + EPS) * subln_ref[...] * (1.0 - lambda_init)
    o_ref[...] = o.astype(BF16)


def _flash(qk, qkv, lam_vecs, subln_g, *, batch, seq, lambda_init, tq, tk):
    t = batch * seq
    width = DA_HEADS * LANES
    q_tiles = seq // tq
    v_col0 = 2 * DA_HEADS
    return pl.pallas_call(
        functools.partial(_flash_kernel, tq=tq, tk=tk, lambda_init=lambda_init),
        grid=(batch, DA_HEADS, q_tiles),
        in_specs=[
            pl.BlockSpec((None, tq, LANES), lambda b, h, i: (0, b * q_tiles + i, h)),
            pl.BlockSpec((None, seq, LANES), lambda b, h, i: (1, b, h)),
            pl.BlockSpec((seq, LANES), lambda b, h, i: (b, v_col0 + h)),
            _const_spec((4, DA_HEAD_DIM)),
            _const_spec((1, DA_V_DIM)),
        ],
        out_specs=pl.BlockSpec((tq, LANES), lambda b, h, i: (b * q_tiles + i, h)),
        out_shape=jax.ShapeDtypeStruct((t, width), BF16),
        scratch_shapes=[
            pltpu.VMEM((2 * tq, 1), F32),
            pltpu.VMEM((2 * tq, 1), F32),
            pltpu.VMEM((2 * tq, LANES), F32),
        ],
        compiler_params=_params("parallel", "parallel", "arbitrary"),
        name="diff_flash_attention",
    )(qk, qk, qkv, lam_vecs.astype(F32), subln_g.reshape(1, DA_V_DIM).astype(F32))


def _identity_prologue(o_ref):
    return o_ref[...]


def _diff_attention_layer(x, g, shift, scale, gate, w_in, q_norm, k_norm, lam_vecs, subln_g, w_out,
                          lambda_init, *, batch, seq):
    t, d = x.shape
    qkv = _nm_matmul(x, g, shift, scale, w_in.astype(BF16), seq=seq, seq_major=False,
                     tm=min(512, seq), tn=1024, out_dtype=BF16)
    cos, sin, rot, grp = _rope_constants(seq)
    qk = _qk_prep(qkv, q_norm, k_norm, cos, sin, rot, grp, seq=seq, tm=min(512, seq))
    o = _flash(qk, qkv, lam_vecs, subln_g, batch=batch, seq=seq, lambda_init=lambda_init,
               tq=min(256, seq), tk=min(512, seq))
    tm = min(512, seq)
    return _out_proj(_identity_prologue, [o], [pl.BlockSpec((tm, o.shape[1]), lambda i: (i, 0))],
                     w_out.astype(BF16), x, gate, seq=seq, seq_major=False, tm=tm)


def _hgrn_scan_kernel(q_ref, f_ref, v_ref, lb_ref, o_ref, st_ref, *, reverse, layer):
    c = HG_CHUNK
    n_chunks = q_ref.shape[0] // c
    width = q_ref.shape[1]

    @pl.when(pl.program_id(1) == 0)
    def _():
        st_ref[...] = jnp.zeros(st_ref.shape, F32)

    lbw = lb_ref[...]
    e = jnp.exp(lbw - jnp.max(lbw, axis=0, keepdims=True))
    lb = jnp.sum(e[1:layer + 1], axis=0, keepdims=True) / jnp.sum(e, axis=0, keepdims=True)

    row = lax.broadcasted_iota(jnp.int32, (c, c), 0)
    col = lax.broadcasted_iota(jnp.int32, (c, c), 1)
    tri = (col >= row) if reverse else (col <= row)
    tri = tri.astype(F32)
    tcol = lax.broadcasted_iota(jnp.int32, (c, 1), 0)
    trow = tcol % HG_SUB
    n_sub = c // HG_SUB

    def chunk(ci, carry):
        cidx = (n_chunks - 1 - ci) if reverse else ci
        r0 = pl.multiple_of(cidx * c, c)
        qs = _silu(q_ref[pl.ds(r0, c), :])
        forget = lb + (1.0 - lb) * _sigmoid(f_ref[pl.ds(r0, c), :])
        kk = 1.0 - forget
        logf = jnp.log(forget)
        v = v_ref[pl.ds(r0, c), :]
        vb = v.astype(BF16)
        b = jnp.dot(tri, logf, preferred_element_type=F32, precision=HIGHEST)
        b_edge = b[0:1] if reverse else b[c - 1:c]
        q_in = (qs * jnp.exp(b)).astype(BF16)
        k_out = (kk * jnp.exp(b_edge - b)).astype(BF16)
        dec = jnp.exp(b_edge)

        diag = []
        for delta in range(HG_SUB):
            if delta == 0:
                term = qs * kk
            else:
                shift = (c - delta) if reverse else delta
                valid = (trow + delta <= HG_SUB - 1) if reverse else (trow >= delta)
                k_s = pltpu.roll(kk, shift, 0)
                b_s = pltpu.roll(b, shift, 0)
                term = qs * k_s * jnp.exp(jnp.where(valid, b - b_s, -jnp.inf))
            diag.append(term)

        for h in range(HG_HEADS):
            sl = slice(h * LANES, (h + 1) * LANES)
            a = jnp.zeros((c, c), F32)
            for delta in range(HG_SUB):
                a_d = jnp.sum(diag[delta][:, sl], axis=1, keepdims=True)
                hit = (col == row + delta) if reverse else (col == row - delta)
                a = a + jnp.where(hit, a_d, 0.0)
            blocks = []
            for i in range(n_sub):
                rows = slice(i * HG_SUB, (i + 1) * HG_SUB)
                ref_row = (i + 1) * HG_SUB if reverse else i * HG_SUB - 1
                if ref_row < 0 or ref_row >= c:
                    blocks.append(jnp.zeros((HG_SUB, c), F32))
                    continue
                passed = (tcol >= ref_row) if reverse else (tcol <= ref_row)
                b_ref_row = b[ref_row:ref_row + 1, sl]
                q_t = (qs[rows, sl] * jnp.exp(b[rows, sl] - b_ref_row)).astype(BF16)
                k_t = (kk[:, sl] * jnp.exp(jnp.where(passed, b_ref_row - b[:, sl], -jnp.inf))).astype(BF16)
                blocks.append(lax.dot_general(q_t, k_t, (((1,), (1,)), ((), ())),
                                              preferred_element_type=F32))
            a = a + jnp.concatenate(blocks, axis=0)
            st = st_ref[h]
            o_h = jnp.dot(a.astype(BF16), vb[:, sl], preferred_element_type=F32)
            o_h = o_h + lax.dot_general(q_in[:, sl], st.astype(BF16), (((1,), (1,)), ((), ())),
                                        preferred_element_type=F32)
            o_ref[pl.ds(r0, c), sl] = o_h
            upd = lax.dot_general(vb[:, sl], k_out[:, sl], (((0,), (0,)), ((), ())),
                                  preferred_element_type=F32)
            st_ref[h] = st * dec[:, sl] + upd
        return carry

    lax.fori_loop(0, n_chunks, chunk, 0)


def _hgrn_scan(proj, lower_bounds, *, batch, seq, layer, reverse, tt):
    t = batch * seq
    width = HG_HEADS * LANES
    nt = seq // tt
    depth = lower_bounds.shape[1]
    d_idx = 1 if reverse else 0

    def rows(b, i):
        return b * nt + ((nt - 1 - i) if reverse else i)

    return pl.pallas_call(
        functools.partial(_hgrn_scan_kernel, reverse=reverse, layer=layer),
        grid=(batch, nt),
        in_specs=[
            pl.BlockSpec((tt, width), lambda b, i: (rows(b, i), 0)),
            pl.BlockSpec((tt, width), lambda b, i: (rows(b, i), 1 + d_idx)),
            pl.BlockSpec((tt, width), lambda b, i: (rows(b, i), 3)),
            pl.BlockSpec((None, depth, width), lambda b, i: (d_idx, 0, 0)),
        ],
        out_specs=pl.BlockSpec((tt, width), lambda b, i: (rows(b, i), 0)),
        out_shape=jax.ShapeDtypeStruct((t, width), F32),
        scratch_shapes=[pltpu.VMEM((HG_HEADS, LANES, LANES), F32)],
        compiler_params=_params("parallel", "arbitrary"),
        name="hgrn2_scan_bwd" if reverse else "hgrn2_scan_fwd",
    )(proj, proj, proj, lower_bounds.astype(F32))


def _hgrn_prologue(of_ref, ob_ref, g_ref, ng_ref):
    ng = ng_ref[...]
    outs = []
    for h in range(HG_HEADS):
        sl = slice(h * LANES, (h + 1) * LANES)
        o = of_ref[:, sl] + ob_ref[:, sl]
        ms = jnp.mean(o * o, axis=-1, keepdims=True)
        outs.append((o * lax.rsqrt(ms + EPS) * ng * _silu(g_ref[:, sl])).astype(BF16))
    return jnp.concatenate(outs, axis=1)


def _hgrn_layer(x, g, shift, scale, gate, w_in, lower_bounds, norm_g, w_out, layer, *, batch, seq):
    width = HG_HEADS * LANES
    proj = _nm_matmul(x, g, shift, scale, w_in.astype(BF16), seq=seq, seq_major=False,
                      tm=min(512, seq), tn=1024, out_dtype=F32)
    tt = min(512, seq)
    o_f = _hgrn_scan(proj, lower_bounds, batch=batch, seq=seq, layer=layer, reverse=False, tt=tt)
    o_b = _hgrn_scan(proj, lower_bounds, batch=batch, seq=seq, layer=layer, reverse=True, tt=tt)
    tm = min(512, seq)
    ins = [o_f, o_b, proj, norm_g.reshape(1, LANES).astype(F32)]
    specs = [
        pl.BlockSpec((tm, width), lambda i: (i, 0)),
        pl.BlockSpec((tm, width), lambda i: (i, 0)),
        pl.BlockSpec((tm, width), lambda i: (i, 4)),
        _const_spec((1, LANES)),
    ]
    return _out_proj(_hgrn_prologue, ins, specs, w_out.astype(BF16), x, gate,
                     seq=seq, seq_major=False, tm=tm)


def _lru_scan_kernel(prev_ref, main_ref, next_ref, cw_ref, cb_ref, wg_ref, ba_ref, bx_ref, lam_ref,
                     o_ref, a_ref, u_ref, h_ref, *, batch, nt):
    d = pl.program_id(0)
    i = pl.program_id(1)
    ti = i + d * (nt - 1 - 2 * i)
    rows, width = main_ref.shape
    tt = rows // batch

    @pl.when(i == 0)
    def _():
        h_ref[...] = jnp.zeros(h_ref.shape, F32)

    prev = jnp.where(ti == 0, 0.0, prev_ref[...])
    nxt = jnp.where(ti == nt - 1, 0.0, next_ref[...])
    ext = jnp.concatenate([prev, main_ref[...], nxt], axis=0)
    cw = cw_ref[...]
    xc = cb_ref[...] + cw[0:1] * ext[0:rows]
    for j in range(1, CONV_WIDTH):
        xc = xc + cw[j:j + 1] * ext[j * batch:j * batch + rows]

    neg_softplus = -LRU_C * (jnp.maximum(-lam_ref[...], 0.0) + jnp.log1p(jnp.exp(-jnp.abs(lam_ref[...]))))
    for n in range(width // LRU_BLOCK_W):
        sl = slice(n * LRU_BLOCK_W, (n + 1) * LRU_BLOCK_W)
        xb = xc[:, sl]
        gates = jnp.dot(xb.astype(BF16), wg_ref[n], preferred_element_type=F32)
        r = _sigmoid(gates[:, :LRU_BLOCK_W] + ba_ref[:, sl])
        ig = _sigmoid(gates[:, LRU_BLOCK_W:] + bx_ref[:, sl])
        a = jnp.exp(r * neg_softplus[:, sl])
        a_ref[:, sl] = a
        u_ref[:, sl] = jnp.sqrt(1.0 - a * a) * (ig * xb)

    def step(t, h):
        tl = t + d * (tt - 1 - 2 * t)
        r0 = pl.multiple_of(tl * batch, batch)
        h = a_ref[pl.ds(r0, batch), :] * h + u_ref[pl.ds(r0, batch), :]
        o_ref[pl.ds(r0, batch), :] = h
        return h

    h_ref[...] = lax.fori_loop(0, tt, step, h_ref[...])


def _lru_scan(proj, conv_w, conv_b, w_a, b_a, w_x, b_x, lam, *, batch, seq, tt):
    t = batch * seq
    width = conv_w.shape[1]
    n_blocks = width // LRU_BLOCK_W
    nt = seq // tt
    rows = tt * batch
    assert tt % 2 == 0

    def tile(d, i):
        return i + d * (nt - 1 - 2 * i)

    wg = jnp.concatenate([w_a, w_x], axis=-1).astype(BF16)
    vec = lambda a: a.reshape(2, 1, width).astype(F32)
    return pl.pallas_call(
        functools.partial(_lru_scan_kernel, batch=batch, nt=nt),
        grid=(2, nt),
        in_specs=[
            pl.BlockSpec((batch, width), lambda d, i: (jnp.maximum(tile(d, i) * tt - 1, 0), 1)),
            pl.BlockSpec((rows, width), lambda d, i: (tile(d, i), 1)),
            pl.BlockSpec((2 * batch, width),
                         lambda d, i: (jnp.minimum((tile(d, i) + 1) * (tt // 2), seq // 2 - 1), 1)),
            _const_spec((CONV_WIDTH, width)),
            _const_spec((1, width)),
            pl.BlockSpec((None, n_blocks, LRU_BLOCK_W, 2 * LRU_BLOCK_W), lambda d, i: (d, 0, 0, 0)),
            pl.BlockSpec((None, 1, width), lambda d, i: (d, 0, 0)),
            pl.BlockSpec((None, 1, width), lambda d, i: (d, 0, 0)),
            pl.BlockSpec((None, 1, width), lambda d, i: (d, 0, 0)),
        ],
        out_specs=pl.BlockSpec((None, rows, width), lambda d, i: (d, tile(d, i), 0)),
        out_shape=jax.ShapeDtypeStruct((2, t, width), F32),
        scratch_shapes=[
            pltpu.VMEM((rows, width), F32),
            pltpu.VMEM((rows, width), F32),
            pltpu.VMEM((batch, width), F32),
        ],
        compiler_params=_params("arbitrary", "arbitrary"),
        name="rglru_scan",
    )(proj, proj, proj, conv_w.astype(F32), conv_b.reshape(1, width).astype(F32), wg,
      vec(b_a), vec(b_x), vec(lam))


def _gelu_tanh(x):
    return 0.5 * x * (1.0 + jnp.tanh(math.sqrt(2.0 / math.pi) * (x + 0.044715 * (x * x * x))))


def _lru_prologue(hf_ref, hb_ref, y_ref):
    return ((hf_ref[...] + hb_ref[...]) * _gelu_tanh(y_ref[...])).astype(BF16)


def _rglru_layer(x, g, shift, scale, gate, w_in, conv_w, conv_b, w_a, b_a, w_x, b_x, lam, w_out,
                 *, batch, seq):
    width = conv_w.shape[1]
    tm = 256
    proj = _nm_matmul(x, g, shift, scale, w_in.astype(BF16), seq=seq, seq_major=True,
                      tm=tm, tn=width, out_dtype=F32)
    h2 = _lru_scan(proj, conv_w, conv_b, w_a, b_a, w_x, b_x, lam, batch=batch, seq=seq, tt=min(64, seq))
    specs = [
        pl.BlockSpec((None, tm, width), lambda i: (0, i, 0)),
        pl.BlockSpec((None, tm, width), lambda i: (1, i, 0)),
        pl.BlockSpec((tm, width), lambda i: (i, 0)),
    ]
    return _out_proj(_lru_prologue, [h2, h2, proj], specs, w_out.astype(BF16), x, gate,
                     seq=seq, seq_major=True, tm=tm)


def _ffn_kernel(x_ref, g_ref, sh_ref, sc_ref, wg_ref, wu_ref, wd_ref, gate_ref, o_ref, h_ref, acc_ref):
    j = pl.program_id(1)

    @pl.when(j == 0)
    def _():
        h_ref[...] = _modulated(x_ref[...], g_ref[...], sh_ref[...], sc_ref[...]).astype(BF16)
        acc_ref[...] = jnp.zeros(acc_ref.shape, F32)

    h = h_ref[...]
    gt = jnp.dot(h, wg_ref[...], preferred_element_type=F32)
    up = jnp.dot(h, wu_ref[...], preferred_element_type=F32)
    act = (_silu(gt) * up).astype(BF16)
    acc_ref[...] += jnp.dot(act, wd_ref[...], preferred_element_type=F32)

    @pl.when(j == pl.num_programs(1) - 1)
    def _():
        o_ref[...] = x_ref[...] + gate_ref[...] * acc_ref[...]


def _dense_ffn(x, g, shift, scale, gate, w_gu, w_down, *, batch, seq, x_seq_major, out_seq_major, tm, tf):
    t = batch * seq
    d = w_gu.shape[0]
    f = w_down.shape[0]
    assert f % tf == 0 and seq % tm == 0
    nf = f // tf
    tiles_per_seq = seq // tm
    sh = _RowVec(shift, tm, seq, False)
    sc = _RowVec(scale, tm, seq, False)
    gv = _RowVec(gate, tm, seq, False)

    def view(a, seq_major):
        if seq_major:
            return a.reshape(seq, batch * d), pl.BlockSpec(
                (tm, d), lambda i, j: (i % tiles_per_seq, i // tiles_per_seq))
        return a, pl.BlockSpec((tm, d), lambda i, j: (i, 0))

    x_arr, x_spec = view(x, x_seq_major)
    out_shape = (seq, batch * d) if out_seq_major else (t, d)
    _, o_spec = view(x, out_seq_major)
    out = pl.pallas_call(
        _ffn_kernel,
        grid=(t // tm, nf),
        in_specs=[
            x_spec,
            _const_spec((1, d)),
            sh.spec, sc.spec,
            pl.BlockSpec((d, tf), lambda i, j: (0, j)),
            pl.BlockSpec((d, tf), lambda i, j: (0, nf + j)),
            pl.BlockSpec((tf, d), lambda i, j: (j, 0)),
            gv.spec,
        ],
        out_specs=o_spec,
        out_shape=jax.ShapeDtypeStruct(out_shape, F32),
        scratch_shapes=[pltpu.VMEM((tm, d), BF16), pltpu.VMEM((tm, d), F32)],
        compiler_params=_params("parallel", "arbitrary"),
        name="dense_swiglu",
    )(x_arr, g.reshape(1, d), sh.array, sc.array, w_gu, w_gu, w_down, gv.array)
    return out.reshape(t, d)


def _route_kernel(x_ref, g_ref, sh_ref, sc_ref, r_ref, hf_ref, meta_ref, cnt_ref, carry_ref):
    tm = x_ref.shape[0]

    @pl.when(pl.program_id(0) == 0)
    def _():
        carry_ref[...] = jnp.zeros(carry_ref.shape, F32)

    hf = _modulated(x_ref[...], g_ref[...], sh_ref[...], sc_ref[...])
    hf_ref[...] = hf
    logits = jnp.dot(hf, r_ref[...], preferred_element_type=F32, precision=HIGHEST)
    lane = lax.broadcasted_iota(jnp.int32, logits.shape, 1).astype(F32)
    logits = jnp.where(lane < N_EXPERTS, logits, -jnp.inf)
    m1 = jnp.max(logits, axis=1, keepdims=True)
    e1 = jnp.min(jnp.where(logits == m1, lane, float(LANES)), axis=1, keepdims=True)
    rest = jnp.where(lane == e1, -jnp.inf, logits)
    m2 = jnp.max(rest, axis=1, keepdims=True)
    e2 = jnp.min(jnp.where(rest == m2, lane, float(LANES)), axis=1, keepdims=True)
    gate2 = 1.0 / (1.0 + jnp.exp(m1 - m2))
    gate1 = 1.0 - gate2
    hot1 = (lane == e1).astype(F32)
    hot2 = (lane == e2).astype(F32)
    sel = hot1 + hot2
    row = lax.broadcasted_iota(jnp.int32, (tm, tm), 0)
    col = lax.broadcasted_iota(jnp.int32, (tm, tm), 1)
    before = (col < row).astype(BF16)
    seen = carry_ref[...] + jnp.dot(before, sel.astype(BF16), preferred_element_type=F32)
    rank1 = jnp.sum(hot1 * seen, axis=1, keepdims=True)
    rank2 = jnp.sum(hot2 * seen, axis=1, keepdims=True)
    carry_ref[...] += jnp.sum(sel, axis=0, keepdims=True)
    cnt_ref[...] = carry_ref[...]
    meta = jnp.zeros(logits.shape, F32)
    for k, val in enumerate((e1, e2, gate1, gate2, rank1, rank2)):
        meta = jnp.where(lane == float(k), val, meta)
    meta_ref[...] = meta


def _route(x, g, shift, scale, router, *, seq, tm):
    t, d = x.shape
    sh = _RowVec(shift, tm, seq, False)
    sc = _RowVec(scale, tm, seq, False)
    router_pad = jnp.zeros((d, LANES), F32).at[:, :N_EXPERTS].set(router.astype(F32))
    return pl.pallas_call(
        _route_kernel,
        grid=(t // tm,),
        in_specs=[
            pl.BlockSpec((tm, d), lambda i: (i, 0)),
            _const_spec((1, d)),
            sh.spec, sc.spec,
            _const_spec((d, LANES)),
        ],
        out_specs=[
            pl.BlockSpec((tm, d), lambda i: (i, 0)),
            pl.BlockSpec((tm, LANES), lambda i: (i, 0)),
            _const_spec((1, LANES)),
        ],
        out_shape=[
            jax.ShapeDtypeStruct((t, d), F32),
            jax.ShapeDtypeStruct((t, LANES), F32),
            jax.ShapeDtypeStruct((1, LANES), F32),
        ],
        scratch_shapes=[pltpu.VMEM((1, LANES), F32)],
        compiler_params=_params("arbitrary"),
        name="moe_route",
    )(x, g.reshape(1, d), sh.array, sc.array, router_pad)


def _row_copy(src_hbm, src_row, dst_vmem, dst_row, sem):
    return pltpu.make_async_copy(src_hbm.at[pl.ds(src_row, 1)], dst_vmem.at[pl.ds(dst_row, 1)], sem)


def _start_row_gather(idx_smem, n, src_hbm, dst_vmem, sem):
    def start(r, c):
        _row_copy(src_hbm, idx_smem[r], dst_vmem, r, sem).start()
        return c

    lax.fori_loop(0, n, start, 0)


def _wait_row_gather(n, src_hbm, dst_vmem, sem):
    def wait(r, c):
        _row_copy(src_hbm, 0, dst_vmem, r, sem).wait()
        return c

    lax.fori_loop(0, n, wait, 0)


def _expert_kernel(te_ref, nu_ref, src_ref, hf_ref, wg_ref, wu_ref, wd_ref, o_ref,
                   idx_ref, xg_ref, h_ref, acc_ref, sem_idx, sem_rows):
    i = pl.program_id(0)
    j = pl.program_id(1)
    tm = xg_ref.shape[0]
    used = i < nu_ref[0]

    @pl.when(jnp.logical_and(used, j == 0))
    def _():
        cp = pltpu.make_async_copy(src_ref.at[i], idx_ref, sem_idx)
        cp.start()
        cp.wait()
        _start_row_gather(idx_ref, tm, hf_ref, xg_ref, sem_rows)
        _wait_row_gather(tm, hf_ref, xg_ref, sem_rows)
        h_ref[...] = xg_ref[...].astype(BF16)
        acc_ref[...] = jnp.zeros(acc_ref.shape, F32)

    @pl.when(used)
    def _():
        h = h_ref[...]
        gt = jnp.dot(h, wg_ref[...], preferred_element_type=F32)
        up = jnp.dot(h, wu_ref[...], preferred_element_type=F32)
        act = (_silu(gt) * up).astype(BF16)
        acc_ref[...] += jnp.dot(act, wd_ref[...], preferred_element_type=F32)

    @pl.when(j == pl.num_programs(1) - 1)
    def _():
        o_ref[...] = jnp.where(used, acc_ref[...], 0.0)


def _experts(hf, src_tiles, tile_expert, n_used, w_gu, w_down, *, tm, tf):
    t, d = hf.shape
    n_tiles = src_tiles.shape[0]
    f = w_down.shape[1]
    nf = f // tf

    def jj(i, j, nu):
        return jnp.where(i < nu[0], j, nf - 1)

    grid_spec = pltpu.PrefetchScalarGridSpec(
        num_scalar_prefetch=2,
        grid=(n_tiles, nf),
        in_specs=[
            pl.BlockSpec(memory_space=pl.ANY),
            pl.BlockSpec(memory_space=pl.ANY),
            pl.BlockSpec((None, d, tf), lambda i, j, te, nu: (te[i], 0, jj(i, j, nu))),
            pl.BlockSpec((None, d, tf), lambda i, j, te, nu: (te[i], 0, nf + jj(i, j, nu))),
            pl.BlockSpec((None, tf, d), lambda i, j, te, nu: (te[i], jj(i, j, nu), 0)),
        ],
        out_specs=pl.BlockSpec((tm, d), lambda i, j, te, nu: (i, 0)),
        scratch_shapes=[
            pltpu.SMEM((tm,), jnp.int32),
            pltpu.VMEM((tm, d), F32),
            pltpu.VMEM((tm, d), BF16),
            pltpu.VMEM((tm, d), F32),
            pltpu.SemaphoreType.DMA,
            pltpu.SemaphoreType.DMA,
        ],
    )
    return pl.pallas_call(
        _expert_kernel,
        grid_spec=grid_spec,
        out_shape=jax.ShapeDtypeStruct((n_tiles * tm, d), F32),
        compiler_params=_params("arbitrary", "arbitrary"),
        name="moe_experts",
    )(tile_expert, n_used, src_tiles, hf, w_gu, w_gu, w_down)


def _combine_kernel(d1_ref, d2_ref, ys_ref, x_ref, meta_ref, gate_ref, o_ref,
                    i1_ref, i2_ref, a_ref, b_ref, sem_idx, sem_a, sem_b):
    i = pl.program_id(0)
    tm = x_ref.shape[0]
    c1 = pltpu.make_async_copy(d1_ref.at[i], i1_ref, sem_idx.at[0])
    c2 = pltpu.make_async_copy(d2_ref.at[i], i2_ref, sem_idx.at[1])
    c1.start()
    c2.start()
    c1.wait()
    c2.wait()
    _start_row_gather(i1_ref, tm, ys_ref, a_ref, sem_a)
    _start_row_gather(i2_ref, tm, ys_ref, b_ref, sem_b)
    _wait_row_gather(tm, ys_ref, a_ref, sem_a)
    _wait_row_gather(tm, ys_ref, b_ref, sem_b)
    meta = meta_ref[...]
    y = meta[:, 2:3] * a_ref[...] + meta[:, 3:4] * b_ref[...]
    o_ref[...] = x_ref[...] + gate_ref[...] * y


def _combine(ys, dest1, dest2, x, meta, gate, *, batch, seq, out_seq_major, tm):
    t, d = x.shape
    tiles_per_seq = seq // tm
    gv = _RowVec(gate, tm, seq, False)
    if out_seq_major:
        out_shape = (seq, batch * d)
        o_spec = pl.BlockSpec((tm, d), lambda i: (i % tiles_per_seq, i // tiles_per_seq))
    else:
        out_shape = (t, d)
        o_spec = pl.BlockSpec((tm, d), lambda i: (i, 0))
    out = pl.pallas_call(
        _combine_kernel,
        grid=(t // tm,),
        in_specs=[
            pl.BlockSpec(memory_space=pl.ANY),
            pl.BlockSpec(memory_space=pl.ANY),
            pl.BlockSpec(memory_space=pl.ANY),
            pl.BlockSpec((tm, d), lambda i: (i, 0)),
            pl.BlockSpec((tm, LANES), lambda i: (i, 0)),
            gv.spec,
        ],
        out_specs=o_spec,
        out_shape=jax.ShapeDtypeStruct(out_shape, F32),
        scratch_shapes=[
            pltpu.SMEM((tm,), jnp.int32),
            pltpu.SMEM((tm,), jnp.int32),
            pltpu.VMEM((tm, d), F32),
            pltpu.VMEM((tm, d), F32),
            pltpu.SemaphoreType.DMA((2,)),
            pltpu.SemaphoreType.DMA,
            pltpu.SemaphoreType.DMA,
        ],
        compiler_params=_params("arbitrary"),
        name="moe_combine",
    )(dest1.reshape(t // tm, tm), dest2.reshape(t // tm, tm), ys, x, meta, gv.array)
    return out.reshape(t, d)


def _moe_ffn(x, g, shift, scale, gate, router, w_gu, w_down, *, batch, seq, out_seq_major):
    t, d = x.shape
    tm_route = min(512, seq)
    tm_e = min(1024, seq)
    hf, meta, counts = _route(x, g, shift, scale, router, seq=seq, tm=tm_route)

    e1 = meta[:, 0].astype(jnp.int32)
    e2 = meta[:, 1].astype(jnp.int32)
    r1 = meta[:, 4].astype(jnp.int32)
    r2 = meta[:, 5].astype(jnp.int32)
    cnt = counts[0, :N_EXPERTS].astype(jnp.int32)
    padded = (cnt + tm_e - 1) // tm_e * tm_e
    ends = jnp.cumsum(padded)
    starts = ends - padded
    dest1 = starts[e1] + r1
    dest2 = starts[e2] + r2
    n_tiles = TOP_K * t // tm_e + N_EXPERTS
    tok = jnp.arange(t, dtype=jnp.int32)
    src = jnp.zeros((n_tiles * tm_e,), jnp.int32).at[dest1].set(tok).at[dest2].set(tok)
    n_used = (ends[-1] // tm_e).astype(jnp.int32)
    tile_start = jnp.arange(n_tiles, dtype=jnp.int32) * tm_e
    tile_expert = jnp.minimum(jnp.searchsorted(ends, tile_start, side="right"), N_EXPERTS - 1)
    tile_expert = jnp.where(jnp.arange(n_tiles) < n_used, tile_expert,
                            tile_expert[jnp.maximum(n_used - 1, 0)]).astype(jnp.int32)

    ys = _experts(hf, src.reshape(n_tiles, tm_e), tile_expert, n_used.reshape(1),
                  w_gu.astype(BF16), w_down.astype(BF16), tm=tm_e, tf=512)
    return _combine(ys, dest1, dest2, x, meta, gate, batch=batch, seq=seq,
                    out_seq_major=out_seq_major, tm=min(256, seq))


def kernel(x, c, ada_w, ada_b, norm1_g, norm2_g, at_w_in, at_q_norm, at_k_norm, at_lam, at_subln, at_w_out, hg_w_in, hg_lower_bounds, hg_norm_g, hg_w_out, lru_w_in, lru_conv_w, lru_conv_b, lru_w_a, lru_b_a, lru_w_x, lru_b_x, lru_lambda, lru_w_out, ff_w_gu, ff_w_down, moe_router, moe_w_gu, moe_w_down):
    batch, seq, d = x.shape
    depth = ada_w.shape[0]
    t = batch * seq
    mods = _mods(c, ada_w, ada_b)
    xr = x.reshape(t, d)
    seq_major = False
    for i in range(depth):
        sh1, sc1, g1, sh2, sc2, g2 = [mods[i, :, k * d:(k + 1) * d] for k in range(6)]
        kind, j = i % N_MIXERS, i // N_MIXERS
        next_is_lru = (i + 1 < depth) and ((i + 1) % N_MIXERS == 2)
        if kind == 0:
            assert not seq_major
            lambda_init = 0.8 - 0.6 * math.exp(-0.3 * i)
            xr = _diff_attention_layer(xr, norm1_g[i], sh1, sc1, g1, at_w_in[j], at_q_norm[j], at_k_norm[j],
                                       at_lam[j], at_subln[j], at_w_out[j], lambda_init, batch=batch, seq=seq)
        elif kind == 1:
            assert not seq_major
            xr = _hgrn_layer(xr, norm1_g[i], sh1, sc1, g1, hg_w_in[j], hg_lower_bounds, hg_norm_g[j],
                             hg_w_out[j], i, batch=batch, seq=seq)
        else:
            if not seq_major:
                xr = xr.reshape(batch, seq, d).transpose(1, 0, 2).reshape(t, d)
                seq_major = True
            xr = _rglru_layer(xr, norm1_g[i], sh1, sc1, g1, lru_w_in[j], lru_conv_w[j], lru_conv_b[j],
                              lru_w_a[j], lru_b_a[j], lru_w_x[j], lru_b_x[j], lru_lambda[j], lru_w_out[j],
                              batch=batch, seq=seq)
        m = i // 2
        if i % 2 == 0:
            xr = _dense_ffn(xr, norm2_g[i], sh2, sc2, g2, ff_w_gu[m].astype(BF16), ff_w_down[m].astype(BF16),
                            batch=batch, seq=seq, x_seq_major=seq_major, out_seq_major=next_is_lru,
                            tm=min(1024, seq), tf=ff_w_down.shape[1] // 2)
            seq_major = next_is_lru
        else:
            if seq_major:
                xr = xr.reshape(seq, batch, d).transpose(1, 0, 2).reshape(t, d)
                seq_major = False
            xr = _moe_ffn(xr, norm2_g[i], sh2, sc2, g2, moe_router[m], moe_w_gu[m], moe_w_down[m],
                          batch=batch, seq=seq, out_seq_major=next_is_lru)
            seq_major = next_is_lru
    if seq_major:
        xr = xr.reshape(seq, batch, d).transpose(1, 0, 2).reshape(t, d)
    return xr.reshape(batch, seq, d)
```

```python
import functools
import math

import numpy as np
import jax
import jax.numpy as jnp
from jax import lax
from jax.experimental import pallas as pl
from jax.experimental.pallas import tpu as pltpu

F32 = jnp.float32
BF16 = jnp.bfloat16
HIGHEST = lax.Precision.HIGHEST

EPS = 1e-6
LANES = 128
VMEM_LIMIT_BYTES = 56 * 2**20

N_MIXERS = 3
DA_HEADS = 8
DA_HEAD_DIM = 64
DA_V_DIM = 2 * DA_HEAD_DIM
ROPE_THETA = 10000.0
HG_HEADS = 8
HG_CHUNK = 64
HG_SUB = 16
LRU_BLOCK_W = 128
CONV_WIDTH = 4
LRU_C = 8.0
N_EXPERTS = 8
TOP_K = 2


def _params(*sem):
    return pltpu.CompilerParams(dimension_semantics=sem, vmem_limit_bytes=VMEM_LIMIT_BYTES)


def _sigmoid(x):
    return 1.0 / (1.0 + jnp.exp(-x))


def _silu(x):
    return x * _sigmoid(x)


def _modulated(x, g, shift, scale):
    ms = jnp.mean(x * x, axis=-1, keepdims=True)
    return x * lax.rsqrt(ms + EPS) * g * (1.0 + scale) + shift


def _mods_kernel(c_ref, w_ref, b_ref, o_ref):
    c = c_ref[...]
    o_ref[...] = jnp.dot(_silu(c), w_ref[...], preferred_element_type=F32, precision=HIGHEST) + b_ref[...]


def _mods(c, ada_w, ada_b):
    depth, d, n = ada_w.shape
    b = c.shape[0]
    tn = 2048
    return pl.pallas_call(
        _mods_kernel,
        grid=(depth, n // tn),
        in_specs=[
            pl.BlockSpec((b, d), lambda l, j: (0, 0)),
            pl.BlockSpec((None, d, tn), lambda l, j: (l, 0, j)),
            pl.BlockSpec((None, 1, tn), lambda l, j: (l, 0, j)),
        ],
        out_specs=pl.BlockSpec((None, b, tn), lambda l, j: (l, 0, j)),
        out_shape=jax.ShapeDtypeStruct((depth, b, n), F32),
        compiler_params=_params("parallel", "parallel"),
        name="adaln_mods",
    )(c, ada_w, ada_b.reshape(depth, 1, n))


class _RowVec:
    def __init__(self, vec, tm, seq, seq_major):
        b, d = vec.shape
        if seq_major:
            assert tm % b == 0
            self.array = jnp.tile(vec, (tm // b, 1))
            self.spec = pl.BlockSpec((tm, d), lambda i, *_: (0, 0))
        else:
            assert seq % tm == 0
            tiles_per_batch = seq // tm
            self.array = vec.reshape(b, 1, d)
            self.spec = pl.BlockSpec((None, 1, d), lambda i, *_: (i // tiles_per_batch, 0, 0))


def _const_spec(shape):
    nd = len(shape)
    return pl.BlockSpec(shape, lambda *_: (0,) * nd)


def _nm_matmul_kernel(x_ref, g_ref, sh_ref, sc_ref, w_ref, o_ref, h_ref):
    @pl.when(pl.program_id(1) == 0)
    def _():
        h_ref[...] = _modulated(x_ref[...], g_ref[...], sh_ref[...], sc_ref[...]).astype(BF16)

    o_ref[...] = jnp.dot(h_ref[...], w_ref[...], preferred_element_type=F32).astype(o_ref.dtype)


def _nm_matmul(x, g, shift, scale, w, *, seq, seq_major, tm, tn, out_dtype):
    t, d = x.shape
    n = w.shape[1]
    assert t % tm == 0 and n % tn == 0
    sh = _RowVec(shift, tm, seq, seq_major)
    sc = _RowVec(scale, tm, seq, seq_major)
    return pl.pallas_call(
        _nm_matmul_kernel,
        grid=(t // tm, n // tn),
        in_specs=[
            pl.BlockSpec((tm, d), lambda i, j: (i, 0)),
            _const_spec((1, d)),
            sh.spec, sc.spec,
            pl.BlockSpec((d, tn), lambda i, j: (0, j)),
        ],
        out_specs=pl.BlockSpec((tm, tn), lambda i, j: (i, j)),
        out_shape=jax.ShapeDtypeStruct((t, n), out_dtype),
        scratch_shapes=[pltpu.VMEM((tm, d), BF16)],
        compiler_params=_params("parallel", "arbitrary"),
        name="modulate_matmul",
    )(x, g.reshape(1, d), sh.array, sc.array, w)


def _out_proj_kernel(*refs, prologue, n_in):
    ins = refs[:n_in]
    w_ref, x_ref, gate_ref, o_ref = refs[n_in:]
    a = prologue(*ins)
    mix = jnp.dot(a, w_ref[...], preferred_element_type=F32)
    o_ref[...] = x_ref[...] + gate_ref[...] * mix


def _out_proj(prologue, ins, in_specs, w, x, gate, *, seq, seq_major, tm):
    t, d = x.shape
    k = w.shape[0]
    gv = _RowVec(gate, tm, seq, seq_major)
    return pl.pallas_call(
        functools.partial(_out_proj_kernel, prologue=prologue, n_in=len(ins)),
        grid=(t // tm,),
        in_specs=list(in_specs) + [
            _const_spec((k, d)),
            pl.BlockSpec((tm, d), lambda i: (i, 0)),
            gv.spec,
        ],
        out_specs=pl.BlockSpec((tm, d), lambda i: (i, 0)),
        out_shape=jax.ShapeDtypeStruct((t, d), F32),
        compiler_params=_params("parallel"),
        name="out_proj_residual",
    )(*ins, w, x, gv.array)


def _rope_constants(seq):
    half = DA_HEAD_DIM // 2
    inv = 1.0 / (ROPE_THETA ** (jnp.arange(0, DA_HEAD_DIM, 2, dtype=F32) / DA_HEAD_DIM))
    ang = jnp.arange(seq, dtype=F32)[:, None] * inv[None, :]
    reps = LANES // half
    cos = jnp.tile(jnp.cos(ang), (1, reps))
    sin = jnp.tile(jnp.sin(ang), (1, reps))
    lane = np.arange(LANES)
    first = (lane % DA_HEAD_DIM) < half
    rot = np.zeros((LANES, LANES), np.float32)
    rot[(lane + half)[first], lane[first]] = -1.0
    rot[(lane - half)[~first], lane[~first]] = 1.0
    group = (lane[:, None] // DA_HEAD_DIM == lane[None, :] // DA_HEAD_DIM).astype(np.float32)
    return cos, sin, jnp.asarray(rot, BF16), jnp.asarray(group, BF16)


def _qk_prep_kernel(in_ref, gain_ref, cos_ref, sin_ref, rot_ref, grp_ref, o_ref):
    is_q = pl.program_id(1) == 0
    out_scale = jnp.where(is_q, DA_HEAD_DIM ** -0.5 * math.log2(math.e), 1.0).astype(F32)
    cos = cos_ref[...]
    sin = sin_ref[...]
    gain = gain_ref[...]
    for h in range(DA_HEADS):
        sl = slice(h * LANES, (h + 1) * LANES)
        y = in_ref[:, sl].astype(F32)
        ss = jnp.dot((y * y).astype(BF16), grp_ref[...], preferred_element_type=F32)
        n = y * lax.rsqrt(ss * (1.0 / DA_HEAD_DIM) + EPS) * gain
        r = jnp.dot(n.astype(BF16), rot_ref[...], preferred_element_type=F32)
        o_ref[:, sl] = ((n * cos + r * sin) * out_scale).astype(BF16)


def _qk_prep(qkv, q_gain, k_gain, cos, sin, rot, grp, *, seq, tm):
    t = qkv.shape[0]
    width = DA_HEADS * LANES
    gains = jnp.stack([jnp.tile(q_gain, 2), jnp.tile(k_gain, 2)]).reshape(2, 1, LANES).astype(F32)
    tiles_per_seq = seq // tm
    return pl.pallas_call(
        _qk_prep_kernel,
        grid=(t // tm, 2),
        in_specs=[
            pl.BlockSpec((tm, width), lambda i, j: (i, j)),
            pl.BlockSpec((None, 1, LANES), lambda i, j: (j, 0, 0)),
            pl.BlockSpec((tm, LANES), lambda i, j: (i % tiles_per_seq, 0)),
            pl.BlockSpec((tm, LANES), lambda i, j: (i % tiles_per_seq, 0)),
            _const_spec((LANES, LANES)),
            _const_spec((LANES, LANES)),
        ],
        out_specs=pl.BlockSpec((None, tm, width), lambda i, j: (j, i, 0)),
        out_shape=jax.ShapeDtypeStruct((2, t, width), BF16),
        compiler_params=_params("parallel", "parallel"),
        name="qk_norm_rope",
    )(qkv, gains, cos, sin, rot, grp)


def _flash_kernel(q_ref, k_ref, v_ref, lam_ref, subln_ref, o_ref,
                  vaug_ref, kmax_ref, bound_ref, mrun_ref, acc_ref, *, tq, tk, lambda_init):
    seq = k_ref.shape[0]
    n_kv = seq // tk
    lane = lax.broadcasted_iota(jnp.int32, (LANES, LANES), 0)
    ones_mat = jnp.ones((LANES, LANES), BF16)
    first_half = jnp.where(lane < DA_HEAD_DIM, 1.0, 0.0).astype(BF16)
    second_half = jnp.where(lane >= DA_HEAD_DIM, 1.0, 0.0).astype(BF16)

    @pl.when(pl.program_id(2) == 0)
    def _():
        col = lax.broadcasted_iota(jnp.int32, (tk, LANES), 1)
        ones_col = jnp.where(col == 0, 1.0, 0.0).astype(BF16)

        def prep(j, carry):
            n1, n2 = carry
            r0 = pl.multiple_of(j * tk, tk)
            vaug_ref[pl.ds(r0, tk), 0:LANES] = v_ref[pl.ds(r0, tk), :]
            vaug_ref[pl.ds(r0, tk), LANES:2 * LANES] = ones_col
            kf = k_ref[pl.ds(r0, tk), :].astype(F32)
            sq = (kf * kf).astype(BF16)
            s1 = jnp.dot(sq, first_half, preferred_element_type=F32)
            s2 = jnp.dot(sq, second_half, preferred_element_type=F32)
            return (jnp.maximum(n1, jnp.max(s1, axis=0, keepdims=True)),
                    jnp.maximum(n2, jnp.max(s2, axis=0, keepdims=True)))

        zero = jnp.zeros((1, LANES), F32)
        n1, n2 = lax.fori_loop(0, n_kv, prep, (zero, zero))
        kmax_ref[0:1, :] = n1
        kmax_ref[1:2, :] = n2

    q = q_ref[...]
    qlane = lax.broadcasted_iota(jnp.int32, q.shape, 1)
    zero = jnp.zeros_like(q)
    qq = jnp.concatenate([jnp.where(qlane < DA_HEAD_DIM, q, zero),
                          jnp.where(qlane >= DA_HEAD_DIM, q, zero)], axis=0)
    qf = qq.astype(F32)
    qn = jnp.dot((qf * qf).astype(BF16), ones_mat, preferred_element_type=F32)
    row = lax.broadcasted_iota(jnp.int32, qn.shape, 0)
    kn = jnp.where(row < tq, kmax_ref[0:1, :], kmax_ref[1:2, :])
    bound = jnp.sqrt(qn * kn) * 1.01
    bound_ref[...] = bound
    acc_ref[...] = jnp.zeros(acc_ref.shape, F32)
    safe = jnp.max(bound) <= 60.0

    def scores(j):
        r0 = pl.multiple_of(j * tk, tk)
        s = lax.dot_general(qq, k_ref[pl.ds(r0, tk), :], (((1,), (1,)), ((), ())),
                            preferred_element_type=F32)
        return s, vaug_ref[pl.ds(r0, tk), :]

    @pl.when(safe)
    def _():
        def body(j, carry):
            s, va = scores(j)
            shift = jnp.tile(bound_ref[...], (1, tk // LANES))
            p = jnp.exp2(s - shift).astype(BF16)
            acc_ref[...] += jnp.dot(p, va, preferred_element_type=F32)
            return carry

        lax.fori_loop(0, n_kv, body, 0, unroll=2)

    @pl.when(jnp.logical_not(safe))
    def _():
        mrun_ref[...] = jnp.full(mrun_ref.shape, -jnp.inf, F32)

        def body(j, carry):
            s, va = scores(j)
            m_old = mrun_ref[...]
            m_new = jnp.maximum(m_old, jnp.max(s, axis=1, keepdims=True))
            alpha = jnp.exp2(m_old - m_new)
            p = jnp.exp2(s - m_new).astype(BF16)
            acc_ref[...] = alpha * acc_ref[...] + jnp.dot(p, va, preferred_element_type=F32)
            mrun_ref[...] = m_new
            return carry

        lax.fori_loop(0, n_kv, body, 0)

    lf = lam_ref[...]
    lam = (jnp.exp(jnp.sum(lf[0:1] * lf[1:2], axis=1, keepdims=True))
           - jnp.exp(jnp.sum(lf[2:3] * lf[3:4], axis=1, keepdims=True)) + lambda_init)
    o1 = acc_ref[0:tq, 0:LANES] / acc_ref[0:tq, LANES:LANES + 1]
    o2 = acc_ref[tq:2 * tq, 0:LANES] / acc_ref[tq:2 * tq, LANES:LANES + 1]
    o = o1 - lam * o2
    ms = jnp.mean(o * o, axis=-1, keepdims=True)
    o = o * lax.rsqrt(ms + EPS) * subln_ref[...] * (1.0 - lambda_init)
    o_ref[...] = o.astype(BF16)


def _flash(qk, qkv, lam_vecs, subln_g, *, batch, seq, lambda_init, tq, tk):
    t = batch * seq
    width = DA_HEADS * LANES
    q_tiles = seq // tq
    v_col0 = 2 * DA_HEADS
    return pl.pallas_call(
        functools.partial(_flash_kernel, tq=tq, tk=tk, lambda_init=lambda_init),
        grid=(batch, DA_HEADS, q_tiles),
        in_specs=[
            pl.BlockSpec((None, tq, LANES), lambda b, h, i: (0, b * q_tiles + i, h)),
            pl.BlockSpec((None, seq, LANES), lambda b, h, i: (1, b, h)),
            pl.BlockSpec((seq, LANES), lambda b, h, i: (b, v_col0 + h)),
            _const_spec((4, DA_HEAD_DIM)),
            _const_spec((1, DA_V_DIM)),
        ],
        out_specs=pl.BlockSpec((tq, LANES), lambda b, h, i: (b * q_tiles + i, h)),
        out_shape=jax.ShapeDtypeStruct((t, width), BF16),
        scratch_shapes=[
            pltpu.VMEM((seq, 2 * LANES), BF16),
            pltpu.VMEM((8, LANES), F32),
            pltpu.VMEM((2 * tq, LANES), F32),
            pltpu.VMEM((2 * tq, 1), F32),
            pltpu.VMEM((2 * tq, 2 * LANES), F32),
        ],
        compiler_params=_params("parallel", "parallel", "arbitrary"),
        name="diff_flash_attention",
    )(qk, qk, qkv, lam_vecs.astype(F32), subln_g.reshape(1, DA_V_DIM).astype(F32))


def _identity_prologue(o_ref):
    return o_ref[...]


def _diff_attention_layer(x, g, shift, scale, gate, w_in, q_norm, k_norm, lam_vecs, subln_g, w_out,
                          lambda_init, *, batch, seq):
    t, d = x.shape
    qkv = _nm_matmul(x, g, shift, scale, w_in.astype(BF16), seq=seq, seq_major=False,
                     tm=min(512, seq), tn=1024, out_dtype=BF16)
    cos, sin, rot, grp = _rope_constants(seq)
    qk = _qk_prep(qkv, q_norm, k_norm, cos, sin, rot, grp, seq=seq, tm=min(512, seq))
    o = _flash(qk, qkv, lam_vecs, subln_g, batch=batch, seq=seq, lambda_init=lambda_init,
               tq=min(512, seq), tk=min(1024, seq))
    tm = min(512, seq)
    return _out_proj(_identity_prologue, [o], [pl.BlockSpec((tm, o.shape[1]), lambda i: (i, 0))],
                     w_out.astype(BF16), x, gate, seq=seq, seq_major=False, tm=tm)


def _hgrn_scan_kernel(q_ref, f_ref, v_ref, lb_ref, o_ref, st_ref, *, reverse, layer):
    c = HG_CHUNK
    n_chunks = q_ref.shape[0] // c
    width = q_ref.shape[1]

    @pl.when(pl.program_id(1) == 0)
    def _():
        st_ref[...] = jnp.zeros(st_ref.shape, F32)

    lbw = lb_ref[...]
    e = jnp.exp(lbw - jnp.max(lbw, axis=0, keepdims=True))
    lb = jnp.sum(e[1:layer + 1], axis=0, keepdims=True) / jnp.sum(e, axis=0, keepdims=True)

    row = lax.broadcasted_iota(jnp.int32, (c, c), 0)
    col = lax.broadcasted_iota(jnp.int32, (c, c), 1)
    tri = (col >= row) if reverse else (col <= row)
    tri = tri.astype(F32)
    tcol = lax.broadcasted_iota(jnp.int32, (c, 1), 0)
    trow = tcol % HG_SUB
    n_sub = c // HG_SUB

    def chunk(ci, carry):
        cidx = (n_chunks - 1 - ci) if reverse else ci
        r0 = pl.multiple_of(cidx * c, c)
        qs = _silu(q_ref[pl.ds(r0, c), :])
        forget = lb + (1.0 - lb) * _sigmoid(f_ref[pl.ds(r0, c), :])
        kk = 1.0 - forget
        logf = jnp.log(forget)
        v = v_ref[pl.ds(r0, c), :]
        vb = v.astype(BF16)
        b = jnp.dot(tri, logf, preferred_element_type=F32, precision=HIGHEST)
        b_edge = b[0:1] if reverse else b[c - 1:c]
        q_in = (qs * jnp.exp(b)).astype(BF16)
        k_out = (kk * jnp.exp(b_edge - b)).astype(BF16)
        dec = jnp.exp(b_edge)

        diag = []
        for delta in range(HG_SUB):
            if delta == 0:
                term = qs * kk
            else:
                shift = (c - delta) if reverse else delta
                valid = (trow + delta <= HG_SUB - 1) if reverse else (trow >= delta)
                k_s = pltpu.roll(kk, shift, 0)
                b_s = pltpu.roll(b, shift, 0)
                term = qs * k_s * jnp.exp(jnp.where(valid, b - b_s, -jnp.inf))
            diag.append(term)

        for h in range(HG_HEADS):
            sl = slice(h * LANES, (h + 1) * LANES)
            a = jnp.zeros((c, c), F32)
            for delta in range(HG_SUB):
                a_d = jnp.sum(diag[delta][:, sl], axis=1, keepdims=True)
                hit = (col == row + delta) if reverse else (col == row - delta)
                a = a + jnp.where(hit, a_d, 0.0)
            blocks = []
            for i in range(n_sub):
                rows = slice(i * HG_SUB, (i + 1) * HG_SUB)
                ref_row = (i + 1) * HG_SUB if reverse else i * HG_SUB - 1
                if ref_row < 0 or ref_row >= c:
                    blocks.append(jnp.zeros((HG_SUB, c), F32))
                    continue
                passed = (tcol >= ref_row) if reverse else (tcol <= ref_row)
                b_ref_row = b[ref_row:ref_row + 1, sl]
                q_t = (qs[rows, sl] * jnp.exp(b[rows, sl] - b_ref_row)).astype(BF16)
                k_t = (kk[:, sl] * jnp.exp(jnp.where(passed, b_ref_row - b[:, sl], -jnp.inf))).astype(BF16)
                blocks.append(lax.dot_general(q_t, k_t, (((1,), (1,)), ((), ())),
                                              preferred_element_type=F32))
            a = a + jnp.concatenate(blocks, axis=0)
            st = st_ref[h]
            o_h = jnp.dot(a.astype(BF16), vb[:, sl], preferred_element_type=F32)
            o_h = o_h + lax.dot_general(q_in[:, sl], st.astype(BF16), (((1,), (1,)), ((), ())),
                                        preferred_element_type=F32)
            o_ref[pl.ds(r0, c), sl] = o_h
            upd = lax.dot_general(vb[:, sl], k_out[:, sl], (((0,), (0,)), ((), ())),
                                  preferred_element_type=F32)
            st_ref[h] = st * dec[:, sl] + upd
        return carry

    lax.fori_loop(0, n_chunks, chunk, 0)


def _hgrn_scan(proj, lower_bounds, *, batch, seq, layer, reverse, tt):
    t = batch * seq
    width = HG_HEADS * LANES
    nt = seq // tt
    depth = lower_bounds.shape[1]
    d_idx = 1 if reverse else 0

    def rows(b, i):
        return b * nt + ((nt - 1 - i) if reverse else i)

    return pl.pallas_call(
        functools.partial(_hgrn_scan_kernel, reverse=reverse, layer=layer),
        grid=(batch, nt),
        in_specs=[
            pl.BlockSpec((tt, width), lambda b, i: (rows(b, i), 0)),
            pl.BlockSpec((tt, width), lambda b, i: (rows(b, i), 1 + d_idx)),
            pl.BlockSpec((tt, width), lambda b, i: (rows(b, i), 3)),
            pl.BlockSpec((None, depth, width), lambda b, i: (d_idx, 0, 0)),
        ],
        out_specs=pl.BlockSpec((tt, width), lambda b, i: (rows(b, i), 0)),
        out_shape=jax.ShapeDtypeStruct((t, width), F32),
        scratch_shapes=[pltpu.VMEM((HG_HEADS, LANES, LANES), F32)],
        compiler_params=_params("parallel", "arbitrary"),
        name="hgrn2_scan_bwd" if reverse else "hgrn2_scan_fwd",
    )(proj, proj, proj, lower_bounds.astype(F32))


def _hgrn_prologue(of_ref, ob_ref, g_ref, ng_ref):
    ng = ng_ref[...]
    outs = []
    for h in range(HG_HEADS):
        sl = slice(h * LANES, (h + 1) * LANES)
        o = of_ref[:, sl] + ob_ref[:, sl]
        ms = jnp.mean(o * o, axis=-1, keepdims=True)
        outs.append((o * lax.rsqrt(ms + EPS) * ng * _silu(g_ref[:, sl])).astype(BF16))
    return jnp.concatenate(outs, axis=1)


def _hgrn_layer(x, g, shift, scale, gate, w_in, lower_bounds, norm_g, w_out, layer, *, batch, seq):
    width = HG_HEADS * LANES
    proj = _nm_matmul(x, g, shift, scale, w_in.astype(BF16), seq=seq, seq_major=False,
                      tm=min(512, seq), tn=1024, out_dtype=F32)
    tt = min(512, seq)
    o_f = _hgrn_scan(proj, lower_bounds, batch=batch, seq=seq, layer=layer, reverse=False, tt=tt)
    o_b = _hgrn_scan(proj, lower_bounds, batch=batch, seq=seq, layer=layer, reverse=True, tt=tt)
    tm = min(512, seq)
    ins = [o_f, o_b, proj, norm_g.reshape(1, LANES).astype(F32)]
    specs = [
        pl.BlockSpec((tm, width), lambda i: (i, 0)),
        pl.BlockSpec((tm, width), lambda i: (i, 0)),
        pl.BlockSpec((tm, width), lambda i: (i, 4)),
        _const_spec((1, LANES)),
    ]
    return _out_proj(_hgrn_prologue, ins, specs, w_out.astype(BF16), x, gate,
                     seq=seq, seq_major=False, tm=tm)


def _lru_scan_kernel(prev_ref, main_ref, next_ref, cw_ref, cb_ref, wg_ref, ba_ref, bx_ref, lam_ref,
                     o_ref, a_ref, u_ref, h_ref, *, batch, nt):
    d = pl.program_id(0)
    i = pl.program_id(1)
    ti = i + d * (nt - 1 - 2 * i)
    rows, width = main_ref.shape
    tt = rows // batch

    @pl.when(i == 0)
    def _():
        h_ref[...] = jnp.zeros(h_ref.shape, F32)

    prev = jnp.where(ti == 0, 0.0, prev_ref[...])
    nxt = jnp.where(ti == nt - 1, 0.0, next_ref[...])
    ext = jnp.concatenate([prev, main_ref[...], nxt], axis=0)
    cw = cw_ref[...]
    xc = cb_ref[...] + cw[0:1] * ext[0:rows]
    for j in range(1, CONV_WIDTH):
        xc = xc + cw[j:j + 1] * ext[j * batch:j * batch + rows]

    neg_softplus = -LRU_C * (jnp.maximum(-lam_ref[...], 0.0) + jnp.log1p(jnp.exp(-jnp.abs(lam_ref[...]))))
    for n in range(width // LRU_BLOCK_W):
        sl = slice(n * LRU_BLOCK_W, (n + 1) * LRU_BLOCK_W)
        xb = xc[:, sl]
        gates = jnp.dot(xb.astype(BF16), wg_ref[n], preferred_element_type=F32)
        r = _sigmoid(gates[:, :LRU_BLOCK_W] + ba_ref[:, sl])
        ig = _sigmoid(gates[:, LRU_BLOCK_W:] + bx_ref[:, sl])
        a = jnp.exp(r * neg_softplus[:, sl])
        a_ref[:, sl] = a
        u_ref[:, sl] = jnp.sqrt(1.0 - a * a) * (ig * xb)

    def step(t, h):
        tl = t + d * (tt - 1 - 2 * t)
        r0 = pl.multiple_of(tl * batch, batch)
        h = a_ref[pl.ds(r0, batch), :] * h + u_ref[pl.ds(r0, batch), :]
        o_ref[pl.ds(r0, batch), :] = h
        return h

    h_ref[...] = lax.fori_loop(0, tt, step, h_ref[...])


def _lru_scan(proj, conv_w, conv_b, w_a, b_a, w_x, b_x, lam, *, batch, seq, tt):
    t = batch * seq
    width = conv_w.shape[1]
    n_blocks = width // LRU_BLOCK_W
    nt = seq // tt
    rows = tt * batch
    assert tt % 2 == 0

    def tile(d, i):
        return i + d * (nt - 1 - 2 * i)

    wg = jnp.concatenate([w_a, w_x], axis=-1).astype(BF16)
    vec = lambda a: a.reshape(2, 1, width).astype(F32)
    return pl.pallas_call(
        functools.partial(_lru_scan_kernel, batch=batch, nt=nt),
        grid=(2, nt),
        in_specs=[
            pl.BlockSpec((batch, width), lambda d, i: (jnp.maximum(tile(d, i) * tt - 1, 0), 1)),
            pl.BlockSpec((rows, width), lambda d, i: (tile(d, i), 1)),
            pl.BlockSpec((2 * batch, width),
                         lambda d, i: (jnp.minimum((tile(d, i) + 1) * (tt // 2), seq // 2 - 1), 1)),
            _const_spec((CONV_WIDTH, width)),
            _const_spec((1, width)),
            pl.BlockSpec((None, n_blocks, LRU_BLOCK_W, 2 * LRU_BLOCK_W), lambda d, i: (d, 0, 0, 0)),
            pl.BlockSpec((None, 1, width), lambda d, i: (d, 0, 0)),
            pl.BlockSpec((None, 1, width), lambda d, i: (d, 0, 0)),
            pl.BlockSpec((None, 1, width), lambda d, i: (d, 0, 0)),
        ],
        out_specs=pl.BlockSpec((None, rows, width), lambda d, i: (d, tile(d, i), 0)),
        out_shape=jax.ShapeDtypeStruct((2, t, width), F32),
        scratch_shapes=[
            pltpu.VMEM((rows, width), F32),
            pltpu.VMEM((rows, width), F32),
            pltpu.VMEM((batch, width), F32),
        ],
        compiler_params=_params("arbitrary", "arbitrary"),
        name="rglru_scan",
    )(proj, proj, proj, conv_w.astype(F32), conv_b.reshape(1, width).astype(F32), wg,
      vec(b_a), vec(b_x), vec(lam))


def _gelu_tanh(x):
    return 0.5 * x * (1.0 + jnp.tanh(math.sqrt(2.0 / math.pi) * (x + 0.044715 * (x * x * x))))


def _lru_prologue(hf_ref, hb_ref, y_ref):
    return ((hf_ref[...] + hb_ref[...]) * _gelu_tanh(y_ref[...])).astype(BF16)


def _rglru_layer(x, g, shift, scale, gate, w_in, conv_w, conv_b, w_a, b_a, w_x, b_x, lam, w_out,
                 *, batch, seq):
    width = conv_w.shape[1]
    tm = 256
    proj = _nm_matmul(x, g, shift, scale, w_in.astype(BF16), seq=seq, seq_major=True,
                      tm=tm, tn=width, out_dtype=F32)
    h2 = _lru_scan(proj, conv_w, conv_b, w_a, b_a, w_x, b_x, lam, batch=batch, seq=seq, tt=min(64, seq))
    specs = [
        pl.BlockSpec((None, tm, width), lambda i: (0, i, 0)),
        pl.BlockSpec((None, tm, width), lambda i: (1, i, 0)),
        pl.BlockSpec((tm, width), lambda i: (i, 0)),
    ]
    return _out_proj(_lru_prologue, [h2, h2, proj], specs, w_out.astype(BF16), x, gate,
                     seq=seq, seq_major=True, tm=tm)


def _ffn_kernel(x_ref, g_ref, sh_ref, sc_ref, wg_ref, wu_ref, wd_ref, gate_ref, o_ref, h_ref, acc_ref):
    j = pl.program_id(1)

    @pl.when(j == 0)
    def _():
        h_ref[...] = _modulated(x_ref[...], g_ref[...], sh_ref[...], sc_ref[...]).astype(BF16)
        acc_ref[...] = jnp.zeros(acc_ref.shape, F32)

    h = h_ref[...]
    gt = jnp.dot(h, wg_ref[...], preferred_element_type=F32)
    up = jnp.dot(h, wu_ref[...], preferred_element_type=F32)
    act = (_silu(gt) * up).astype(BF16)
    acc_ref[...] += jnp.dot(act, wd_ref[...], preferred_element_type=F32)

    @pl.when(j == pl.num_programs(1) - 1)
    def _():
        o_ref[...] = x_ref[...] + gate_ref[...] * acc_ref[...]


def _dense_ffn(x, g, shift, scale, gate, w_gu, w_down, *, batch, seq, x_seq_major, out_seq_major, tm, tf):
    t = batch * seq
    d = w_gu.shape[0]
    f = w_down.shape[0]
    assert f % tf == 0 and seq % tm == 0
    nf = f // tf
    tiles_per_seq = seq // tm
    sh = _RowVec(shift, tm, seq, False)
    sc = _RowVec(scale, tm, seq, False)
    gv = _RowVec(gate, tm, seq, False)

    def view(a, seq_major):
        if seq_major:
            return a.reshape(seq, batch * d), pl.BlockSpec(
                (tm, d), lambda i, j: (i % tiles_per_seq, i // tiles_per_seq))
        return a, pl.BlockSpec((tm, d), lambda i, j: (i, 0))

    x_arr, x_spec = view(x, x_seq_major)
    out_shape = (seq, batch * d) if out_seq_major else (t, d)
    _, o_spec = view(x, out_seq_major)
    out = pl.pallas_call(
        _ffn_kernel,
        grid=(t // tm, nf),
        in_specs=[
            x_spec,
            _const_spec((1, d)),
            sh.spec, sc.spec,
            pl.BlockSpec((d, tf), lambda i, j: (0, j)),
            pl.BlockSpec((d, tf), lambda i, j: (0, nf + j)),
            pl.BlockSpec((tf, d), lambda i, j: (j, 0)),
            gv.spec,
        ],
        out_specs=o_spec,
        out_shape=jax.ShapeDtypeStruct(out_shape, F32),
        scratch_shapes=[pltpu.VMEM((tm, d), BF16), pltpu.VMEM((tm, d), F32)],
        compiler_params=_params("parallel", "arbitrary"),
        name="dense_swiglu",
    )(x_arr, g.reshape(1, d), sh.array, sc.array, w_gu, w_gu, w_down, gv.array)
    return out.reshape(t, d)


def _route_kernel(x_ref, g_ref, sh_ref, sc_ref, r_ref, hf_ref, meta_ref, cnt_ref, carry_ref):
    tm = x_ref.shape[0]

    @pl.when(pl.program_id(0) == 0)
    def _():
        carry_ref[...] = jnp.zeros(carry_ref.shape, F32)

    hf = _modulated(x_ref[...], g_ref[...], sh_ref[...], sc_ref[...])
    hf_ref[...] = hf
    logits = jnp.dot(hf, r_ref[...], preferred_element_type=F32, precision=HIGHEST)
    lane = lax.broadcasted_iota(jnp.int32, logits.shape, 1).astype(F32)
    logits = jnp.where(lane < N_EXPERTS, logits, -jnp.inf)
    m1 = jnp.max(logits, axis=1, keepdims=True)
    e1 = jnp.min(jnp.where(logits == m1, lane, float(LANES)), axis=1, keepdims=True)
    rest = jnp.where(lane == e1, -jnp.inf, logits)
    m2 = jnp.max(rest, axis=1, keepdims=True)
    e2 = jnp.min(jnp.where(rest == m2, lane, float(LANES)), axis=1, keepdims=True)
    gate2 = 1.0 / (1.0 + jnp.exp(m1 - m2))
    gate1 = 1.0 - gate2
    hot1 = (lane == e1).astype(F32)
    hot2 = (lane == e2).astype(F32)
    sel = hot1 + hot2
    row = lax.broadcasted_iota(jnp.int32, (tm, tm), 0)
    col = lax.broadcasted_iota(jnp.int32, (tm, tm), 1)
    before = (col < row).astype(BF16)
    seen = carry_ref[...] + jnp.dot(before, sel.astype(BF16), preferred_element_type=F32)
    rank1 = jnp.sum(hot1 * seen, axis=1, keepdims=True)
    rank2 = jnp.sum(hot2 * seen, axis=1, keepdims=True)
    carry_ref[...] += jnp.sum(sel, axis=0, keepdims=True)
    cnt_ref[...] = carry_ref[...]
    meta = jnp.zeros(logits.shape, F32)
    for k, val in enumerate((e1, e2, gate1, gate2, rank1, rank2)):
        meta = jnp.where(lane == float(k), val, meta)
    meta_ref[...] = meta


def _route(x, g, shift, scale, router, *, seq, tm):
    t, d = x.shape
    sh = _RowVec(shift, tm, seq, False)
    sc = _RowVec(scale, tm, seq, False)
    router_pad = jnp.zeros((d, LANES), F32).at[:, :N_EXPERTS].set(router.astype(F32))
    return pl.pallas_call(
        _route_kernel,
        grid=(t // tm,),
        in_specs=[
            pl.BlockSpec((tm, d), lambda i: (i, 0)),
            _const_spec((1, d)),
            sh.spec, sc.spec,
            _const_spec((d, LANES)),
        ],
        out_specs=[
            pl.BlockSpec((tm, d), lambda i: (i, 0)),
            pl.BlockSpec((tm, LANES), lambda i: (i, 0)),
            _const_spec((1, LANES)),
        ],
        out_shape=[
            jax.ShapeDtypeStruct((t, d), F32),
            jax.ShapeDtypeStruct((t, LANES), F32),
            jax.ShapeDtypeStruct((1, LANES), F32),
        ],
        scratch_shapes=[pltpu.VMEM((1, LANES), F32)],
        compiler_params=_params("arbitrary"),
        name="moe_route",
    )(x, g.reshape(1, d), sh.array, sc.array, router_pad)


def _row_copy(src_hbm, src_row, dst_vmem, dst_row, sem):
    return pltpu.make_async_copy(src_hbm.at[pl.ds(src_row, 1)], dst_vmem.at[pl.ds(dst_row, 1)], sem)


def _start_row_gather(idx_smem, slot, src_hbm, dst_vmem, sem):
    n = dst_vmem.shape[1]

    def start(r, c):
        _row_copy(src_hbm, idx_smem[slot, r], dst_vmem.at[slot], r, sem.at[slot]).start()
        return c

    lax.fori_loop(0, n, start, 0, unroll=8)


def _wait_row_gather(slot, src_hbm, dst_vmem, sem):
    n = dst_vmem.shape[1]
    pltpu.make_async_copy(src_hbm.at[pl.ds(0, n)], dst_vmem.at[slot], sem.at[slot]).wait()


def _expert_kernel(te_ref, nu_ref, src_ref, hf_ref, wg_ref, wu_ref, wd_ref, o_ref,
                   idx_ref, xg_ref, h_ref, acc_ref, sem_idx, sem_rows):
    i = pl.program_id(0)
    j = pl.program_id(1)
    n_used = nu_ref[0]
    used = i < n_used
    slot = i % 2

    def idx_copy(tile, into):
        return pltpu.make_async_copy(src_ref.at[tile], idx_ref.at[into], sem_idx)

    @pl.when(jnp.logical_and(used, j == 0))
    def _():
        @pl.when(i == 0)
        def _():
            idx_copy(0, 0).start()
            idx_copy(0, 0).wait()
            _start_row_gather(idx_ref, 0, hf_ref, xg_ref, sem_rows)

        has_next = i + 1 < n_used

        @pl.when(has_next)
        def _():
            idx_copy(i + 1, 1 - slot).start()

        _wait_row_gather(slot, hf_ref, xg_ref, sem_rows)
        h_ref[...] = xg_ref[slot].astype(BF16)
        acc_ref[...] = jnp.zeros(acc_ref.shape, F32)

        @pl.when(has_next)
        def _():
            idx_copy(i + 1, 1 - slot).wait()
            _start_row_gather(idx_ref, 1 - slot, hf_ref, xg_ref, sem_rows)

    @pl.when(used)
    def _():
        h = h_ref[...]
        gt = jnp.dot(h, wg_ref[...], preferred_element_type=F32)
        up = jnp.dot(h, wu_ref[...], preferred_element_type=F32)
        act = (_silu(gt) * up).astype(BF16)
        acc_ref[...] += jnp.dot(act, wd_ref[...], preferred_element_type=F32)

    @pl.when(j == pl.num_programs(1) - 1)
    def _():
        o_ref[...] = jnp.where(used, acc_ref[...], 0.0)


def _experts(hf, src_tiles, tile_expert, n_used, w_gu, w_down, *, tm, tf):
    t, d = hf.shape
    n_tiles = src_tiles.shape[0]
    f = w_down.shape[1]
    nf = f // tf

    def jj(i, j, nu):
        return jnp.where(i < nu[0], j, nf - 1)

    grid_spec = pltpu.PrefetchScalarGridSpec(
        num_scalar_prefetch=2,
        grid=(n_tiles, nf),
        in_specs=[
            pl.BlockSpec(memory_space=pl.ANY),
            pl.BlockSpec(memory_space=pl.ANY),
            pl.BlockSpec((None, d, tf), lambda i, j, te, nu: (te[i], 0, jj(i, j, nu))),
            pl.BlockSpec((None, d, tf), lambda i, j, te, nu: (te[i], 0, nf + jj(i, j, nu))),
            pl.BlockSpec((None, tf, d), lambda i, j, te, nu: (te[i], jj(i, j, nu), 0)),
        ],
        out_specs=pl.BlockSpec((tm, d), lambda i, j, te, nu: (i, 0)),
        scratch_shapes=[
            pltpu.SMEM((2, tm), jnp.int32),
            pltpu.VMEM((2, tm, d), F32),
            pltpu.VMEM((tm, d), BF16),
            pltpu.VMEM((tm, d), F32),
            pltpu.SemaphoreType.DMA,
            pltpu.SemaphoreType.DMA((2,)),
        ],
    )
    return pl.pallas_call(
        _expert_kernel,
        grid_spec=grid_spec,
        out_shape=jax.ShapeDtypeStruct((n_tiles * tm, d), F32),
        compiler_params=_params("arbitrary", "arbitrary"),
        name="moe_experts",
    )(tile_expert, n_used, src_tiles, hf, w_gu, w_gu, w_down)


def _combine_kernel(d1_ref, d2_ref, ys_ref, x_ref, meta_ref, gate_ref, o_ref,
                    i1_ref, i2_ref, a_ref, b_ref, sem_idx, sem_a, sem_b):
    i = pl.program_id(0)
    slot = i % 2
    has_next = i + 1 < pl.num_programs(0)

    def idx_copies(tile, into):
        return (pltpu.make_async_copy(d1_ref.at[tile], i1_ref.at[into], sem_idx.at[0]),
                pltpu.make_async_copy(d2_ref.at[tile], i2_ref.at[into], sem_idx.at[1]))

    def start_rows(into):
        _start_row_gather(i1_ref, into, ys_ref, a_ref, sem_a)
        _start_row_gather(i2_ref, into, ys_ref, b_ref, sem_b)

    @pl.when(i == 0)
    def _():
        for cp in idx_copies(0, 0):
            cp.start()
        for cp in idx_copies(0, 0):
            cp.wait()
        start_rows(0)

    @pl.when(has_next)
    def _():
        for cp in idx_copies(i + 1, 1 - slot):
            cp.start()

    _wait_row_gather(slot, ys_ref, a_ref, sem_a)
    _wait_row_gather(slot, ys_ref, b_ref, sem_b)

    @pl.when(has_next)
    def _():
        for cp in idx_copies(i + 1, 1 - slot):
            cp.wait()
        start_rows(1 - slot)

    meta = meta_ref[...]
    y = meta[:, 2:3] * a_ref[slot] + meta[:, 3:4] * b_ref[slot]
    o_ref[...] = x_ref[...] + gate_ref[...] * y


def _combine(ys, dest1, dest2, x, meta, gate, *, batch, seq, out_seq_major, tm):
    t, d = x.shape
    tiles_per_seq = seq // tm
    gv = _RowVec(gate, tm, seq, False)
    if out_seq_major:
        out_shape = (seq, batch * d)
        o_spec = pl.BlockSpec((tm, d), lambda i: (i % tiles_per_seq, i // tiles_per_seq))
    else:
        out_shape = (t, d)
        o_spec = pl.BlockSpec((tm, d), lambda i: (i, 0))
    out = pl.pallas_call(
        _combine_kernel,
        grid=(t // tm,),
        in_specs=[
            pl.BlockSpec(memory_space=pl.ANY),
            pl.BlockSpec(memory_space=pl.ANY),
            pl.BlockSpec(memory_space=pl.ANY),
            pl.BlockSpec((tm, d), lambda i: (i, 0)),
            pl.BlockSpec((tm, LANES), lambda i: (i, 0)),
            gv.spec,
        ],
        out_specs=o_spec,
        out_shape=jax.ShapeDtypeStruct(out_shape, F32),
        scratch_shapes=[
            pltpu.SMEM((2, tm), jnp.int32),
            pltpu.SMEM((2, tm), jnp.int32),
            pltpu.VMEM((2, tm, d), F32),
            pltpu.VMEM((2, tm, d), F32),
            pltpu.SemaphoreType.DMA((2,)),
            pltpu.SemaphoreType.DMA((2,)),
            pltpu.SemaphoreType.DMA((2,)),
        ],
        compiler_params=_params("arbitrary"),
        name="moe_combine",
    )(dest1.reshape(t // tm, tm), dest2.reshape(t // tm, tm), ys, x, meta, gv.array)
    return out.reshape(t, d)


def _moe_ffn(x, g, shift, scale, gate, router, w_gu, w_down, *, batch, seq, out_seq_major):
    t, d = x.shape
    tm_route = min(512, seq)
    tm_e = min(1024, seq)
    hf, meta, counts = _route(x, g, shift, scale, router, seq=seq, tm=tm_route)

    e1 = meta[:, 0].astype(jnp.int32)
    e2 = meta[:, 1].astype(jnp.int32)
    r1 = meta[:, 4].astype(jnp.int32)
    r2 = meta[:, 5].astype(jnp.int32)
    cnt = counts[0, :N_EXPERTS].astype(jnp.int32)
    padded = (cnt + tm_e - 1) // tm_e * tm_e
    ends = jnp.cumsum(padded)
    starts = ends - padded
    dest1 = starts[e1] + r1
    dest2 = starts[e2] + r2
    n_tiles = TOP_K * t // tm_e + N_EXPERTS
    tok = jnp.arange(t, dtype=jnp.int32)
    src = jnp.zeros((n_tiles * tm_e,), jnp.int32).at[dest1].set(tok).at[dest2].set(tok)
    n_used = (ends[-1] // tm_e).astype(jnp.int32)
    tile_start = jnp.arange(n_tiles, dtype=jnp.int32) * tm_e
    tile_expert = jnp.minimum(jnp.sum(tile_start[:, None] >= ends[None, :], axis=1), N_EXPERTS - 1)
    tile_expert = jnp.where(jnp.arange(n_tiles) < n_used, tile_expert,
                            tile_expert[jnp.maximum(n_used - 1, 0)]).astype(jnp.int32)

    ys = _experts(hf, src.reshape(n_tiles, tm_e), tile_expert, n_used.reshape(1),
                  w_gu.astype(BF16), w_down.astype(BF16), tm=tm_e, tf=512)
    return _combine(ys, dest1, dest2, x, meta, gate, batch=batch, seq=seq,
                    out_seq_major=out_seq_major, tm=min(256, seq))


def kernel(x, c, ada_w, ada_b, norm1_g, norm2_g, at_w_in, at_q_norm, at_k_norm, at_lam, at_subln, at_w_out, hg_w_in, hg_lower_bounds, hg_norm_g, hg_w_out, lru_w_in, lru_conv_w, lru_conv_b, lru_w_a, lru_b_a, lru_w_x, lru_b_x, lru_lambda, lru_w_out, ff_w_gu, ff_w_down, moe_router, moe_w_gu, moe_w_down):
    batch, seq, d = x.shape
    depth = ada_w.shape[0]
    t = batch * seq
    mods = _mods(c, ada_w, ada_b)
    xr = x.reshape(t, d)
    seq_major = False
    for i in range(depth):
        sh1, sc1, g1, sh2, sc2, g2 = [mods[i, :, k * d:(k + 1) * d] for k in range(6)]
        kind, j = i % N_MIXERS, i // N_MIXERS
        next_is_lru = (i + 1 < depth) and ((i + 1) % N_MIXERS == 2)
        if kind == 0:
            assert not seq_major
            lambda_init = 0.8 - 0.6 * math.exp(-0.3 * i)
            xr = _diff_attention_layer(xr, norm1_g[i], sh1, sc1, g1, at_w_in[j], at_q_norm[j], at_k_norm[j],
                                       at_lam[j], at_subln[j], at_w_out[j], lambda_init, batch=batch, seq=seq)
        elif kind == 1:
            assert not seq_major
            xr = _hgrn_layer(xr, norm1_g[i], sh1, sc1, g1, hg_w_in[j], hg_lower_bounds, hg_norm_g[j],
                             hg_w_out[j], i, batch=batch, seq=seq)
        else:
            if not seq_major:
                xr = xr.reshape(batch, seq, d).transpose(1, 0, 2).reshape(t, d)
                seq_major = True
            xr = _rglru_layer(xr, norm1_g[i], sh1, sc1, g1, lru_w_in[j], lru_conv_w[j], lru_conv_b[j],
                              lru_w_a[j], lru_b_a[j], lru_w_x[j], lru_b_x[j], lru_lambda[j], lru_w_out[j],
                              batch=batch, seq=seq)
        m = i // 2
        if i % 2 == 0:
            xr = _dense_ffn(xr, norm2_g[i], sh2, sc2, g2, ff_w_gu[m].astype(BF16), ff_w_down[m].astype(BF16),
                            batch=batch, seq=seq, x_seq_major=seq_major, out_seq_major=next_is_lru,
                            tm=min(1024, seq), tf=ff_w_down.shape[1] // 2)
            seq_major = next_is_lru
        else:
            if seq_major:
                xr = xr.reshape(seq, batch, d).transpose(1, 0, 2).reshape(t, d)
                seq_major = False
            xr = _moe_ffn(xr, norm2_g[i], sh2, sc2, g2, moe_router[m], moe_w_gu[m], moe_w_down[m],
                          batch=batch, seq=seq, out_seq_major=next_is_lru)
            seq_major = next_is_lru
    if seq_major:
        xr = xr.reshape(seq, batch, d).transpose(1, 0, 2).reshape(t, d)
    return xr.reshape(batch, seq, d)
```

```python
import functools
import math

import numpy as np
import jax
import jax.numpy as jnp
from jax import lax
from jax.experimental import pallas as pl
from jax.experimental.pallas import tpu as pltpu

F32 = jnp.float32
BF16 = jnp.bfloat16
HIGHEST = lax.Precision.HIGHEST

EPS = 1e-6
LANES = 128
VMEM_LIMIT_BYTES = 56 * 2**20

N_MIXERS = 3
DA_HEADS = 8
DA_HEAD_DIM = 64
DA_V_DIM = 2 * DA_HEAD_DIM
ROPE_THETA = 10000.0
HG_HEADS = 8
HG_CHUNK = 64
HG_SUB = 16
LRU_BLOCK_W = 128
CONV_WIDTH = 4
LRU_C = 8.0
N_EXPERTS = 8
TOP_K = 2


def _params(*sem):
    return pltpu.CompilerParams(dimension_semantics=sem, vmem_limit_bytes=VMEM_LIMIT_BYTES)


def _sigmoid(x):
    return 1.0 / (1.0 + jnp.exp(-x))


def _silu(x):
    return x * _sigmoid(x)


def _modulated(x, g, shift, scale):
    ms = jnp.mean(x * x, axis=-1, keepdims=True)
    return x * lax.rsqrt(ms + EPS) * g * (1.0 + scale) + shift


def _mods_kernel(c_ref, w_ref, b_ref, o_ref):
    c = c_ref[...]
    o_ref[...] = jnp.dot(_silu(c), w_ref[...], preferred_element_type=F32, precision=HIGHEST) + b_ref[...]


def _mods(c, ada_w, ada_b):
    depth, d, n = ada_w.shape
    b = c.shape[0]
    tn = 2048
    return pl.pallas_call(
        _mods_kernel,
        grid=(depth, n // tn),
        in_specs=[
            pl.BlockSpec((b, d), lambda l, j: (0, 0)),
            pl.BlockSpec((None, d, tn), lambda l, j: (l, 0, j)),
            pl.BlockSpec((None, 1, tn), lambda l, j: (l, 0, j)),
        ],
        out_specs=pl.BlockSpec((None, b, tn), lambda l, j: (l, 0, j)),
        out_shape=jax.ShapeDtypeStruct((depth, b, n), F32),
        compiler_params=_params("parallel", "parallel"),
        name="adaln_mods",
    )(c, ada_w, ada_b.reshape(depth, 1, n))


class _RowVec:
    def __init__(self, vec, tm, seq):
        b, d = vec.shape
        assert seq % tm == 0
        tiles_per_batch = seq // tm
        self.array = vec.reshape(b, 1, d)
        self.spec = pl.BlockSpec((None, 1, d), lambda i, *_: (i // tiles_per_batch, 0, 0))


def _const_spec(shape):
    nd = len(shape)
    return pl.BlockSpec(shape, lambda *_: (0,) * nd)


def _nm_matmul_kernel(x_ref, g_ref, sh_ref, sc_ref, w_ref, o_ref, h_ref):
    @pl.when(pl.program_id(1) == 0)
    def _():
        h_ref[...] = _modulated(x_ref[...], g_ref[...], sh_ref[...], sc_ref[...]).astype(BF16)

    o_ref[...] = jnp.dot(h_ref[...], w_ref[...], preferred_element_type=F32).astype(o_ref.dtype)


def _nm_matmul(x, g, shift, scale, w, *, seq, tm, tn, out_dtype):
    t, d = x.shape
    n = w.shape[1]
    assert t % tm == 0 and n % tn == 0
    sh = _RowVec(shift, tm, seq)
    sc = _RowVec(scale, tm, seq)
    return pl.pallas_call(
        _nm_matmul_kernel,
        grid=(t // tm, n // tn),
        in_specs=[
            pl.BlockSpec((tm, d), lambda i, j: (i, 0)),
            _const_spec((1, d)),
            sh.spec, sc.spec,
            pl.BlockSpec((d, tn), lambda i, j: (0, j)),
        ],
        out_specs=pl.BlockSpec((tm, tn), lambda i, j: (i, j)),
        out_shape=jax.ShapeDtypeStruct((t, n), out_dtype),
        scratch_shapes=[pltpu.VMEM((tm, d), BF16)],
        compiler_params=_params("parallel", "arbitrary"),
        name="modulate_matmul",
    )(x, g.reshape(1, d), sh.array, sc.array, w)


def _out_proj_kernel(*refs, prologue, n_in):
    ins = refs[:n_in]
    w_ref, x_ref, gate_ref, o_ref = refs[n_in:]
    a = prologue(*ins)
    mix = jnp.dot(a, w_ref[...], preferred_element_type=F32)
    o_ref[...] = x_ref[...] + gate_ref[...] * mix


def _out_proj(prologue, ins, in_specs, w, x, gate, *, seq, tm):
    t, d = x.shape
    k = w.shape[0]
    gv = _RowVec(gate, tm, seq)
    return pl.pallas_call(
        functools.partial(_out_proj_kernel, prologue=prologue, n_in=len(ins)),
        grid=(t // tm,),
        in_specs=list(in_specs) + [
            _const_spec((k, d)),
            pl.BlockSpec((tm, d), lambda i: (i, 0)),
            gv.spec,
        ],
        out_specs=pl.BlockSpec((tm, d), lambda i: (i, 0)),
        out_shape=jax.ShapeDtypeStruct((t, d), F32),
        compiler_params=_params("parallel"),
        name="out_proj_residual",
    )(*ins, w, x, gv.array)


def _rope_constants(seq):
    half = DA_HEAD_DIM // 2
    inv = 1.0 / (ROPE_THETA ** (jnp.arange(0, DA_HEAD_DIM, 2, dtype=F32) / DA_HEAD_DIM))
    ang = jnp.arange(seq, dtype=F32)[:, None] * inv[None, :]
    reps = LANES // half
    cos = jnp.tile(jnp.cos(ang), (1, reps))
    sin = jnp.tile(jnp.sin(ang), (1, reps))
    lane = np.arange(LANES)
    first = (lane % DA_HEAD_DIM) < half
    rot = np.zeros((LANES, LANES), np.float32)
    rot[(lane + half)[first], lane[first]] = -1.0
    rot[(lane - half)[~first], lane[~first]] = 1.0
    group = (lane[:, None] // DA_HEAD_DIM == lane[None, :] // DA_HEAD_DIM).astype(np.float32)
    return cos, sin, jnp.asarray(rot, BF16), jnp.asarray(group, BF16)


def _attn_in_kernel(x_ref, g_ref, sh_ref, sc_ref, w_ref, gain_ref, cos_ref, sin_ref, rot_ref, grp_ref,
                    o_ref, h_ref):
    j = pl.program_id(1)

    @pl.when(j == 0)
    def _():
        h_ref[...] = _modulated(x_ref[...], g_ref[...], sh_ref[...], sc_ref[...]).astype(BF16)

    y_all = jnp.dot(h_ref[...], w_ref[...], preferred_element_type=F32)

    @pl.when(j == 2)
    def _():
        o_ref[...] = y_all.astype(BF16)

    @pl.when(j < 2)
    def _():
        out_scale = jnp.where(j == 0, DA_HEAD_DIM ** -0.5 * math.log2(math.e), 1.0).astype(F32)
        cos = cos_ref[...]
        sin = sin_ref[...]
        gain = gain_ref[...]
        for h in range(DA_HEADS):
            sl = slice(h * LANES, (h + 1) * LANES)
            y = y_all[:, sl]
            ss = jnp.dot((y * y).astype(BF16), grp_ref[...], preferred_element_type=F32)
            n = y * lax.rsqrt(ss * (1.0 / DA_HEAD_DIM) + EPS) * gain
            r = jnp.dot(n.astype(BF16), rot_ref[...], preferred_element_type=F32)
            o_ref[:, sl] = ((n * cos + r * sin) * out_scale).astype(BF16)


def _attn_in_proj(x, g, shift, scale, w, q_gain, k_gain, cos, sin, rot, grp, *, seq, tm):
    t, d = x.shape
    width = DA_HEADS * LANES
    assert w.shape[1] == 3 * width
    gains = jnp.stack([jnp.tile(q_gain, 2), jnp.tile(k_gain, 2)]).reshape(2, 1, LANES).astype(F32)
    tiles_per_seq = seq // tm
    sh = _RowVec(shift, tm, seq)
    sc = _RowVec(scale, tm, seq)
    return pl.pallas_call(
        _attn_in_kernel,
        grid=(t // tm, 3),
        in_specs=[
            pl.BlockSpec((tm, d), lambda i, j: (i, 0)),
            _const_spec((1, d)),
            sh.spec, sc.spec,
            pl.BlockSpec((d, width), lambda i, j: (0, j)),
            pl.BlockSpec((None, 1, LANES), lambda i, j: (jnp.minimum(j, 1), 0, 0)),
            pl.BlockSpec((tm, LANES), lambda i, j: (i % tiles_per_seq, 0)),
            pl.BlockSpec((tm, LANES), lambda i, j: (i % tiles_per_seq, 0)),
            _const_spec((LANES, LANES)),
            _const_spec((LANES, LANES)),
        ],
        out_specs=pl.BlockSpec((None, tm, width), lambda i, j: (j, i, 0)),
        out_shape=jax.ShapeDtypeStruct((3, t, width), BF16),
        scratch_shapes=[pltpu.VMEM((tm, d), BF16)],
        compiler_params=_params("parallel", "arbitrary"),
        name="attn_in_proj",
    )(x, g.reshape(1, d), sh.array, sc.array, w, gains, cos, sin, rot, grp)


FLASH_VT_ROWS = 144


def _flash_kernel(q_ref, k_ref, v_ref, lam_ref, subln_ref, o_ref,
                  vt_ref, kmax_ref, bound_ref, mrun_ref, acc_ref, *, tq, tk, lambda_init):
    seq = k_ref.shape[0]
    n_kv = seq // tk
    lane = lax.broadcasted_iota(jnp.int32, (LANES, LANES), 0)
    first_half = jnp.where(lane < DA_HEAD_DIM, 1.0, 0.0).astype(BF16)
    second_half = jnp.where(lane >= DA_HEAD_DIM, 1.0, 0.0).astype(BF16)

    @pl.when(pl.program_id(2) == 0)
    def _():
        extra = lax.broadcasted_iota(jnp.int32, (FLASH_VT_ROWS - LANES, tk), 0)
        ones_row = jnp.where(extra == 0, 1.0, 0.0).astype(BF16)

        def prep(j, carry):
            n1, n2 = carry
            r0 = pl.multiple_of(j * tk, tk)
            vt_ref[0:LANES, pl.ds(r0, tk)] = v_ref[pl.ds(r0, tk), :].astype(F32).T.astype(BF16)
            vt_ref[LANES:FLASH_VT_ROWS, pl.ds(r0, tk)] = ones_row
            kf = k_ref[pl.ds(r0, tk), :].astype(F32)
            sq = (kf * kf).astype(BF16)
            s1 = jnp.dot(sq, first_half, preferred_element_type=F32)
            s2 = jnp.dot(sq, second_half, preferred_element_type=F32)
            return (jnp.maximum(n1, jnp.max(s1, axis=0, keepdims=True)),
                    jnp.maximum(n2, jnp.max(s2, axis=0, keepdims=True)))

        zero = jnp.zeros((1, LANES), F32)
        n1, n2 = lax.fori_loop(0, n_kv, prep, (zero, zero))
        kmax_ref[0:1, :] = n1
        kmax_ref[1:2, :] = n2

    q = q_ref[...]
    qlane = lax.broadcasted_iota(jnp.int32, q.shape, 1)
    zero = jnp.zeros_like(q)
    qq = jnp.concatenate([jnp.where(qlane < DA_HEAD_DIM, q, zero),
                          jnp.where(qlane >= DA_HEAD_DIM, q, zero)], axis=0)
    qf = qq.astype(F32)
    nt = (((1,), (1,)), ((), ()))
    qn = lax.dot_general(jnp.ones((8, LANES), BF16), (qf * qf).astype(BF16), nt,
                         preferred_element_type=F32)[0:1]
    col = lax.broadcasted_iota(jnp.int32, qn.shape, 1)
    kn = jnp.where(col < tq, kmax_ref[0:1, 0:1], kmax_ref[1:2, 0:1])
    bound = jnp.sqrt(qn * kn) * 1.01
    bound_ref[...] = bound
    acc_ref[...] = jnp.zeros(acc_ref.shape, F32)
    safe = jnp.max(bound) <= 60.0

    def scores(j):
        r0 = pl.multiple_of(j * tk, tk)
        st = lax.dot_general(k_ref[pl.ds(r0, tk), :], qq, nt, preferred_element_type=F32)
        return st, vt_ref[:, pl.ds(r0, tk)]

    @pl.when(safe)
    def _():
        def body(j, carry):
            st, vt = scores(j)
            p = jnp.exp2(st - bound_ref[...]).astype(BF16)
            acc_ref[...] += jnp.dot(vt, p, preferred_element_type=F32)
            return carry

        lax.fori_loop(0, n_kv, body, 0, unroll=2)

    @pl.when(jnp.logical_not(safe))
    def _():
        mrun_ref[...] = jnp.full(mrun_ref.shape, -jnp.inf, F32)

        def body(j, carry):
            st, vt = scores(j)
            m_old = mrun_ref[...]
            m_new = jnp.maximum(m_old, jnp.max(st, axis=0, keepdims=True))
            alpha = jnp.exp2(m_old - m_new)
            p = jnp.exp2(st - m_new).astype(BF16)
            acc_ref[...] = alpha * acc_ref[...] + jnp.dot(vt, p, preferred_element_type=F32)
            mrun_ref[...] = m_new
            return carry

        lax.fori_loop(0, n_kv, body, 0)

    lf = lam_ref[...]
    lam = (jnp.exp(jnp.sum(lf[0:1] * lf[1:2], axis=1, keepdims=True))
           - jnp.exp(jnp.sum(lf[2:3] * lf[3:4], axis=1, keepdims=True)) + lambda_init)
    o1 = acc_ref[0:LANES, 0:tq] / acc_ref[LANES:LANES + 1, 0:tq]
    o2 = acc_ref[0:LANES, tq:2 * tq] / acc_ref[LANES:LANES + 1, tq:2 * tq]
    o = (o1 - lam * o2).T
    ms = jnp.mean(o * o, axis=-1, keepdims=True)
    o = o * lax.rsqrt(ms + EPS) * subln_ref[...] * (1.0 - lambda_init)
    o_ref[...] = o.astype(BF16)


def _flash(qkv, lam_vecs, subln_g, *, batch, seq, lambda_init, tq, tk):
    t = batch * seq
    width = DA_HEADS * LANES
    q_tiles = seq // tq
    return pl.pallas_call(
        functools.partial(_flash_kernel, tq=tq, tk=tk, lambda_init=lambda_init),
        grid=(batch, DA_HEADS, q_tiles),
        in_specs=[
            pl.BlockSpec((None, tq, LANES), lambda b, h, i: (0, b * q_tiles + i, h)),
            pl.BlockSpec((None, seq, LANES), lambda b, h, i: (1, b, h)),
            pl.BlockSpec((None, seq, LANES), lambda b, h, i: (2, b, h)),
            _const_spec((4, DA_HEAD_DIM)),
            _const_spec((1, DA_V_DIM)),
        ],
        out_specs=pl.BlockSpec((tq, LANES), lambda b, h, i: (b * q_tiles + i, h)),
        out_shape=jax.ShapeDtypeStruct((t, width), BF16),
        scratch_shapes=[
            pltpu.VMEM((FLASH_VT_ROWS, seq), BF16),
            pltpu.VMEM((8, LANES), F32),
            pltpu.VMEM((1, 2 * tq), F32),
            pltpu.VMEM((1, 2 * tq), F32),
            pltpu.VMEM((FLASH_VT_ROWS, 2 * tq), F32),
        ],
        compiler_params=_params("parallel", "parallel", "arbitrary"),
        name="diff_flash_attention",
    )(qkv, qkv, qkv, lam_vecs.astype(F32), subln_g.reshape(1, DA_V_DIM).astype(F32))


def _identity_prologue(o_ref):
    return o_ref[...]


def _diff_attention_layer(x, g, shift, scale, gate, w_in, q_norm, k_norm, lam_vecs, subln_g, w_out,
                          lambda_init, *, batch, seq):
    cos, sin, rot, grp = _rope_constants(seq)
    qkv = _attn_in_proj(x, g, shift, scale, w_in.astype(BF16), q_norm, k_norm, cos, sin, rot, grp,
                        seq=seq, tm=min(1024, seq))
    o = _flash(qkv, lam_vecs, subln_g, batch=batch, seq=seq, lambda_init=lambda_init,
               tq=min(1024, seq), tk=min(512, seq))
    tm = min(512, seq)
    return _out_proj(_identity_prologue, [o], [pl.BlockSpec((tm, o.shape[1]), lambda i: (i, 0))],
                     w_out.astype(BF16), x, gate, seq=seq, tm=tm)


def _hgrn_scan_kernel(q_ref, f_ref, v_ref, lb_ref, o_ref, st_ref, *, reverse, layer):
    c = HG_CHUNK
    n_chunks = q_ref.shape[0] // c

    @pl.when(pl.program_id(1) == 0)
    def _():
        st_ref[...] = jnp.zeros(st_ref.shape, F32)

    lbw = lb_ref[...]
    e = jnp.exp(lbw - jnp.max(lbw, axis=0, keepdims=True))
    lb = jnp.sum(e[1:layer + 1], axis=0, keepdims=True) / jnp.sum(e, axis=0, keepdims=True)

    row = lax.broadcasted_iota(jnp.int32, (c, c), 0)
    col = lax.broadcasted_iota(jnp.int32, (c, c), 1)
    tri = (col >= row) if reverse else (col <= row)
    tri = tri.astype(F32)
    n_sub = c // HG_SUB
    half = HG_SUB // 2
    col8 = lax.broadcasted_iota(jnp.int32, (half, c), 1)
    row8 = lax.broadcasted_iota(jnp.int32, (half, c), 0)

    def chunk(ci, carry):
        cidx = (n_chunks - 1 - ci) if reverse else ci
        r0 = pl.multiple_of(cidx * c, c)
        qs = _silu(q_ref[pl.ds(r0, c), :])
        forget = lb + (1.0 - lb) * _sigmoid(f_ref[pl.ds(r0, c), :])
        kk = 1.0 - forget
        vb = v_ref[pl.ds(r0, c), :].astype(BF16)
        b = jnp.dot(tri, jnp.log2(forget), preferred_element_type=F32, precision=HIGHEST)
        b_edge = b[0:1] if reverse else b[c - 1:c]
        q_in = (qs * jnp.exp2(b)).astype(BF16)
        k_out = (kk * jnp.exp2(b_edge - b)).astype(BF16)
        dec = jnp.exp2(b_edge)

        pieces = [[jnp.zeros((half, c), F32) for _ in range(2 * n_sub)] for _ in range(HG_HEADS)]
        for i in range(n_sub):
            for r in range(2):
                t0 = i * HG_SUB + r * half
                q8 = qs[t0:t0 + half]
                b8 = b[t0:t0 + half]
                if reverse:
                    s_range = range(t0, (i + 1) * HG_SUB)
                else:
                    s_range = range(i * HG_SUB, t0 + half)
                for s in s_range:
                    term = q8 * kk[s:s + 1] * jnp.exp2(b8 - b[s:s + 1])
                    reached = (row8 + t0 <= s) if reverse else (row8 + t0 >= s)
                    place = jnp.logical_and(col8 == s, reached)
                    for h in range(HG_HEADS):
                        a_col = jnp.sum(term[:, h * LANES:(h + 1) * LANES], axis=1, keepdims=True)
                        pieces[h][2 * i + r] = jnp.where(place, a_col, pieces[h][2 * i + r])

        for h in range(HG_HEADS):
            sl = slice(h * LANES, (h + 1) * LANES)
            blocks = []
            for i in range(n_sub):
                rows = slice(i * HG_SUB, (i + 1) * HG_SUB)
                passed = slice((i + 1) * HG_SUB, c) if reverse else slice(0, i * HG_SUB)
                n_passed = passed.stop - passed.start
                if n_passed == 0:
                    blocks.append(jnp.zeros((HG_SUB, c), F32))
                    continue
                ref_row = (i + 1) * HG_SUB if reverse else i * HG_SUB - 1
                b_ref_row = b[ref_row:ref_row + 1, sl]
                q_t = (qs[rows, sl] * jnp.exp2(b[rows, sl] - b_ref_row)).astype(BF16)
                k_p = (kk[passed, sl] * jnp.exp2(b_ref_row - b[passed, sl])).astype(BF16)
                rest = jnp.zeros((c - n_passed, LANES), BF16)
                k_t = jnp.concatenate([rest, k_p] if reverse else [k_p, rest], axis=0)
                blocks.append(lax.dot_general(q_t, k_t, (((1,), (1,)), ((), ())),
                                              preferred_element_type=F32))
            a = jnp.concatenate(pieces[h], axis=0) + jnp.concatenate(blocks, axis=0)
            st = st_ref[h]
            o_h = jnp.dot(a.astype(BF16), vb[:, sl], preferred_element_type=F32)
            o_h = o_h + lax.dot_general(q_in[:, sl], st.astype(BF16), (((1,), (1,)), ((), ())),
                                        preferred_element_type=F32)
            o_ref[pl.ds(r0, c), sl] = o_h
            upd = lax.dot_general(vb[:, sl], k_out[:, sl], (((0,), (0,)), ((), ())),
                                  preferred_element_type=F32)
            st_ref[h] = st * dec[:, sl] + upd
        return carry

    lax.fori_loop(0, n_chunks, chunk, 0)


def _hgrn_scan(proj, lower_bounds, *, batch, seq, layer, reverse, tt):
    t = batch * seq
    width = HG_HEADS * LANES
    nt = seq // tt
    depth = lower_bounds.shape[1]
    d_idx = 1 if reverse else 0

    def rows(b, i):
        return b * nt + ((nt - 1 - i) if reverse else i)

    return pl.pallas_call(
        functools.partial(_hgrn_scan_kernel, reverse=reverse, layer=layer),
        grid=(batch, nt),
        in_specs=[
            pl.BlockSpec((tt, width), lambda b, i: (rows(b, i), 0)),
            pl.BlockSpec((tt, width), lambda b, i: (rows(b, i), 1 + d_idx)),
            pl.BlockSpec((tt, width), lambda b, i: (rows(b, i), 3)),
            pl.BlockSpec((None, depth, width), lambda b, i: (d_idx, 0, 0)),
        ],
        out_specs=pl.BlockSpec((tt, width), lambda b, i: (rows(b, i), 0)),
        out_shape=jax.ShapeDtypeStruct((t, width), F32),
        scratch_shapes=[pltpu.VMEM((HG_HEADS, LANES, LANES), F32)],
        compiler_params=_params("parallel", "arbitrary"),
        name="hgrn2_scan_bwd" if reverse else "hgrn2_scan_fwd",
    )(proj, proj, proj, lower_bounds.astype(F32))


def _hgrn_prologue(of_ref, ob_ref, g_ref, ng_ref):
    ng = ng_ref[...]
    outs = []
    for h in range(HG_HEADS):
        sl = slice(h * LANES, (h + 1) * LANES)
        o = of_ref[:, sl] + ob_ref[:, sl]
        ms = jnp.mean(o * o, axis=-1, keepdims=True)
        outs.append((o * lax.rsqrt(ms + EPS) * ng * _silu(g_ref[:, sl])).astype(BF16))
    return jnp.concatenate(outs, axis=1)


def _hgrn_layer(x, g, shift, scale, gate, w_in, lower_bounds, norm_g, w_out, layer, *, batch, seq):
    width = HG_HEADS * LANES
    proj = _nm_matmul(x, g, shift, scale, w_in.astype(BF16), seq=seq,
                      tm=min(1024, seq), tn=1024, out_dtype=F32)
    tt = min(512, seq)
    o_f = _hgrn_scan(proj, lower_bounds, batch=batch, seq=seq, layer=layer, reverse=False, tt=tt)
    o_b = _hgrn_scan(proj, lower_bounds, batch=batch, seq=seq, layer=layer, reverse=True, tt=tt)
    tm = min(512, seq)
    ins = [o_f, o_b, proj, norm_g.reshape(1, LANES).astype(F32)]
    specs = [
        pl.BlockSpec((tm, width), lambda i: (i, 0)),
        pl.BlockSpec((tm, width), lambda i: (i, 0)),
        pl.BlockSpec((tm, width), lambda i: (i, 4)),
        _const_spec((1, LANES)),
    ]
    return _out_proj(_hgrn_prologue, ins, specs, w_out.astype(BF16), x, gate, seq=seq, tm=tm)


def _lru_scan_kernel(prev_ref, main_ref, next_ref, cw_ref, cb_ref, wg_ref, ba_ref, bx_ref, lam_ref,
                     o_ref, a_ref, u_ref, h_ref, *, batch, nt):
    d = pl.program_id(0)
    i = pl.program_id(1)
    ti = i + d * (nt - 1 - 2 * i)
    rows, width = main_ref.shape
    tt = rows // batch

    @pl.when(i == 0)
    def _():
        h_ref[...] = jnp.zeros(h_ref.shape, F32)

    prev = jnp.where(ti == 0, 0.0, prev_ref[...])
    nxt = jnp.where(ti == nt - 1, 0.0, next_ref[...])
    ext = jnp.concatenate([prev, main_ref[...], nxt], axis=0)
    cw = cw_ref[...]
    xc = cb_ref[...] + cw[0:1] * ext[0:rows]
    for j in range(1, CONV_WIDTH):
        xc = xc + cw[j:j + 1] * ext[j * batch:j * batch + rows]

    neg_softplus = -LRU_C * (jnp.maximum(-lam_ref[...], 0.0) + jnp.log1p(jnp.exp(-jnp.abs(lam_ref[...]))))
    for n in range(width // LRU_BLOCK_W):
        sl = slice(n * LRU_BLOCK_W, (n + 1) * LRU_BLOCK_W)
        xb = xc[:, sl]
        gates = jnp.dot(xb.astype(BF16), wg_ref[n], preferred_element_type=F32)
        r = _sigmoid(gates[:, :LRU_BLOCK_W] + ba_ref[:, sl])
        ig = _sigmoid(gates[:, LRU_BLOCK_W:] + bx_ref[:, sl])
        a = jnp.exp(r * neg_softplus[:, sl])
        a_ref[:, sl] = a
        u_ref[:, sl] = jnp.sqrt(1.0 - a * a) * (ig * xb)

    def step(t, h):
        tl = t + d * (tt - 1 - 2 * t)
        r0 = pl.multiple_of(tl * batch, batch)
        h = a_ref[pl.ds(r0, batch), :] * h + u_ref[pl.ds(r0, batch), :]
        o_ref[pl.ds(r0, batch), :] = h
        return h

    h_ref[...] = lax.fori_loop(0, tt, step, h_ref[...])


def _lru_scan(proj, conv_w, conv_b, w_a, b_a, w_x, b_x, lam, *, batch, seq, tt):
    t = batch * seq
    width = conv_w.shape[1]
    n_blocks = width // LRU_BLOCK_W
    nt = seq // tt
    rows = tt * batch
    assert tt % 2 == 0

    def tile(d, i):
        return i + d * (nt - 1 - 2 * i)

    wg = jnp.concatenate([w_a, w_x], axis=-1).astype(BF16)
    vec = lambda a: a.reshape(2, 1, width).astype(F32)
    return pl.pallas_call(
        functools.partial(_lru_scan_kernel, batch=batch, nt=nt),
        grid=(2, nt),
        in_specs=[
            pl.BlockSpec((batch, width), lambda d, i: (jnp.maximum(tile(d, i) * tt - 1, 0), 1)),
            pl.BlockSpec((rows, width), lambda d, i: (tile(d, i), 1)),
            pl.BlockSpec((2 * batch, width),
                         lambda d, i: (jnp.minimum((tile(d, i) + 1) * (tt // 2), seq // 2 - 1), 1)),
            _const_spec((CONV_WIDTH, width)),
            _const_spec((1, width)),
            pl.BlockSpec((None, n_blocks, LRU_BLOCK_W, 2 * LRU_BLOCK_W), lambda d, i: (d, 0, 0, 0)),
            pl.BlockSpec((None, 1, width), lambda d, i: (d, 0, 0)),
            pl.BlockSpec((None, 1, width), lambda d, i: (d, 0, 0)),
            pl.BlockSpec((None, 1, width), lambda d, i: (d, 0, 0)),
        ],
        out_specs=pl.BlockSpec((None, rows, width), lambda d, i: (d, tile(d, i), 0)),
        out_shape=jax.ShapeDtypeStruct((2, t, width), F32),
        scratch_shapes=[
            pltpu.VMEM((rows, width), F32),
            pltpu.VMEM((rows, width), F32),
            pltpu.VMEM((batch, width), F32),
        ],
        compiler_params=_params("arbitrary", "arbitrary"),
        name="rglru_scan",
    )(proj, proj, proj, conv_w.astype(F32), conv_b.reshape(1, width).astype(F32), wg,
      vec(b_a), vec(b_x), vec(lam))


def _gelu_tanh(x):
    return 0.5 * x * (1.0 + jnp.tanh(math.sqrt(2.0 / math.pi) * (x + 0.044715 * (x * x * x))))


def _lru_in_kernel(x_ref, g_ref, sh_ref, sc_ref, w_ref, o_ref, xm_ref, h_ref, *, batch, ts):
    n_lane_blocks = x_ref.shape[2] // LANES

    @pl.when(pl.program_id(1) == 0)
    def _():
        for b in range(batch):
            xm = _modulated(x_ref[b], g_ref[...], sh_ref[b:b + 1, :], sc_ref[b:b + 1, :])
            for c in range(n_lane_blocks):
                xm_ref[c, b * ts:(b + 1) * ts, :] = xm[:, c * LANES:(c + 1) * LANES]
        for s in range(ts):
            for c in range(n_lane_blocks):
                h_ref[s * batch:(s + 1) * batch, c * LANES:(c + 1) * LANES] = \
                    xm_ref[c, pl.ds(s, batch, stride=ts), :]

    o_ref[...] = jnp.dot(h_ref[...].astype(BF16), w_ref[...], preferred_element_type=F32)


def _lru_in_proj(x, g, shift, scale, w, *, batch, seq, ts, tn):
    t, d = x.shape
    n = w.shape[1]
    rows = ts * batch
    return pl.pallas_call(
        functools.partial(_lru_in_kernel, batch=batch, ts=ts),
        grid=(seq // ts, n // tn),
        in_specs=[
            pl.BlockSpec((batch, ts, d), lambda i, j: (0, i, 0)),
            _const_spec((1, d)),
            _const_spec((batch, d)),
            _const_spec((batch, d)),
            pl.BlockSpec((d, tn), lambda i, j: (0, j)),
        ],
        out_specs=pl.BlockSpec((rows, tn), lambda i, j: (i, j)),
        out_shape=jax.ShapeDtypeStruct((t, n), F32),
        scratch_shapes=[pltpu.VMEM((d // LANES, rows, LANES), F32), pltpu.VMEM((rows, d), F32)],
        compiler_params=_params("parallel", "arbitrary"),
        name="rglru_in_proj",
    )(x.reshape(batch, seq, d), g.reshape(1, d), shift, scale, w)


def _lru_out_kernel(hf_ref, hb_ref, y_ref, w_ref, x_ref, gate_ref, o_ref, mix_ref, *, batch, ts):
    a = ((hf_ref[...] + hb_ref[...]) * _gelu_tanh(y_ref[...])).astype(BF16)
    mix = jnp.dot(a, w_ref[...], preferred_element_type=F32)
    n_lane_blocks = mix.shape[1] // LANES
    for c in range(n_lane_blocks):
        mix_ref[c] = mix[:, c * LANES:(c + 1) * LANES]
    for b in range(batch):
        for c in range(n_lane_blocks):
            sl = slice(c * LANES, (c + 1) * LANES)
            o_ref[b, :, sl] = x_ref[b, :, sl] + gate_ref[b:b + 1, sl] * mix_ref[c, pl.ds(b, ts, stride=batch), :]


def _lru_out_proj(h2, proj, w, x, gate, *, batch, seq, ts):
    t, d = x.shape
    width = w.shape[0]
    rows = ts * batch
    out = pl.pallas_call(
        functools.partial(_lru_out_kernel, batch=batch, ts=ts),
        grid=(seq // ts,),
        in_specs=[
            pl.BlockSpec((None, rows, width), lambda i: (0, i, 0)),
            pl.BlockSpec((None, rows, width), lambda i: (1, i, 0)),
            pl.BlockSpec((rows, width), lambda i: (i, 0)),
            _const_spec((width, d)),
            pl.BlockSpec((batch, ts, d), lambda i: (0, i, 0)),
            _const_spec((batch, d)),
        ],
        out_specs=pl.BlockSpec((batch, ts, d), lambda i: (0, i, 0)),
        out_shape=jax.ShapeDtypeStruct((batch, seq, d), F32),
        scratch_shapes=[pltpu.VMEM((d // LANES, rows, LANES), F32)],
        compiler_params=_params("parallel"),
        name="rglru_out_proj",
    )(h2, h2, proj, w, x.reshape(batch, seq, d), gate)
    return out.reshape(t, d)


def _rglru_layer(x, g, shift, scale, gate, w_in, conv_w, conv_b, w_a, b_a, w_x, b_x, lam, w_out,
                 *, batch, seq):
    width = conv_w.shape[1]
    ts = min(64, seq)
    proj = _lru_in_proj(x, g, shift, scale, w_in.astype(BF16), batch=batch, seq=seq, ts=ts, tn=width)
    h2 = _lru_scan(proj, conv_w, conv_b, w_a, b_a, w_x, b_x, lam, batch=batch, seq=seq, tt=min(64, seq))
    return _lru_out_proj(h2, proj, w_out.astype(BF16), x, gate, batch=batch, seq=seq, ts=ts)


def _ffn_kernel(x_ref, g_ref, sh_ref, sc_ref, wg_ref, wu_ref, wd_ref, gate_ref, o_ref, h_ref, acc_ref):
    j = pl.program_id(1)

    @pl.when(j == 0)
    def _():
        h_ref[...] = _modulated(x_ref[...], g_ref[...], sh_ref[...], sc_ref[...]).astype(BF16)
        acc_ref[...] = jnp.zeros(acc_ref.shape, F32)

    h = h_ref[...]
    gt = jnp.dot(h, wg_ref[...], preferred_element_type=F32)
    up = jnp.dot(h, wu_ref[...], preferred_element_type=F32)
    act = (_silu(gt) * up).astype(BF16)
    acc_ref[...] += jnp.dot(act, wd_ref[...], preferred_element_type=F32)

    @pl.when(j == pl.num_programs(1) - 1)
    def _():
        o_ref[...] = x_ref[...] + gate_ref[...] * acc_ref[...]


def _dense_ffn(x, g, shift, scale, gate, w_gu, w_down, *, seq, tm, tf):
    t, d = x.shape
    f = w_down.shape[0]
    assert f % tf == 0 and seq % tm == 0
    nf = f // tf
    sh = _RowVec(shift, tm, seq)
    sc = _RowVec(scale, tm, seq)
    gv = _RowVec(gate, tm, seq)
    return pl.pallas_call(
        _ffn_kernel,
        grid=(t // tm, nf),
        in_specs=[
            pl.BlockSpec((tm, d), lambda i, j: (i, 0)),
            _const_spec((1, d)),
            sh.spec, sc.spec,
            pl.BlockSpec((d, tf), lambda i, j: (0, j)),
            pl.BlockSpec((d, tf), lambda i, j: (0, nf + j)),
            pl.BlockSpec((tf, d), lambda i, j: (j, 0)),
            gv.spec,
        ],
        out_specs=pl.BlockSpec((tm, d), lambda i, j: (i, 0)),
        out_shape=jax.ShapeDtypeStruct((t, d), F32),
        scratch_shapes=[pltpu.VMEM((tm, d), BF16), pltpu.VMEM((tm, d), F32)],
        compiler_params=_params("parallel", "arbitrary"),
        name="dense_swiglu",
    )(x, g.reshape(1, d), sh.array, sc.array, w_gu, w_gu, w_down, gv.array)


MOE_BLOCK = 512
MOE_CHUNK = 16
MOE_TILE = 1024


def _route_kernel(x_ref, g_ref, sh_ref, sc_ref, rhi_ref, rlo_ref, xs_ref, meta_ref, cnt_ref):
    tb = x_ref.shape[0]
    local_rows = xs_ref.shape[0]
    hf = _modulated(x_ref[...], g_ref[...], sh_ref[...], sc_ref[...])
    h_hi = hf.astype(BF16)
    h_lo = (hf - h_hi.astype(F32)).astype(BF16)
    logits = (jnp.dot(h_hi, rhi_ref[...], preferred_element_type=F32)
              + jnp.dot(h_lo, rhi_ref[...], preferred_element_type=F32)
              + jnp.dot(h_hi, rlo_ref[...], preferred_element_type=F32))
    lane = lax.broadcasted_iota(jnp.int32, logits.shape, 1).astype(F32)
    logits = jnp.where(lane < N_EXPERTS, logits, -jnp.inf)
    m1 = jnp.max(logits, axis=1, keepdims=True)
    e1 = jnp.min(jnp.where(logits == m1, lane, float(LANES)), axis=1, keepdims=True)
    rest = jnp.where(lane == e1, -jnp.inf, logits)
    m2 = jnp.max(rest, axis=1, keepdims=True)
    e2 = jnp.min(jnp.where(rest == m2, lane, float(LANES)), axis=1, keepdims=True)
    gate2 = 1.0 / (1.0 + jnp.exp(m1 - m2))
    gate1 = 1.0 - gate2
    hot1 = (lane == e1).astype(F32)
    hot2 = (lane == e2).astype(F32)
    sel = hot1 + hot2
    row = lax.broadcasted_iota(jnp.int32, (tb, tb), 0)
    col = lax.broadcasted_iota(jnp.int32, (tb, tb), 1)
    before = (col < row).astype(BF16)
    seen = jnp.dot(before, sel.astype(BF16), preferred_element_type=F32)
    chunks = jnp.floor((jnp.sum(sel, axis=0, keepdims=True) + (MOE_CHUNK - 1.0)) * (1.0 / MOE_CHUNK))
    cnt_ref[...] = chunks
    lrow = lax.broadcasted_iota(jnp.int32, (LANES, LANES), 0)
    lcol = lax.broadcasted_iota(jnp.int32, (LANES, LANES), 1)
    lower_experts = (lrow < lcol).astype(BF16)
    first_chunk = jnp.dot(jnp.broadcast_to(chunks, (8, LANES)).astype(BF16), lower_experts,
                          preferred_element_type=F32)[0:1]
    base = first_chunk * float(MOE_CHUNK) + seen
    ld1 = jnp.sum(hot1 * base, axis=1, keepdims=True)
    ld2 = jnp.sum(hot2 * base, axis=1, keepdims=True)
    meta = jnp.zeros(logits.shape, F32)
    for k, val in enumerate((e1, e2, gate1, gate2, ld1, ld2)):
        meta = jnp.where(lane == float(k), val, meta)
    meta_ref[...] = meta
    eye = row == col
    ld1_row = jnp.sum(jnp.where(eye, ld1, 0.0), axis=0, keepdims=True)
    ld2_row = jnp.sum(jnp.where(eye, ld2, 0.0), axis=0, keepdims=True)
    r_iota = lax.broadcasted_iota(jnp.int32, (local_rows, tb), 0).astype(F32)
    pick = jnp.where(r_iota == ld1_row, 1.0, jnp.where(r_iota == ld2_row, 1.0, 0.0)).astype(BF16)
    xs_ref[...] = jnp.dot(pick, h_hi, preferred_element_type=F32).astype(BF16)


def _route(x, g, shift, scale, router, *, seq):
    t, d = x.shape
    tb = min(MOE_BLOCK, seq)
    local_rows = TOP_K * tb + N_EXPERTS * MOE_CHUNK
    nb = t // tb
    sh = _RowVec(shift, tb, seq)
    sc = _RowVec(scale, tb, seq)
    router_pad = jnp.zeros((d, LANES), F32).at[:, :N_EXPERTS].set(router.astype(F32))
    router_hi = router_pad.astype(BF16)
    router_lo = (router_pad - router_hi.astype(F32)).astype(BF16)
    return pl.pallas_call(
        _route_kernel,
        grid=(nb,),
        in_specs=[
            pl.BlockSpec((tb, d), lambda i: (i, 0)),
            _const_spec((1, d)),
            sh.spec, sc.spec,
            _const_spec((d, LANES)),
            _const_spec((d, LANES)),
        ],
        out_specs=[
            pl.BlockSpec((local_rows, d), lambda i: (i, 0)),
            pl.BlockSpec((tb, LANES), lambda i: (i, 0)),
            pl.BlockSpec((None, 1, LANES), lambda i: (i, 0, 0)),
        ],
        out_shape=[
            jax.ShapeDtypeStruct((nb * local_rows, d), BF16),
            jax.ShapeDtypeStruct((t, LANES), F32),
            jax.ShapeDtypeStruct((nb, 1, LANES), F32),
        ],
        compiler_params=_params("parallel"),
        name="moe_route",
    )(x, g.reshape(1, d), sh.array, sc.array, router_hi, router_lo)


def _start_chunk_gather(row_smem, slot, src_hbm, dst_vmem, sem):
    n = dst_vmem.shape[1] // MOE_CHUNK

    def start(k, c):
        r0 = pl.multiple_of(row_smem[slot, k], MOE_CHUNK)
        pltpu.make_async_copy(src_hbm.at[pl.ds(r0, MOE_CHUNK)],
                              dst_vmem.at[slot, pl.ds(pl.multiple_of(k * MOE_CHUNK, MOE_CHUNK), MOE_CHUNK)],
                              sem.at[slot]).start()
        return c

    lax.fori_loop(0, n, start, 0, unroll=8)


def _wait_chunk_gather(slot, src_hbm, dst_vmem, sem):
    n = dst_vmem.shape[1]
    pltpu.make_async_copy(src_hbm.at[pl.ds(0, n)], dst_vmem.at[slot], sem.at[slot]).wait()


def _expert_kernel(te_ref, nu_ref, src_ref, xs_ref, wg_ref, wu_ref, wd_ref, o_ref,
                   idx_ref, xg_ref, acc_ref, sem_idx, sem_rows):
    i = pl.program_id(0)
    j = pl.program_id(1)
    n_used = nu_ref[0]
    used = i < n_used
    slot = i % 2

    def idx_copy(tile, into):
        return pltpu.make_async_copy(src_ref.at[tile], idx_ref.at[into], sem_idx)

    @pl.when(jnp.logical_and(used, j == 0))
    def _():
        @pl.when(i == 0)
        def _():
            idx_copy(0, 0).start()
            idx_copy(0, 0).wait()
            _start_chunk_gather(idx_ref, 0, xs_ref, xg_ref, sem_rows)

        has_next = i + 1 < n_used

        @pl.when(has_next)
        def _():
            idx_copy(i + 1, 1 - slot).start()

        _wait_chunk_gather(slot, xs_ref, xg_ref, sem_rows)
        acc_ref[...] = jnp.zeros(acc_ref.shape, F32)

        @pl.when(has_next)
        def _():
            idx_copy(i + 1, 1 - slot).wait()
            _start_chunk_gather(idx_ref, 1 - slot, xs_ref, xg_ref, sem_rows)

    @pl.when(used)
    def _():
        h = xg_ref[slot]
        gt = jnp.dot(h, wg_ref[...], preferred_element_type=F32)
        up = jnp.dot(h, wu_ref[...], preferred_element_type=F32)
        act = (_silu(gt) * up).astype(BF16)
        acc_ref[...] += jnp.dot(act, wd_ref[...], preferred_element_type=F32)

    @pl.when(j == pl.num_programs(1) - 1)
    def _():
        o_ref[...] = jnp.where(used, acc_ref[...], 0.0).astype(o_ref.dtype)


def _experts(xs, src_tiles, tile_expert, n_used, w_gu, w_down, *, tm, tf):
    d = xs.shape[1]
    n_tiles, chunks_per_tile = src_tiles.shape
    assert chunks_per_tile * MOE_CHUNK == tm
    f = w_down.shape[1]
    nf = f // tf

    def jj(i, j, nu):
        return jnp.where(i < nu[0], j, nf - 1)

    grid_spec = pltpu.PrefetchScalarGridSpec(
        num_scalar_prefetch=2,
        grid=(n_tiles, nf),
        in_specs=[
            pl.BlockSpec(memory_space=pl.ANY),
            pl.BlockSpec(memory_space=pl.ANY),
            pl.BlockSpec((None, d, tf), lambda i, j, te, nu: (te[i], 0, jj(i, j, nu))),
            pl.BlockSpec((None, d, tf), lambda i, j, te, nu: (te[i], 0, nf + jj(i, j, nu))),
            pl.BlockSpec((None, tf, d), lambda i, j, te, nu: (te[i], jj(i, j, nu), 0)),
        ],
        out_specs=pl.BlockSpec((tm, d), lambda i, j, te, nu: (i, 0)),
        scratch_shapes=[
            pltpu.SMEM((2, chunks_per_tile), jnp.int32),
            pltpu.VMEM((2, tm, d), BF16),
            pltpu.VMEM((tm, d), F32),
            pltpu.SemaphoreType.DMA,
            pltpu.SemaphoreType.DMA((2,)),
        ],
    )
    return pl.pallas_call(
        _expert_kernel,
        grid_spec=grid_spec,
        out_shape=jax.ShapeDtypeStruct((n_tiles * tm, d), BF16),
        compiler_params=_params("arbitrary", "arbitrary"),
        name="moe_experts",
    )(tile_expert, n_used, src_tiles, xs, w_gu, w_gu, w_down)


def _combine_kernel(src_ref, ys_ref, x_ref, meta_ref, gate_ref, o_ref, idx_ref, yl_ref, sem_idx, sem_rows):
    i = pl.program_id(0)
    slot = i % 2
    has_next = i + 1 < pl.num_programs(0)

    def idx_copy(block, into):
        return pltpu.make_async_copy(src_ref.at[block], idx_ref.at[into], sem_idx)

    @pl.when(i == 0)
    def _():
        idx_copy(0, 0).start()
        idx_copy(0, 0).wait()
        _start_chunk_gather(idx_ref, 0, ys_ref, yl_ref, sem_rows)

    @pl.when(has_next)
    def _():
        idx_copy(i + 1, 1 - slot).start()

    _wait_chunk_gather(slot, ys_ref, yl_ref, sem_rows)

    @pl.when(has_next)
    def _():
        idx_copy(i + 1, 1 - slot).wait()
        _start_chunk_gather(idx_ref, 1 - slot, ys_ref, yl_ref, sem_rows)

    meta = meta_ref[...]
    yl = yl_ref[slot]
    lane = lax.broadcasted_iota(jnp.int32, (meta.shape[0], yl.shape[0]), 1).astype(F32)
    y1 = jnp.dot(jnp.where(lane == meta[:, 4:5], 1.0, 0.0).astype(BF16), yl, preferred_element_type=F32)
    y2 = jnp.dot(jnp.where(lane == meta[:, 5:6], 1.0, 0.0).astype(BF16), yl, preferred_element_type=F32)
    o_ref[...] = x_ref[...] + gate_ref[...] * (meta[:, 2:3] * y1 + meta[:, 3:4] * y2)


def _combine(ys, src_blocks, x, meta, gate, *, seq):
    t, d = x.shape
    nb, chunks_per_block = src_blocks.shape
    tb = t // nb
    local_rows = chunks_per_block * MOE_CHUNK
    gv = _RowVec(gate, tb, seq)
    return pl.pallas_call(
        _combine_kernel,
        grid=(nb,),
        in_specs=[
            pl.BlockSpec(memory_space=pl.ANY),
            pl.BlockSpec(memory_space=pl.ANY),
            pl.BlockSpec((tb, d), lambda i: (i, 0)),
            pl.BlockSpec((tb, LANES), lambda i: (i, 0)),
            gv.spec,
        ],
        out_specs=pl.BlockSpec((tb, d), lambda i: (i, 0)),
        out_shape=jax.ShapeDtypeStruct((t, d), F32),
        scratch_shapes=[
            pltpu.SMEM((2, chunks_per_block), jnp.int32),
            pltpu.VMEM((2, local_rows, d), BF16),
            pltpu.SemaphoreType.DMA,
            pltpu.SemaphoreType.DMA((2,)),
        ],
        compiler_params=_params("arbitrary"),
        name="moe_combine",
    )(src_blocks, ys, x, meta, gv.array)


def _moe_ffn(x, g, shift, scale, gate, router, w_gu, w_down, *, seq):
    t, d = x.shape
    tb = min(MOE_BLOCK, seq)
    tm_e = min(MOE_TILE, seq)
    xs, meta, chunks = _route(x, g, shift, scale, router, seq=seq)
    nb = t // tb
    cpb = (TOP_K * tb + N_EXPERTS * MOE_CHUNK) // MOE_CHUNK
    cpt = tm_e // MOE_CHUNK

    nch = chunks[:, 0, :N_EXPERTS].astype(jnp.int32)
    local_first = jnp.cumsum(nch, axis=1) - nch
    per_expert = jnp.sum(nch, axis=0)
    tiles = (per_expert + cpt - 1) // cpt
    sorted_end = jnp.cumsum(tiles) * cpt
    sorted_first = sorted_end - tiles * cpt
    in_expert_end = jnp.cumsum(nch, axis=0)
    in_expert_first = in_expert_end - nch

    n_tiles = -(-(TOP_K * t // MOE_CHUNK + nb * N_EXPERTS) // cpt) + N_EXPERTS
    slot = jnp.arange(n_tiles * cpt, dtype=jnp.int32)
    e_of = jnp.minimum(jnp.sum(slot[:, None] >= sorted_end[None, :], axis=1), N_EXPERTS - 1)
    off = slot - sorted_first[e_of]
    valid = off < per_expert[e_of]
    b_of = jnp.minimum(jnp.sum(off[:, None] >= in_expert_end.T[e_of], axis=1), nb - 1)
    src_chunk = b_of * cpb + local_first[b_of, e_of] + (off - in_expert_first[b_of, e_of])
    src_rows = jnp.where(valid, src_chunk, 0) * MOE_CHUNK
    n_used = jnp.sum(tiles).astype(jnp.int32)
    tile_expert = e_of[::cpt]
    tile_expert = jnp.where(jnp.arange(n_tiles) < n_used, tile_expert,
                            tile_expert[jnp.maximum(n_used - 1, 0)]).astype(jnp.int32)

    lc = jnp.arange(cpb, dtype=jnp.int32)
    local_end = local_first + nch
    e_loc = jnp.sum(lc[None, :, None] >= local_end[:, None, :], axis=2)
    used_loc = e_loc < N_EXPERTS
    e_loc = jnp.minimum(e_loc, N_EXPERTS - 1)
    take = lambda a: jnp.take_along_axis(a, e_loc, axis=1)
    back_chunk = sorted_first[e_loc] + take(in_expert_first) + (lc[None, :] - take(local_first))
    back_rows = jnp.where(used_loc, back_chunk, 0) * MOE_CHUNK

    ys = _experts(xs, src_rows.reshape(n_tiles, cpt), tile_expert, n_used.reshape(1),
                  w_gu.astype(BF16), w_down.astype(BF16), tm=tm_e, tf=512)
    return _combine(ys, back_rows.astype(jnp.int32), x, meta, gate, seq=seq)


def kernel(x, c, ada_w, ada_b, norm1_g, norm2_g, at_w_in, at_q_norm, at_k_norm, at_lam, at_subln, at_w_out, hg_w_in, hg_lower_bounds, hg_norm_g, hg_w_out, lru_w_in, lru_conv_w, lru_conv_b, lru_w_a, lru_b_a, lru_w_x, lru_b_x, lru_lambda, lru_w_out, ff_w_gu, ff_w_down, moe_router, moe_w_gu, moe_w_down):
    batch, seq, d = x.shape
    depth = ada_w.shape[0]
    t = batch * seq
    mods = _mods(c, ada_w, ada_b)
    xr = x.reshape(t, d)
    for i in range(depth):
        sh1, sc1, g1, sh2, sc2, g2 = [mods[i, :, k * d:(k + 1) * d] for k in range(6)]
        kind, j = i % N_MIXERS, i // N_MIXERS
        if kind == 0:
            lambda_init = 0.8 - 0.6 * math.exp(-0.3 * i)
            xr = _diff_attention_layer(xr, norm1_g[i], sh1, sc1, g1, at_w_in[j], at_q_norm[j], at_k_norm[j],
                                       at_lam[j], at_subln[j], at_w_out[j], lambda_init, batch=batch, seq=seq)
        elif kind == 1:
            xr = _hgrn_layer(xr, norm1_g[i], sh1, sc1, g1, hg_w_in[j], hg_lower_bounds, hg_norm_g[j],
                             hg_w_out[j], i, batch=batch, seq=seq)
        else:
            xr = _rglru_layer(xr, norm1_g[i], sh1, sc1, g1, lru_w_in[j], lru_conv_w[j], lru_conv_b[j],
                              lru_w_a[j], lru_b_a[j], lru_w_x[j], lru_b_x[j], lru_lambda[j], lru_w_out[j],
                              batch=batch, seq=seq)
        m = i // 2
        if i % 2 == 0:
            xr = _dense_ffn(xr, norm2_g[i], sh2, sc2, g2, ff_w_gu[m].astype(BF16), ff_w_down[m].astype(BF16),
                            seq=seq, tm=min(1024, seq), tf=ff_w_down.shape[1] // 2)
        else:
            xr = _moe_ffn(xr, norm2_g[i], sh2, sc2, g2, moe_router[m], moe_w_gu[m], moe_w_down[m], seq=seq)
    return xr.reshape(batch, seq, d)
```

```python
import functools
import math

import numpy as np
import jax
import jax.numpy as jnp
from jax import lax
from jax.experimental import pallas as pl
from jax.experimental.pallas import tpu as pltpu

F32 = jnp.float32
BF16 = jnp.bfloat16
HIGHEST = lax.Precision.HIGHEST

EPS = 1e-6
LANES = 128
VMEM_LIMIT_BYTES = 56 * 2**20

N_MIXERS = 3
DA_HEADS = 8
DA_HEAD_DIM = 64
DA_V_DIM = 2 * DA_HEAD_DIM
ROPE_THETA = 10000.0
HG_HEADS = 8
HG_CHUNK = 64
HG_SUB = 16
LRU_BLOCK_W = 128
CONV_WIDTH = 4
LRU_C = 8.0
N_EXPERTS = 8
TOP_K = 2


def _params(*sem):
    return pltpu.CompilerParams(dimension_semantics=sem, vmem_limit_bytes=VMEM_LIMIT_BYTES)


def _sigmoid(x):
    return 1.0 / (1.0 + jnp.exp(-x))


def _silu(x):
    return x * _sigmoid(x)


def _sigmoid_tanh(x):
    return 0.5 * jnp.tanh(0.5 * x) + 0.5


def _modulated(x, g, shift, scale):
    ms = jnp.mean(x * x, axis=-1, keepdims=True)
    return x * lax.rsqrt(ms + EPS) * g * (1.0 + scale) + shift


def _mods_kernel(c_ref, w_ref, b_ref, o_ref):
    c = c_ref[...]
    o_ref[...] = jnp.dot(_silu(c), w_ref[...], preferred_element_type=F32, precision=HIGHEST) + b_ref[...]


def _mods(c, ada_w, ada_b):
    depth, d, n = ada_w.shape
    b = c.shape[0]
    tn = 2048
    return pl.pallas_call(
        _mods_kernel,
        grid=(depth, n // tn),
        in_specs=[
            pl.BlockSpec((b, d), lambda l, j: (0, 0)),
            pl.BlockSpec((None, d, tn), lambda l, j: (l, 0, j)),
            pl.BlockSpec((None, 1, tn), lambda l, j: (l, 0, j)),
        ],
        out_specs=pl.BlockSpec((None, b, tn), lambda l, j: (l, 0, j)),
        out_shape=jax.ShapeDtypeStruct((depth, b, n), F32),
        compiler_params=_params("parallel", "parallel"),
        name="adaln_mods",
    )(c, ada_w, ada_b.reshape(depth, 1, n))


class _RowVec:
    def __init__(self, vec, tm, seq):
        b, d = vec.shape
        assert seq % tm == 0
        tiles_per_batch = seq // tm
        self.array = vec.reshape(b, 1, d)
        self.spec = pl.BlockSpec((None, 1, d), lambda i, *_: (i // tiles_per_batch, 0, 0))


def _const_spec(shape):
    nd = len(shape)
    return pl.BlockSpec(shape, lambda *_: (0,) * nd)


def _nm_matmul_kernel(x_ref, g_ref, sh_ref, sc_ref, w_ref, o_ref, h_ref):
    @pl.when(pl.program_id(1) == 0)
    def _():
        h_ref[...] = _modulated(x_ref[...], g_ref[...], sh_ref[...], sc_ref[...]).astype(BF16)

    o_ref[...] = jnp.dot(h_ref[...], w_ref[...], preferred_element_type=F32).astype(o_ref.dtype)


def _nm_matmul(x, g, shift, scale, w, *, seq, tm, tn, out_dtype):
    t, d = x.shape
    n = w.shape[1]
    assert t % tm == 0 and n % tn == 0
    sh = _RowVec(shift, tm, seq)
    sc = _RowVec(scale, tm, seq)
    return pl.pallas_call(
        _nm_matmul_kernel,
        grid=(t // tm, n // tn),
        in_specs=[
            pl.BlockSpec((tm, d), lambda i, j: (i, 0)),
            _const_spec((1, d)),
            sh.spec, sc.spec,
            pl.BlockSpec((d, tn), lambda i, j: (0, j)),
        ],
        out_specs=pl.BlockSpec((tm, tn), lambda i, j: (i, j)),
        out_shape=jax.ShapeDtypeStruct((t, n), out_dtype),
        scratch_shapes=[pltpu.VMEM((tm, d), BF16)],
        compiler_params=_params("parallel", "arbitrary"),
        name="modulate_matmul",
    )(x, g.reshape(1, d), sh.array, sc.array, w)


def _out_proj_kernel(*refs, prologue, n_in):
    ins = refs[:n_in]
    w_ref, x_ref, gate_ref, o_ref = refs[n_in:]
    a = prologue(*ins)
    mix = jnp.dot(a, w_ref[...], preferred_element_type=F32)
    o_ref[...] = x_ref[...] + gate_ref[...] * mix


def _out_proj(prologue, ins, in_specs, w, x, gate, *, seq, tm):
    t, d = x.shape
    k = w.shape[0]
    gv = _RowVec(gate, tm, seq)
    return pl.pallas_call(
        functools.partial(_out_proj_kernel, prologue=prologue, n_in=len(ins)),
        grid=(t // tm,),
        in_specs=list(in_specs) + [
            _const_spec((k, d)),
            pl.BlockSpec((tm, d), lambda i: (i, 0)),
            gv.spec,
        ],
        out_specs=pl.BlockSpec((tm, d), lambda i: (i, 0)),
        out_shape=jax.ShapeDtypeStruct((t, d), F32),
        compiler_params=_params("parallel"),
        name="out_proj_residual",
    )(*ins, w, x, gv.array)


QK_PREP_WIDTH = 2 * LANES


def _rope_constants(seq):
    half = DA_HEAD_DIM // 2
    inv = 1.0 / (ROPE_THETA ** (jnp.arange(0, DA_HEAD_DIM, 2, dtype=F32) / DA_HEAD_DIM))
    ang = jnp.arange(seq, dtype=F32)[:, None] * inv[None, :]
    reps = QK_PREP_WIDTH // half
    cos = jnp.tile(jnp.cos(ang), (1, reps))
    sin = jnp.tile(jnp.sin(ang), (1, reps))
    lane = np.arange(QK_PREP_WIDTH)
    first = (lane % DA_HEAD_DIM) < half
    rot = np.zeros((QK_PREP_WIDTH, QK_PREP_WIDTH), np.float32)
    rot[(lane + half)[first], lane[first]] = -1.0
    rot[(lane - half)[~first], lane[~first]] = 1.0
    group = (lane[:, None] // DA_HEAD_DIM == lane[None, :] // DA_HEAD_DIM).astype(np.float32)
    return cos, sin, jnp.asarray(rot, BF16), jnp.asarray(group, BF16)


def _attn_in_kernel(x_ref, g_ref, sh_ref, sc_ref, w_ref, gain_ref, cos_ref, sin_ref, rot_ref, grp_ref,
                    o_ref, h_ref):
    j = pl.program_id(1)

    @pl.when(j == 0)
    def _():
        h_ref[...] = _modulated(x_ref[...], g_ref[...], sh_ref[...], sc_ref[...]).astype(BF16)

    y_all = jnp.dot(h_ref[...], w_ref[...], preferred_element_type=F32)

    @pl.when(j == 2)
    def _():
        o_ref[...] = y_all.astype(BF16)

    @pl.when(j < 2)
    def _():
        out_scale = jnp.where(j == 0, DA_HEAD_DIM ** -0.5 * math.log2(math.e), 1.0).astype(F32)
        cos = cos_ref[...]
        sin = sin_ref[...]
        gain = gain_ref[...]
        for h in range(y_all.shape[1] // QK_PREP_WIDTH):
            sl = slice(h * QK_PREP_WIDTH, (h + 1) * QK_PREP_WIDTH)
            y = y_all[:, sl]
            ss = jnp.dot((y * y).astype(BF16), grp_ref[...], preferred_element_type=F32)
            n = y * lax.rsqrt(ss * (1.0 / DA_HEAD_DIM) + EPS) * gain
            r = jnp.dot(n.astype(BF16), rot_ref[...], preferred_element_type=F32)
            o_ref[:, sl] = ((n * cos + r * sin) * out_scale).astype(BF16)


def _attn_in_proj(x, g, shift, scale, w, q_gain, k_gain, cos, sin, rot, grp, *, seq, tm):
    t, d = x.shape
    width = DA_HEADS * LANES
    assert w.shape[1] == 3 * width
    reps = QK_PREP_WIDTH // DA_HEAD_DIM
    gains = jnp.stack([jnp.tile(q_gain, reps), jnp.tile(k_gain, reps)]).reshape(2, 1, QK_PREP_WIDTH).astype(F32)
    tiles_per_seq = seq // tm
    sh = _RowVec(shift, tm, seq)
    sc = _RowVec(scale, tm, seq)
    return pl.pallas_call(
        _attn_in_kernel,
        grid=(t // tm, 3),
        in_specs=[
            pl.BlockSpec((tm, d), lambda i, j: (i, 0)),
            _const_spec((1, d)),
            sh.spec, sc.spec,
            pl.BlockSpec((d, width), lambda i, j: (0, j)),
            pl.BlockSpec((None, 1, QK_PREP_WIDTH), lambda i, j: (jnp.minimum(j, 1), 0, 0)),
            pl.BlockSpec((tm, QK_PREP_WIDTH), lambda i, j: (i % tiles_per_seq, 0)),
            pl.BlockSpec((tm, QK_PREP_WIDTH), lambda i, j: (i % tiles_per_seq, 0)),
            _const_spec((QK_PREP_WIDTH, QK_PREP_WIDTH)),
            _const_spec((QK_PREP_WIDTH, QK_PREP_WIDTH)),
        ],
        out_specs=pl.BlockSpec((None, tm, width), lambda i, j: (j, i, 0)),
        out_shape=jax.ShapeDtypeStruct((3, t, width), BF16),
        scratch_shapes=[pltpu.VMEM((tm, d), BF16)],
        compiler_params=_params("parallel", "arbitrary"),
        name="attn_in_proj",
    )(x, g.reshape(1, d), sh.array, sc.array, w, gains, cos, sin, rot, grp)


def _flash_kernel(q_ref, k_ref, v_ref, lam_ref, subln_ref, o_ref,
                  vt_ref, kmax_ref, bound_ref, mrun_ref, acc_ref, den_ref, *, tq, tk, lambda_init):
    seq = k_ref.shape[0]
    n_kv = seq // tk
    lane = lax.broadcasted_iota(jnp.int32, (LANES, LANES), 0)
    first_half = jnp.where(lane < DA_HEAD_DIM, 1.0, 0.0).astype(BF16)
    second_half = jnp.where(lane >= DA_HEAD_DIM, 1.0, 0.0).astype(BF16)

    @pl.when(pl.program_id(2) == 0)
    def _():
        def prep(j, carry):
            n1, n2 = carry
            r0 = pl.multiple_of(j * tk, tk)
            vt_ref[:, pl.ds(r0, tk)] = v_ref[pl.ds(r0, tk), :].astype(F32).T.astype(BF16)
            kf = k_ref[pl.ds(r0, tk), :].astype(F32)
            sq = (kf * kf).astype(BF16)
            s1 = jnp.dot(sq, first_half, preferred_element_type=F32)
            s2 = jnp.dot(sq, second_half, preferred_element_type=F32)
            return (jnp.maximum(n1, jnp.max(s1, axis=0, keepdims=True)),
                    jnp.maximum(n2, jnp.max(s2, axis=0, keepdims=True)))

        zero = jnp.zeros((1, LANES), F32)
        n1, n2 = lax.fori_loop(0, n_kv, prep, (zero, zero))
        kmax_ref[0:1, :] = n1
        kmax_ref[1:2, :] = n2

    q = q_ref[...]
    qlane = lax.broadcasted_iota(jnp.int32, q.shape, 1)
    zero = jnp.zeros_like(q)
    qq = jnp.concatenate([jnp.where(qlane < DA_HEAD_DIM, q, zero),
                          jnp.where(qlane >= DA_HEAD_DIM, q, zero)], axis=0)
    qf = qq.astype(F32)
    nt = (((1,), (1,)), ((), ()))
    qn = lax.dot_general(jnp.ones((8, LANES), BF16), (qf * qf).astype(BF16), nt,
                         preferred_element_type=F32)[0:1]
    col = lax.broadcasted_iota(jnp.int32, qn.shape, 1)
    kn = jnp.where(col < tq, kmax_ref[0:1, 0:1], kmax_ref[1:2, 0:1])
    bound = jnp.sqrt(qn * kn) * 1.01
    bound_ref[...] = bound
    acc_ref[...] = jnp.zeros(acc_ref.shape, F32)
    den_ref[...] = jnp.zeros(den_ref.shape, F32)
    safe = jnp.max(bound) <= 60.0

    def scores(j):
        r0 = pl.multiple_of(j * tk, tk)
        st = lax.dot_general(k_ref[pl.ds(r0, tk), :], qq, nt, preferred_element_type=F32)
        return st, vt_ref[:, pl.ds(r0, tk)]

    @pl.when(safe)
    def _():
        def body(j, carry):
            st, vt = scores(j)
            p = jnp.exp2(st - bound_ref[...])
            den_ref[...] += jnp.sum(p, axis=0, keepdims=True)
            acc_ref[...] += jnp.dot(vt, p.astype(BF16), preferred_element_type=F32)
            return carry

        lax.fori_loop(0, n_kv, body, 0, unroll=2)

    @pl.when(jnp.logical_not(safe))
    def _():
        mrun_ref[...] = jnp.full(mrun_ref.shape, -jnp.inf, F32)

        def body(j, carry):
            st, vt = scores(j)
            m_old = mrun_ref[...]
            m_new = jnp.maximum(m_old, jnp.max(st, axis=0, keepdims=True))
            alpha = jnp.exp2(m_old - m_new)
            p = jnp.exp2(st - m_new)
            den_ref[...] = alpha * den_ref[...] + jnp.sum(p, axis=0, keepdims=True)
            acc_ref[...] = alpha * acc_ref[...] + jnp.dot(vt, p.astype(BF16), preferred_element_type=F32)
            mrun_ref[...] = m_new
            return carry

        lax.fori_loop(0, n_kv, body, 0)

    lf = lam_ref[...]
    lam = (jnp.exp(jnp.sum(lf[0:1] * lf[1:2], axis=1, keepdims=True))
           - jnp.exp(jnp.sum(lf[2:3] * lf[3:4], axis=1, keepdims=True)) + lambda_init)
    inv = 1.0 / den_ref[...]
    o1 = acc_ref[:, 0:tq] * inv[:, 0:tq]
    o2 = acc_ref[:, tq:2 * tq] * inv[:, tq:2 * tq]
    o = (o1 - lam * o2).T
    ms = jnp.mean(o * o, axis=-1, keepdims=True)
    o = o * lax.rsqrt(ms + EPS) * subln_ref[...] * (1.0 - lambda_init)
    o_ref[...] = o.astype(BF16)


def _flash(qkv, lam_vecs, subln_g, *, batch, seq, lambda_init, tq, tk):
    t = batch * seq
    width = DA_HEADS * LANES
    q_tiles = seq // tq
    return pl.pallas_call(
        functools.partial(_flash_kernel, tq=tq, tk=tk, lambda_init=lambda_init),
        grid=(batch, DA_HEADS, q_tiles),
        in_specs=[
            pl.BlockSpec((None, tq, LANES), lambda b, h, i: (0, b * q_tiles + i, h)),
            pl.BlockSpec((None, seq, LANES), lambda b, h, i: (1, b, h)),
            pl.BlockSpec((None, seq, LANES), lambda b, h, i: (2, b, h)),
            _const_spec((4, DA_HEAD_DIM)),
            _const_spec((1, DA_V_DIM)),
        ],
        out_specs=pl.BlockSpec((tq, LANES), lambda b, h, i: (b * q_tiles + i, h)),
        out_shape=jax.ShapeDtypeStruct((t, width), BF16),
        scratch_shapes=[
            pltpu.VMEM((LANES, seq), BF16),
            pltpu.VMEM((8, LANES), F32),
            pltpu.VMEM((1, 2 * tq), F32),
            pltpu.VMEM((1, 2 * tq), F32),
            pltpu.VMEM((LANES, 2 * tq), F32),
            pltpu.VMEM((1, 2 * tq), F32),
        ],
        compiler_params=_params("parallel", "parallel", "arbitrary"),
        name="diff_flash_attention",
    )(qkv, qkv, qkv, lam_vecs.astype(F32), subln_g.reshape(1, DA_V_DIM).astype(F32))


def _identity_prologue(o_ref):
    return o_ref[...]


def _diff_attention_layer(x, g, shift, scale, gate, w_in, q_norm, k_norm, lam_vecs, subln_g, w_out,
                          lambda_init, *, batch, seq):
    cos, sin, rot, grp = _rope_constants(seq)
    qkv = _attn_in_proj(x, g, shift, scale, w_in.astype(BF16), q_norm, k_norm, cos, sin, rot, grp,
                        seq=seq, tm=min(1024, seq))
    o = _flash(qkv, lam_vecs, subln_g, batch=batch, seq=seq, lambda_init=lambda_init,
               tq=min(1024, seq), tk=min(512, seq))
    tm = min(512, seq)
    return _out_proj(_identity_prologue, [o], [pl.BlockSpec((tm, o.shape[1]), lambda i: (i, 0))],
                     w_out.astype(BF16), x, gate, seq=seq, tm=tm)


def _hgrn_scan_kernel(q_ref, f_ref, v_ref, lb_ref, o_ref, st_ref, *, reverse, layer):
    c = HG_CHUNK
    n_chunks = q_ref.shape[0] // c

    @pl.when(pl.program_id(1) == 0)
    def _():
        st_ref[...] = jnp.zeros(st_ref.shape, F32)

    lbw = lb_ref[...]
    e = jnp.exp(lbw - jnp.max(lbw, axis=0, keepdims=True))
    lb = jnp.sum(e[1:layer + 1], axis=0, keepdims=True) / jnp.sum(e, axis=0, keepdims=True)

    row = lax.broadcasted_iota(jnp.int32, (c, c), 0)
    col = lax.broadcasted_iota(jnp.int32, (c, c), 1)
    tri = (col >= row) if reverse else (col <= row)
    tri = tri.astype(F32)
    n_sub = c // HG_SUB
    half = HG_SUB // 2
    col8 = lax.broadcasted_iota(jnp.int32, (half, c), 1)
    row8 = lax.broadcasted_iota(jnp.int32, (half, c), 0)

    def key_rows(i, r):
        t0 = i * HG_SUB + r * half
        return range(t0, (i + 1) * HG_SUB) if reverse else range(i * HG_SUB, t0 + half)

    def placement(i, r, s):
        t0 = i * HG_SUB + r * half
        reached = (row8 + t0 <= s) if reverse else (row8 + t0 >= s)
        return jnp.logical_and(col8 == s, reached)

    place = {(i, r, s): placement(i, r, s) for i in range(n_sub) for r in range(2) for s in key_rows(i, r)}

    def chunk(ci, carry):
        cidx = (n_chunks - 1 - ci) if reverse else ci
        r0 = pl.multiple_of(cidx * c, c)
        qs = _silu(q_ref[pl.ds(r0, c), :])
        forget = lb + (1.0 - lb) * _sigmoid(f_ref[pl.ds(r0, c), :])
        kk = 1.0 - forget
        vb = v_ref[pl.ds(r0, c), :].astype(BF16)
        b = jnp.dot(tri, jnp.log2(forget), preferred_element_type=F32, precision=HIGHEST)
        b_edge = b[0:1] if reverse else b[c - 1:c]
        q_in = (qs * jnp.exp2(b)).astype(BF16)
        k_out = (kk * jnp.exp2(b_edge - b)).astype(BF16)
        dec = jnp.exp2(b_edge)
        key_shift = b - jnp.log2(kk)

        pieces = [[jnp.zeros((half, c), F32) for _ in range(2 * n_sub)] for _ in range(HG_HEADS)]
        for i in range(n_sub):
            for r in range(2):
                t0 = i * HG_SUB + r * half
                q8 = qs[t0:t0 + half]
                b8 = b[t0:t0 + half]
                for s in key_rows(i, r):
                    term = q8 * jnp.exp2(b8 - key_shift[s:s + 1])
                    for h in range(HG_HEADS):
                        a_col = jnp.sum(term[:, h * LANES:(h + 1) * LANES], axis=1, keepdims=True)
                        pieces[h][2 * i + r] = jnp.where(place[i, r, s], a_col, pieces[h][2 * i + r])

        for h in range(HG_HEADS):
            sl = slice(h * LANES, (h + 1) * LANES)
            blocks = []
            for i in range(n_sub):
                rows = slice(i * HG_SUB, (i + 1) * HG_SUB)
                passed = slice((i + 1) * HG_SUB, c) if reverse else slice(0, i * HG_SUB)
                n_passed = passed.stop - passed.start
                if n_passed == 0:
                    blocks.append(jnp.zeros((HG_SUB, c), F32))
                    continue
                ref_row = (i + 1) * HG_SUB if reverse else i * HG_SUB - 1
                b_ref_row = b[ref_row:ref_row + 1, sl]
                q_t = (qs[rows, sl] * jnp.exp2(b[rows, sl] - b_ref_row)).astype(BF16)
                k_p = (kk[passed, sl] * jnp.exp2(b_ref_row - b[passed, sl])).astype(BF16)
                rest = jnp.zeros((c - n_passed, LANES), BF16)
                k_t = jnp.concatenate([rest, k_p] if reverse else [k_p, rest], axis=0)
                blocks.append(lax.dot_general(q_t, k_t, (((1,), (1,)), ((), ())),
                                              preferred_element_type=F32))
            a = jnp.concatenate(pieces[h], axis=0) + jnp.concatenate(blocks, axis=0)
            st = st_ref[h]
            o_h = jnp.dot(a.astype(BF16), vb[:, sl], preferred_element_type=F32)
            o_h = o_h + lax.dot_general(q_in[:, sl], st.astype(BF16), (((1,), (1,)), ((), ())),
                                        preferred_element_type=F32)
            o_ref[pl.ds(r0, c), sl] = o_h.astype(o_ref.dtype)
            upd = lax.dot_general(vb[:, sl], k_out[:, sl], (((0,), (0,)), ((), ())),
                                  preferred_element_type=F32)
            st_ref[h] = st * dec[:, sl] + upd
        return carry

    lax.fori_loop(0, n_chunks, chunk, 0, unroll=2)


def _hgrn_scan(proj, lower_bounds, *, batch, seq, layer, reverse, tt):
    t = batch * seq
    width = HG_HEADS * LANES
    nt = seq // tt
    depth = lower_bounds.shape[1]
    d_idx = 1 if reverse else 0

    def rows(b, i):
        return b * nt + ((nt - 1 - i) if reverse else i)

    return pl.pallas_call(
        functools.partial(_hgrn_scan_kernel, reverse=reverse, layer=layer),
        grid=(batch, nt),
        in_specs=[
            pl.BlockSpec((tt, width), lambda b, i: (rows(b, i), 0)),
            pl.BlockSpec((tt, width), lambda b, i: (rows(b, i), 1 + d_idx)),
            pl.BlockSpec((tt, width), lambda b, i: (rows(b, i), 3)),
            pl.BlockSpec((None, depth, width), lambda b, i: (d_idx, 0, 0)),
        ],
        out_specs=pl.BlockSpec((tt, width), lambda b, i: (rows(b, i), 0)),
        out_shape=jax.ShapeDtypeStruct((t, width), BF16),
        scratch_shapes=[pltpu.VMEM((HG_HEADS, LANES, LANES), F32)],
        compiler_params=_params("parallel", "arbitrary"),
        name="hgrn2_scan_bwd" if reverse else "hgrn2_scan_fwd",
    )(proj, proj, proj, lower_bounds.astype(F32))


def _hgrn_prologue(of_ref, ob_ref, g_ref, ng_ref):
    ng = ng_ref[...]
    outs = []
    for h in range(HG_HEADS):
        sl = slice(h * LANES, (h + 1) * LANES)
        o = of_ref[:, sl].astype(F32) + ob_ref[:, sl].astype(F32)
        ms = jnp.mean(o * o, axis=-1, keepdims=True)
        outs.append((o * lax.rsqrt(ms + EPS) * ng * _silu(g_ref[:, sl])).astype(BF16))
    return jnp.concatenate(outs, axis=1)


def _hgrn_layer(x, g, shift, scale, gate, w_in, lower_bounds, norm_g, w_out, layer, *, batch, seq):
    width = HG_HEADS * LANES
    proj = _nm_matmul(x, g, shift, scale, w_in.astype(BF16), seq=seq,
                      tm=min(1024, seq), tn=1024, out_dtype=F32)
    tt = min(512, seq)
    o_f = _hgrn_scan(proj, lower_bounds, batch=batch, seq=seq, layer=layer, reverse=False, tt=tt)
    o_b = _hgrn_scan(proj, lower_bounds, batch=batch, seq=seq, layer=layer, reverse=True, tt=tt)
    tm = min(512, seq)
    ins = [o_f, o_b, proj, norm_g.reshape(1, LANES).astype(F32)]
    specs = [
        pl.BlockSpec((tm, width), lambda i: (i, 0)),
        pl.BlockSpec((tm, width), lambda i: (i, 0)),
        pl.BlockSpec((tm, width), lambda i: (i, 4)),
        _const_spec((1, LANES)),
    ]
    return _out_proj(_hgrn_prologue, ins, specs, w_out.astype(BF16), x, gate, seq=seq, tm=tm)


def _lru_scan_kernel(prev_ref, main_ref, next_ref, cw_ref, cb_ref, wg_ref, ba_ref, bx_ref, lam_ref,
                     o_ref, a_ref, u_ref, h_ref, *, batch, nt):
    d = pl.program_id(0)
    i = pl.program_id(1)
    ti = i + d * (nt - 1 - 2 * i)
    rows, width = main_ref.shape
    tt = rows // batch

    @pl.when(i == 0)
    def _():
        h_ref[...] = jnp.zeros(h_ref.shape, F32)

    prev = jnp.where(ti == 0, 0.0, prev_ref[...])
    nxt = jnp.where(ti == nt - 1, 0.0, next_ref[...])
    ext = jnp.concatenate([prev, main_ref[...], nxt], axis=0)
    cw = cw_ref[...]
    xc = cb_ref[...] + cw[0:1] * ext[0:rows]
    for j in range(1, CONV_WIDTH):
        xc = xc + cw[j:j + 1] * ext[j * batch:j * batch + rows]

    neg_softplus = -LRU_C * (jnp.maximum(-lam_ref[...], 0.0) + jnp.log1p(jnp.exp(-jnp.abs(lam_ref[...]))))
    for n in range(width // LRU_BLOCK_W):
        sl = slice(n * LRU_BLOCK_W, (n + 1) * LRU_BLOCK_W)
        xb = xc[:, sl]
        gates = jnp.dot(xb.astype(BF16), wg_ref[n], preferred_element_type=F32)
        r = _sigmoid_tanh(gates[:, :LRU_BLOCK_W] + ba_ref[:, sl])
        ig = _sigmoid_tanh(gates[:, LRU_BLOCK_W:] + bx_ref[:, sl])
        a = jnp.exp(r * neg_softplus[:, sl])
        a_ref[:, sl] = a
        t = 1.0 - a * a
        u_ref[:, sl] = (t * lax.rsqrt(jnp.maximum(t, 1e-30))) * (ig * xb)

    def step(t, h):
        tl = t + d * (tt - 1 - 2 * t)
        r0 = pl.multiple_of(tl * batch, batch)
        h = a_ref[pl.ds(r0, batch), :] * h + u_ref[pl.ds(r0, batch), :]
        o_ref[pl.ds(r0, batch), :] = h
        return h

    h_ref[...] = lax.fori_loop(0, tt, step, h_ref[...])


def _lru_scan(proj, conv_w, conv_b, w_a, b_a, w_x, b_x, lam, *, batch, seq, tt):
    t = batch * seq
    width = conv_w.shape[1]
    n_blocks = width // LRU_BLOCK_W
    nt = seq // tt
    rows = tt * batch
    assert tt % 2 == 0

    def tile(d, i):
        return i + d * (nt - 1 - 2 * i)

    wg = jnp.concatenate([w_a, w_x], axis=-1).astype(BF16)
    vec = lambda a: a.reshape(2, 1, width).astype(F32)
    return pl.pallas_call(
        functools.partial(_lru_scan_kernel, batch=batch, nt=nt),
        grid=(2, nt),
        in_specs=[
            pl.BlockSpec((batch, width), lambda d, i: (jnp.maximum(tile(d, i) * tt - 1, 0), 1)),
            pl.BlockSpec((rows, width), lambda d, i: (tile(d, i), 1)),
            pl.BlockSpec((2 * batch, width),
                         lambda d, i: (jnp.minimum((tile(d, i) + 1) * (tt // 2), seq // 2 - 1), 1)),
            _const_spec((CONV_WIDTH, width)),
            _const_spec((1, width)),
            pl.BlockSpec((None, n_blocks, LRU_BLOCK_W, 2 * LRU_BLOCK_W), lambda d, i: (d, 0, 0, 0)),
            pl.BlockSpec((None, 1, width), lambda d, i: (d, 0, 0)),
            pl.BlockSpec((None, 1, width), lambda d, i: (d, 0, 0)),
            pl.BlockSpec((None, 1, width), lambda d, i: (d, 0, 0)),
        ],
        out_specs=pl.BlockSpec((None, rows, width), lambda d, i: (d, tile(d, i), 0)),
        out_shape=jax.ShapeDtypeStruct((2, t, width), F32),
        scratch_shapes=[
            pltpu.VMEM((rows, width), F32),
            pltpu.VMEM((rows, width), F32),
            pltpu.VMEM((batch, width), F32),
        ],
        compiler_params=_params("arbitrary", "arbitrary"),
        name="rglru_scan",
    )(proj, proj, proj, conv_w.astype(F32), conv_b.reshape(1, width).astype(F32), wg,
      vec(b_a), vec(b_x), vec(lam))


def _gelu_tanh(x):
    return 0.5 * x * (1.0 + jnp.tanh(math.sqrt(2.0 / math.pi) * (x + 0.044715 * (x * x * x))))


def _lru_in_kernel(x_ref, g_ref, sh_ref, sc_ref, w_ref, o_ref, xm_ref, h_ref, *, batch, ts):
    n_lane_blocks = x_ref.shape[2] // LANES
    for b in range(batch):
        xm = _modulated(x_ref[b], g_ref[...], sh_ref[b:b + 1, :], sc_ref[b:b + 1, :])
        for c in range(n_lane_blocks):
            xm_ref[c, b * ts:(b + 1) * ts, :] = xm[:, c * LANES:(c + 1) * LANES]
    for s in range(ts):
        for c in range(n_lane_blocks):
            h_ref[s * batch:(s + 1) * batch, c * LANES:(c + 1) * LANES] = \
                xm_ref[c, pl.ds(s, batch, stride=ts), :]
    o_ref[...] = jnp.dot(h_ref[...].astype(BF16), w_ref[...], preferred_element_type=F32)


def _lru_in_proj(x, g, shift, scale, w, *, batch, seq, ts):
    t, d = x.shape
    n = w.shape[1]
    rows = ts * batch
    return pl.pallas_call(
        functools.partial(_lru_in_kernel, batch=batch, ts=ts),
        grid=(seq // ts,),
        in_specs=[
            pl.BlockSpec((batch, ts, d), lambda i: (0, i, 0)),
            _const_spec((1, d)),
            _const_spec((batch, d)),
            _const_spec((batch, d)),
            _const_spec((d, n)),
        ],
        out_specs=pl.BlockSpec((rows, n), lambda i: (i, 0)),
        out_shape=jax.ShapeDtypeStruct((t, n), F32),
        scratch_shapes=[pltpu.VMEM((d // LANES, rows, LANES), F32), pltpu.VMEM((rows, d), F32)],
        compiler_params=_params("parallel"),
        name="rglru_in_proj",
    )(x.reshape(batch, seq, d), g.reshape(1, d), shift, scale, w)


def _lru_out_kernel(hf_ref, hb_ref, y_ref, w_ref, x_ref, gate_ref, o_ref, mix_ref, *, batch, ts):
    a = ((hf_ref[...] + hb_ref[...]) * _gelu_tanh(y_ref[...])).astype(BF16)
    mix = jnp.dot(a, w_ref[...], preferred_element_type=F32)
    n_lane_blocks = mix.shape[1] // LANES
    for c in range(n_lane_blocks):
        mix_ref[c] = mix[:, c * LANES:(c + 1) * LANES]
    for b in range(batch):
        for c in range(n_lane_blocks):
            sl = slice(c * LANES, (c + 1) * LANES)
            o_ref[b, :, sl] = x_ref[b, :, sl] + gate_ref[b:b + 1, sl] * mix_ref[c, pl.ds(b, ts, stride=batch), :]


def _lru_out_proj(h2, proj, w, x, gate, *, batch, seq, ts):
    t, d = x.shape
    width = w.shape[0]
    rows = ts * batch
    out = pl.pallas_call(
        functools.partial(_lru_out_kernel, batch=batch, ts=ts),
        grid=(seq // ts,),
        in_specs=[
            pl.BlockSpec((None, rows, width), lambda i: (0, i, 0)),
            pl.BlockSpec((None, rows, width), lambda i: (1, i, 0)),
            pl.BlockSpec((rows, width), lambda i: (i, 0)),
            _const_spec((width, d)),
            pl.BlockSpec((batch, ts, d), lambda i: (0, i, 0)),
            _const_spec((batch, d)),
        ],
        out_specs=pl.BlockSpec((batch, ts, d), lambda i: (0, i, 0)),
        out_shape=jax.ShapeDtypeStruct((batch, seq, d), F32),
        scratch_shapes=[pltpu.VMEM((d // LANES, rows, LANES), F32)],
        compiler_params=_params("parallel"),
        name="rglru_out_proj",
    )(h2, h2, proj, w, x.reshape(batch, seq, d), gate)
    return out.reshape(t, d)


def _rglru_layer(x, g, shift, scale, gate, w_in, conv_w, conv_b, w_a, b_a, w_x, b_x, lam, w_out,
                 *, batch, seq):
    width = conv_w.shape[1]
    ts = min(64, seq)
    proj = _lru_in_proj(x, g, shift, scale, w_in.astype(BF16), batch=batch, seq=seq, ts=ts)
    h2 = _lru_scan(proj, conv_w, conv_b, w_a, b_a, w_x, b_x, lam, batch=batch, seq=seq, tt=min(64, seq))
    return _lru_out_proj(h2, proj, w_out.astype(BF16), x, gate, batch=batch, seq=seq, ts=ts)


def _ffn_kernel(x_ref, g_ref, sh_ref, sc_ref, wg_ref, wu_ref, wd_ref, gate_ref, o_ref, h_ref, acc_ref):
    j = pl.program_id(1)

    @pl.when(j == 0)
    def _():
        h_ref[...] = _modulated(x_ref[...], g_ref[...], sh_ref[...], sc_ref[...]).astype(BF16)
        acc_ref[...] = jnp.zeros(acc_ref.shape, F32)

    h = h_ref[...]
    gt = jnp.dot(h, wg_ref[...], preferred_element_type=F32)
    up = jnp.dot(h, wu_ref[...], preferred_element_type=F32)
    act = (_silu(gt) * up).astype(BF16)
    acc_ref[...] += jnp.dot(act, wd_ref[...], preferred_element_type=F32)

    @pl.when(j == pl.num_programs(1) - 1)
    def _():
        o_ref[...] = x_ref[...] + gate_ref[...] * acc_ref[...]


def _dense_ffn(x, g, shift, scale, gate, w_gu, w_down, *, seq, tm, tf):
    t, d = x.shape
    f = w_down.shape[0]
    assert f % tf == 0 and seq % tm == 0
    nf = f // tf
    sh = _RowVec(shift, tm, seq)
    sc = _RowVec(scale, tm, seq)
    gv = _RowVec(gate, tm, seq)
    weight_buffers = pl.Buffered(1) if nf == 1 else pl.Buffered(2)
    return pl.pallas_call(
        _ffn_kernel,
        grid=(t // tm, nf),
        in_specs=[
            pl.BlockSpec((tm, d), lambda i, j: (i, 0)),
            _const_spec((1, d)),
            sh.spec, sc.spec,
            pl.BlockSpec((d, tf), lambda i, j: (0, j), pipeline_mode=weight_buffers),
            pl.BlockSpec((d, tf), lambda i, j: (0, nf + j), pipeline_mode=weight_buffers),
            pl.BlockSpec((tf, d), lambda i, j: (j, 0), pipeline_mode=weight_buffers),
            gv.spec,
        ],
        out_specs=pl.BlockSpec((tm, d), lambda i, j: (i, 0)),
        out_shape=jax.ShapeDtypeStruct((t, d), F32),
        scratch_shapes=[pltpu.VMEM((tm, d), BF16), pltpu.VMEM((tm, d), F32)],
        compiler_params=_params("parallel", "arbitrary"),
        name="dense_swiglu",
    )(x, g.reshape(1, d), sh.array, sc.array, w_gu, w_gu, w_down, gv.array)


MOE_BLOCK = 512
MOE_CHUNK = 16
MOE_TILE = 1024


def _route_kernel(x_ref, g_ref, sh_ref, sc_ref, rhi_ref, rlo_ref, xs_ref, meta_ref, cnt_ref):
    tb = x_ref.shape[0]
    local_rows = xs_ref.shape[0]
    hf = _modulated(x_ref[...], g_ref[...], sh_ref[...], sc_ref[...])
    h_hi = hf.astype(BF16)
    h_lo = (hf - h_hi.astype(F32)).astype(BF16)
    logits = (jnp.dot(h_hi, rhi_ref[...], preferred_element_type=F32)
              + jnp.dot(h_lo, rhi_ref[...], preferred_element_type=F32)
              + jnp.dot(h_hi, rlo_ref[...], preferred_element_type=F32))
    lane = lax.broadcasted_iota(jnp.int32, logits.shape, 1).astype(F32)
    logits = jnp.where(lane < N_EXPERTS, logits, -jnp.inf)
    m1 = jnp.max(logits, axis=1, keepdims=True)
    e1 = jnp.min(jnp.where(logits == m1, lane, float(LANES)), axis=1, keepdims=True)
    rest = jnp.where(lane == e1, -jnp.inf, logits)
    m2 = jnp.max(rest, axis=1, keepdims=True)
    e2 = jnp.min(jnp.where(rest == m2, lane, float(LANES)), axis=1, keepdims=True)
    gate2 = 1.0 / (1.0 + jnp.exp(m1 - m2))
    gate1 = 1.0 - gate2
    hot1 = (lane == e1).astype(F32)
    hot2 = (lane == e2).astype(F32)
    sel = hot1 + hot2
    row = lax.broadcasted_iota(jnp.int32, (tb, tb), 0)
    col = lax.broadcasted_iota(jnp.int32, (tb, tb), 1)
    before = (col < row).astype(BF16)
    seen = jnp.dot(before, sel.astype(BF16), preferred_element_type=F32)
    chunks = jnp.floor((jnp.sum(sel, axis=0, keepdims=True) + (MOE_CHUNK - 1.0)) * (1.0 / MOE_CHUNK))
    cnt_ref[...] = chunks
    lrow = lax.broadcasted_iota(jnp.int32, (LANES, LANES), 0)
    lcol = lax.broadcasted_iota(jnp.int32, (LANES, LANES), 1)
    lower_experts = (lrow < lcol).astype(BF16)
    first_chunk = jnp.dot(jnp.broadcast_to(chunks, (8, LANES)).astype(BF16), lower_experts,
                          preferred_element_type=F32)[0:1]
    base = first_chunk * float(MOE_CHUNK) + seen
    ld1 = jnp.sum(hot1 * base, axis=1, keepdims=True)
    ld2 = jnp.sum(hot2 * base, axis=1, keepdims=True)
    meta = jnp.zeros(logits.shape, F32)
    for k, val in enumerate((e1, e2, gate1, gate2, ld1, ld2)):
        meta = jnp.where(lane == float(k), val, meta)
    meta_ref[...] = meta
    eye = row == col
    ld1_row = jnp.sum(jnp.where(eye, ld1, 0.0), axis=0, keepdims=True)
    ld2_row = jnp.sum(jnp.where(eye, ld2, 0.0), axis=0, keepdims=True)
    r_iota = lax.broadcasted_iota(jnp.int32, (local_rows, tb), 0).astype(F32)
    pick = jnp.where(r_iota == ld1_row, 1.0, jnp.where(r_iota == ld2_row, 1.0, 0.0)).astype(BF16)
    xs_ref[...] = jnp.dot(pick, h_hi, preferred_element_type=F32).astype(BF16)


def _route(x, g, shift, scale, router, *, seq):
    t, d = x.shape
    tb = min(MOE_BLOCK, seq)
    local_rows = TOP_K * tb + N_EXPERTS * MOE_CHUNK
    nb = t // tb
    sh = _RowVec(shift, tb, seq)
    sc = _RowVec(scale, tb, seq)
    router_pad = jnp.zeros((d, LANES), F32).at[:, :N_EXPERTS].set(router.astype(F32))
    router_hi = router_pad.astype(BF16)
    router_lo = (router_pad - router_hi.astype(F32)).astype(BF16)
    return pl.pallas_call(
        _route_kernel,
        grid=(nb,),
        in_specs=[
            pl.BlockSpec((tb, d), lambda i: (i, 0)),
            _const_spec((1, d)),
            sh.spec, sc.spec,
            _const_spec((d, LANES)),
            _const_spec((d, LANES)),
        ],
        out_specs=[
            pl.BlockSpec((local_rows, d), lambda i: (i, 0)),
            pl.BlockSpec((tb, LANES), lambda i: (i, 0)),
            pl.BlockSpec((None, 1, LANES), lambda i: (i, 0, 0)),
        ],
        out_shape=[
            jax.ShapeDtypeStruct((nb * local_rows, d), BF16),
            jax.ShapeDtypeStruct((t, LANES), F32),
            jax.ShapeDtypeStruct((nb, 1, LANES), F32),
        ],
        compiler_params=_params("parallel"),
        name="moe_route",
    )(x, g.reshape(1, d), sh.array, sc.array, router_hi, router_lo)


def _start_chunk_gather(row_smem, slot, src_hbm, dst_vmem, sem):
    n = dst_vmem.shape[1] // MOE_CHUNK

    def start(k, c):
        r0 = pl.multiple_of(row_smem[slot, k], MOE_CHUNK)
        pltpu.make_async_copy(src_hbm.at[pl.ds(r0, MOE_CHUNK)],
                              dst_vmem.at[slot, pl.ds(pl.multiple_of(k * MOE_CHUNK, MOE_CHUNK), MOE_CHUNK)],
                              sem.at[slot]).start()
        return c

    lax.fori_loop(0, n, start, 0, unroll=8)


def _wait_chunk_gather(slot, src_hbm, dst_vmem, sem):
    n = dst_vmem.shape[1]
    pltpu.make_async_copy(src_hbm.at[pl.ds(0, n)], dst_vmem.at[slot], sem.at[slot]).wait()


def _expert_kernel(te_ref, nu_ref, src_ref, xs_ref, wg_ref, wu_ref, wd_ref, o_ref,
                   idx_ref, xg_ref, acc_ref, sem_idx, sem_rows):
    i = pl.program_id(0)
    j = pl.program_id(1)
    n_used = nu_ref[0]
    used = i < n_used
    slot = i % 2

    def idx_copy(tile, into):
        return pltpu.make_async_copy(src_ref.at[tile], idx_ref.at[into], sem_idx)

    @pl.when(jnp.logical_and(used, j == 0))
    def _():
        @pl.when(i == 0)
        def _():
            idx_copy(0, 0).start()
            idx_copy(0, 0).wait()
            _start_chunk_gather(idx_ref, 0, xs_ref, xg_ref, sem_rows)

        has_next = i + 1 < n_used

        @pl.when(has_next)
        def _():
            idx_copy(i + 1, 1 - slot).start()

        _wait_chunk_gather(slot, xs_ref, xg_ref, sem_rows)
        acc_ref[...] = jnp.zeros(acc_ref.shape, F32)

        @pl.when(has_next)
        def _():
            idx_copy(i + 1, 1 - slot).wait()
            _start_chunk_gather(idx_ref, 1 - slot, xs_ref, xg_ref, sem_rows)

    @pl.when(used)
    def _():
        h = xg_ref[slot]
        gt = jnp.dot(h, wg_ref[...], preferred_element_type=F32)
        up = jnp.dot(h, wu_ref[...], preferred_element_type=F32)
        act = (_silu(gt) * up).astype(BF16)
        acc_ref[...] += jnp.dot(act, wd_ref[...], preferred_element_type=F32)

    @pl.when(j == pl.num_programs(1) - 1)
    def _():
        o_ref[...] = jnp.where(used, acc_ref[...], 0.0).astype(o_ref.dtype)


def _experts(xs, src_tiles, tile_expert, n_used, w_gu, w_down, *, tm, tf):
    d = xs.shape[1]
    n_tiles, chunks_per_tile = src_tiles.shape
    assert chunks_per_tile * MOE_CHUNK == tm
    f = w_down.shape[1]
    nf = f // tf

    def jj(i, j, nu):
        return jnp.where(i < nu[0], j, nf - 1)

    grid_spec = pltpu.PrefetchScalarGridSpec(
        num_scalar_prefetch=2,
        grid=(n_tiles, nf),
        in_specs=[
            pl.BlockSpec(memory_space=pl.ANY),
            pl.BlockSpec(memory_space=pl.ANY),
            pl.BlockSpec((None, d, tf), lambda i, j, te, nu: (te[i], 0, jj(i, j, nu))),
            pl.BlockSpec((None, d, tf), lambda i, j, te, nu: (te[i], 0, nf + jj(i, j, nu))),
            pl.BlockSpec((None, tf, d), lambda i, j, te, nu: (te[i], jj(i, j, nu), 0)),
        ],
        out_specs=pl.BlockSpec((tm, d), lambda i, j, te, nu: (i, 0)),
        scratch_shapes=[
            pltpu.SMEM((2, chunks_per_tile), jnp.int32),
            pltpu.VMEM((2, tm, d), BF16),
            pltpu.VMEM((tm, d), F32),
            pltpu.SemaphoreType.DMA,
            pltpu.SemaphoreType.DMA((2,)),
        ],
    )
    return pl.pallas_call(
        _expert_kernel,
        grid_spec=grid_spec,
        out_shape=jax.ShapeDtypeStruct((n_tiles * tm, d), BF16),
        compiler_params=_params("arbitrary", "arbitrary"),
        name="moe_experts",
    )(tile_expert, n_used, src_tiles, xs, w_gu, w_gu, w_down)


def _combine_kernel(src_ref, ys_ref, x_ref, meta_ref, gate_ref, o_ref, idx_ref, yl_ref, sem_idx, sem_rows):
    i = pl.program_id(0)
    slot = i % 2
    has_next = i + 1 < pl.num_programs(0)

    def idx_copy(block, into):
        return pltpu.make_async_copy(src_ref.at[block], idx_ref.at[into], sem_idx)

    @pl.when(i == 0)
    def _():
        idx_copy(0, 0).start()
        idx_copy(0, 0).wait()
        _start_chunk_gather(idx_ref, 0, ys_ref, yl_ref, sem_rows)

    @pl.when(has_next)
    def _():
        idx_copy(i + 1, 1 - slot).start()

    _wait_chunk_gather(slot, ys_ref, yl_ref, sem_rows)

    @pl.when(has_next)
    def _():
        idx_copy(i + 1, 1 - slot).wait()
        _start_chunk_gather(idx_ref, 1 - slot, ys_ref, yl_ref, sem_rows)

    meta = meta_ref[...]
    yl = yl_ref[slot]
    lane = lax.broadcasted_iota(jnp.int32, (meta.shape[0], yl.shape[0]), 1).astype(F32)
    y1 = jnp.dot(jnp.where(lane == meta[:, 4:5], 1.0, 0.0).astype(BF16), yl, preferred_element_type=F32)
    y2 = jnp.dot(jnp.where(lane == meta[:, 5:6], 1.0, 0.0).astype(BF16), yl, preferred_element_type=F32)
    o_ref[...] = x_ref[...] + gate_ref[...] * (meta[:, 2:3] * y1 + meta[:, 3:4] * y2)


def _combine(ys, src_blocks, x, meta, gate, *, seq):
    t, d = x.shape
    nb, chunks_per_block = src_blocks.shape
    tb = t // nb
    local_rows = chunks_per_block * MOE_CHUNK
    gv = _RowVec(gate, tb, seq)
    return pl.pallas_call(
        _combine_kernel,
        grid=(nb,),
        in_specs=[
            pl.BlockSpec(memory_space=pl.ANY),
            pl.BlockSpec(memory_space=pl.ANY),
            pl.BlockSpec((tb, d), lambda i: (i, 0)),
            pl.BlockSpec((tb, LANES), lambda i: (i, 0)),
            gv.spec,
        ],
        out_specs=pl.BlockSpec((tb, d), lambda i: (i, 0)),
        out_shape=jax.ShapeDtypeStruct((t, d), F32),
        scratch_shapes=[
            pltpu.SMEM((2, chunks_per_block), jnp.int32),
            pltpu.VMEM((2, local_rows, d), BF16),
            pltpu.SemaphoreType.DMA,
            pltpu.SemaphoreType.DMA((2,)),
        ],
        compiler_params=_params("arbitrary"),
        name="moe_combine",
    )(src_blocks, ys, x, meta, gv.array)


def _moe_ffn(x, g, shift, scale, gate, router, w_gu, w_down, *, seq):
    t, d = x.shape
    tb = min(MOE_BLOCK, seq)
    tm_e = min(MOE_TILE, seq)
    xs, meta, chunks = _route(x, g, shift, scale, router, seq=seq)
    nb = t // tb
    cpb = (TOP_K * tb + N_EXPERTS * MOE_CHUNK) // MOE_CHUNK
    cpt = tm_e // MOE_CHUNK

    nch = chunks[:, 0, :N_EXPERTS].astype(jnp.int32)
    local_first = jnp.cumsum(nch, axis=1) - nch
    per_expert = jnp.sum(nch, axis=0)
    tiles = (per_expert + cpt - 1) // cpt
    sorted_end = jnp.cumsum(tiles) * cpt
    sorted_first = sorted_end - tiles * cpt
    in_expert_end = jnp.cumsum(nch, axis=0)
    in_expert_first = in_expert_end - nch

    n_tiles = -(-(TOP_K * t // MOE_CHUNK + nb * N_EXPERTS) // cpt) + N_EXPERTS
    slot = jnp.arange(n_tiles * cpt, dtype=jnp.int32)
    e_of = jnp.minimum(jnp.sum(slot[:, None] >= sorted_end[None, :], axis=1), N_EXPERTS - 1)
    is_e = e_of[:, None] == jnp.arange(N_EXPERTS)[None, :]
    by_expert = lambda v: jnp.sum(jnp.where(is_e, v[None, :], 0), axis=1)
    off = slot - by_expert(sorted_first)
    valid = off < by_expert(per_expert)
    end_of_blocks = jnp.sum(jnp.where(is_e[:, :, None], in_expert_end.T[None], 0), axis=1)
    b_of = jnp.minimum(jnp.sum(off[:, None] >= end_of_blocks, axis=1), nb - 1)
    is_b = b_of[:, None] == jnp.arange(nb)[None, :]
    shift_tab = (local_first - in_expert_first).T
    shift = jnp.sum(jnp.where(is_b, jnp.sum(jnp.where(is_e[:, :, None], shift_tab[None], 0), axis=1), 0), axis=1)
    src_rows = jnp.where(valid, b_of * cpb + shift + off, 0) * MOE_CHUNK
    n_used = jnp.sum(tiles).astype(jnp.int32)
    tile_id = jnp.arange(n_tiles, dtype=jnp.int32)
    frozen = jnp.minimum(tile_id, jnp.maximum(n_used - 1, 0)) * cpt
    tile_expert = jnp.minimum(jnp.sum(frozen[:, None] >= sorted_end[None, :], axis=1), N_EXPERTS - 1)
    tile_expert = tile_expert.astype(jnp.int32)

    lc = jnp.arange(cpb, dtype=jnp.int32)
    local_end = local_first + nch
    e_loc = jnp.sum(lc[None, :, None] >= local_end[:, None, :], axis=2)
    used_loc = e_loc < N_EXPERTS
    is_e_loc = e_loc[:, :, None] == jnp.arange(N_EXPERTS)[None, None, :]
    back_tab = sorted_first[None, :] + in_expert_first - local_first
    back_chunk = jnp.sum(jnp.where(is_e_loc, back_tab[:, None, :], 0), axis=2) + lc[None, :]
    back_rows = jnp.where(used_loc, back_chunk, 0) * MOE_CHUNK

    ys = _experts(xs, src_rows.reshape(n_tiles, cpt), tile_expert, n_used.reshape(1),
                  w_gu.astype(BF16), w_down.astype(BF16), tm=tm_e, tf=512)
    return _combine(ys, back_rows.astype(jnp.int32), x, meta, gate, seq=seq)


def kernel(x, c, ada_w, ada_b, norm1_g, norm2_g, at_w_in, at_q_norm, at_k_norm, at_lam, at_subln, at_w_out, hg_w_in, hg_lower_bounds, hg_norm_g, hg_w_out, lru_w_in, lru_conv_w, lru_conv_b, lru_w_a, lru_b_a, lru_w_x, lru_b_x, lru_lambda, lru_w_out, ff_w_gu, ff_w_down, moe_router, moe_w_gu, moe_w_down):
    batch, seq, d = x.shape
    depth = ada_w.shape[0]
    t = batch * seq
    mods = _mods(c, ada_w, ada_b)
    xr = x.reshape(t, d)
    for i in range(depth):
        sh1, sc1, g1, sh2, sc2, g2 = [mods[i, :, k * d:(k + 1) * d] for k in range(6)]
        kind, j = i % N_MIXERS, i // N_MIXERS
        if kind == 0:
            lambda_init = 0.8 - 0.6 * math.exp(-0.3 * i)
            xr = _diff_attention_layer(xr, norm1_g[i], sh1, sc1, g1, at_w_in[j], at_q_norm[j], at_k_norm[j],
                                       at_lam[j], at_subln[j], at_w_out[j], lambda_init, batch=batch, seq=seq)
        elif kind == 1:
            xr = _hgrn_layer(xr, norm1_g[i], sh1, sc1, g1, hg_w_in[j], hg_lower_bounds, hg_norm_g[j],
                             hg_w_out[j], i, batch=batch, seq=seq)
        else:
            xr = _rglru_layer(xr, norm1_g[i], sh1, sc1, g1, lru_w_in[j], lru_conv_w[j], lru_conv_b[j],
                              lru_w_a[j], lru_b_a[j], lru_w_x[j], lru_b_x[j], lru_lambda[j], lru_w_out[j],
                              batch=batch, seq=seq)
        m = i // 2
        if i % 2 == 0:
            xr = _dense_ffn(xr, norm2_g[i], sh2, sc2, g2, ff_w_gu[m].astype(BF16), ff_w_down[m].astype(BF16),
                            seq=seq, tm=min(512, seq), tf=ff_w_down.shape[1])
        else:
            xr = _moe_ffn(xr, norm2_g[i], sh2, sc2, g2, moe_router[m], moe_w_gu[m], moe_w_down[m], seq=seq)
    return xr.reshape(batch, seq, d)
```

```python
import functools
import math

import numpy as np
import jax
import jax.numpy as jnp
from jax import lax
from jax.experimental import pallas as pl
from jax.experimental.pallas import tpu as pltpu

F32 = jnp.float32
BF16 = jnp.bfloat16
HIGHEST = lax.Precision.HIGHEST

EPS = 1e-6
LANES = 128
VMEM_LIMIT_BYTES = 56 * 2**20

N_MIXERS = 3
DA_HEADS = 8
DA_HEAD_DIM = 64
DA_V_DIM = 2 * DA_HEAD_DIM
ROPE_THETA = 10000.0
HG_HEADS = 8
HG_CHUNK = 64
HG_SUB = 16
LRU_BLOCK_W = 128
CONV_WIDTH = 4
LRU_C = 8.0
N_EXPERTS = 8
TOP_K = 2


def _params(*sem):
    return pltpu.CompilerParams(dimension_semantics=sem, vmem_limit_bytes=VMEM_LIMIT_BYTES)


def _sigmoid(x):
    return 1.0 / (1.0 + jnp.exp(-x))


def _silu(x):
    return x * _sigmoid(x)


def _sigmoid_tanh(x):
    return 0.5 * jnp.tanh(0.5 * x) + 0.5


def _modulated(x, g, shift, scale):
    ms = jnp.mean(x * x, axis=-1, keepdims=True)
    return x * lax.rsqrt(ms + EPS) * g * (1.0 + scale) + shift


def _mods_kernel(c_ref, w_ref, b_ref, o_ref):
    c = c_ref[...]
    o_ref[...] = jnp.dot(_silu(c), w_ref[...], preferred_element_type=F32, precision=HIGHEST) + b_ref[...]


def _mods(c, ada_w, ada_b):
    depth, d, n = ada_w.shape
    b = c.shape[0]
    tn = 2048
    return pl.pallas_call(
        _mods_kernel,
        grid=(depth, n // tn),
        in_specs=[
            pl.BlockSpec((b, d), lambda l, j: (0, 0)),
            pl.BlockSpec((None, d, tn), lambda l, j: (l, 0, j)),
            pl.BlockSpec((None, 1, tn), lambda l, j: (l, 0, j)),
        ],
        out_specs=pl.BlockSpec((None, b, tn), lambda l, j: (l, 0, j)),
        out_shape=jax.ShapeDtypeStruct((depth, b, n), F32),
        compiler_params=_params("parallel", "parallel"),
        name="adaln_mods",
    )(c, ada_w, ada_b.reshape(depth, 1, n))


class _RowVec:
    def __init__(self, vec, tm, seq):
        b, d = vec.shape
        assert seq % tm == 0
        tiles_per_batch = seq // tm
        self.array = vec.reshape(b, 1, d)
        self.spec = pl.BlockSpec((None, 1, d), lambda i, *_: (i // tiles_per_batch, 0, 0))


def _const_spec(shape):
    nd = len(shape)
    return pl.BlockSpec(shape, lambda *_: (0,) * nd)


def _nm_matmul_kernel(x_ref, g_ref, sh_ref, sc_ref, w_ref, o_ref, h_ref):
    @pl.when(pl.program_id(1) == 0)
    def _():
        h_ref[...] = _modulated(x_ref[...], g_ref[...], sh_ref[...], sc_ref[...]).astype(BF16)

    o_ref[...] = jnp.dot(h_ref[...], w_ref[...], preferred_element_type=F32).astype(o_ref.dtype)


def _nm_matmul(x, g, shift, scale, w, *, seq, tm, tn, out_dtype):
    t, d = x.shape
    n = w.shape[1]
    assert t % tm == 0 and n % tn == 0
    sh = _RowVec(shift, tm, seq)
    sc = _RowVec(scale, tm, seq)
    return pl.pallas_call(
        _nm_matmul_kernel,
        grid=(t // tm, n // tn),
        in_specs=[
            pl.BlockSpec((tm, d), lambda i, j: (i, 0)),
            _const_spec((1, d)),
            sh.spec, sc.spec,
            pl.BlockSpec((d, tn), lambda i, j: (0, j)),
        ],
        out_specs=pl.BlockSpec((tm, tn), lambda i, j: (i, j)),
        out_shape=jax.ShapeDtypeStruct((t, n), out_dtype),
        scratch_shapes=[pltpu.VMEM((tm, d), BF16)],
        compiler_params=_params("parallel", "arbitrary"),
        name="modulate_matmul",
    )(x, g.reshape(1, d), sh.array, sc.array, w)


def _out_proj_kernel(*refs, prologue, n_in):
    ins = refs[:n_in]
    w_ref, x_ref, gate_ref, o_ref = refs[n_in:]
    a = prologue(*ins)
    mix = jnp.dot(a, w_ref[...], preferred_element_type=F32)
    o_ref[...] = x_ref[...] + gate_ref[...] * mix


def _out_proj(prologue, ins, in_specs, w, x, gate, *, seq, tm):
    t, d = x.shape
    k = w.shape[0]
    gv = _RowVec(gate, tm, seq)
    return pl.pallas_call(
        functools.partial(_out_proj_kernel, prologue=prologue, n_in=len(ins)),
        grid=(t // tm,),
        in_specs=list(in_specs) + [
            _const_spec((k, d)),
            pl.BlockSpec((tm, d), lambda i: (i, 0)),
            gv.spec,
        ],
        out_specs=pl.BlockSpec((tm, d), lambda i: (i, 0)),
        out_shape=jax.ShapeDtypeStruct((t, d), F32),
        compiler_params=_params("parallel"),
        name="out_proj_residual",
    )(*ins, w, x, gv.array)


QK_PREP_WIDTH = 2 * LANES


def _rope_constants(seq):
    half = DA_HEAD_DIM // 2
    inv = 1.0 / (ROPE_THETA ** (jnp.arange(0, DA_HEAD_DIM, 2, dtype=F32) / DA_HEAD_DIM))
    ang = jnp.arange(seq, dtype=F32)[:, None] * inv[None, :]
    reps = QK_PREP_WIDTH // half
    cos = jnp.tile(jnp.cos(ang), (1, reps))
    sin = jnp.tile(jnp.sin(ang), (1, reps))
    lane = np.arange(QK_PREP_WIDTH)
    first = (lane % DA_HEAD_DIM) < half
    rot = np.zeros((QK_PREP_WIDTH, QK_PREP_WIDTH), np.float32)
    rot[(lane + half)[first], lane[first]] = -1.0
    rot[(lane - half)[~first], lane[~first]] = 1.0
    group = (lane[:, None] // DA_HEAD_DIM == lane[None, :] // DA_HEAD_DIM).astype(np.float32)
    return cos, sin, jnp.asarray(rot, BF16), jnp.asarray(group, BF16)


def _attn_in_kernel(x_ref, g_ref, sh_ref, sc_ref, w_ref, gain_ref, cos_ref, sin_ref, rot_ref, grp_ref,
                    o_ref, h_ref):
    j = pl.program_id(1)

    @pl.when(j == 0)
    def _():
        h_ref[...] = _modulated(x_ref[...], g_ref[...], sh_ref[...], sc_ref[...]).astype(BF16)

    y_all = jnp.dot(h_ref[...], w_ref[...], preferred_element_type=F32)

    @pl.when(j == 2)
    def _():
        o_ref[...] = y_all.astype(BF16)

    @pl.when(j < 2)
    def _():
        out_scale = jnp.where(j == 0, DA_HEAD_DIM ** -0.5 * math.log2(math.e), 1.0).astype(F32)
        cos = cos_ref[...]
        sin = sin_ref[...]
        gain = gain_ref[...]
        for h in range(y_all.shape[1] // QK_PREP_WIDTH):
            sl = slice(h * QK_PREP_WIDTH, (h + 1) * QK_PREP_WIDTH)
            y = y_all[:, sl]
            ss = jnp.dot((y * y).astype(BF16), grp_ref[...], preferred_element_type=F32)
            n = y * lax.rsqrt(ss * (1.0 / DA_HEAD_DIM) + EPS) * gain
            r = jnp.dot(n.astype(BF16), rot_ref[...], preferred_element_type=F32)
            o_ref[:, sl] = ((n * cos + r * sin) * out_scale).astype(BF16)


def _attn_in_proj(x, g, shift, scale, w, q_gain, k_gain, cos, sin, rot, grp, *, seq, tm):
    t, d = x.shape
    width = DA_HEADS * LANES
    assert w.shape[1] == 3 * width
    reps = QK_PREP_WIDTH // DA_HEAD_DIM
    gains = jnp.stack([jnp.tile(q_gain, reps), jnp.tile(k_gain, reps)]).reshape(2, 1, QK_PREP_WIDTH).astype(F32)
    tiles_per_seq = seq // tm
    sh = _RowVec(shift, tm, seq)
    sc = _RowVec(scale, tm, seq)
    return pl.pallas_call(
        _attn_in_kernel,
        grid=(t // tm, 3),
        in_specs=[
            pl.BlockSpec((tm, d), lambda i, j: (i, 0)),
            _const_spec((1, d)),
            sh.spec, sc.spec,
            pl.BlockSpec((d, width), lambda i, j: (0, j)),
            pl.BlockSpec((None, 1, QK_PREP_WIDTH), lambda i, j: (jnp.minimum(j, 1), 0, 0)),
            pl.BlockSpec((tm, QK_PREP_WIDTH), lambda i, j: (i % tiles_per_seq, 0)),
            pl.BlockSpec((tm, QK_PREP_WIDTH), lambda i, j: (i % tiles_per_seq, 0)),
            _const_spec((QK_PREP_WIDTH, QK_PREP_WIDTH)),
            _const_spec((QK_PREP_WIDTH, QK_PREP_WIDTH)),
        ],
        out_specs=pl.BlockSpec((None, tm, width), lambda i, j: (j, i, 0)),
        out_shape=jax.ShapeDtypeStruct((3, t, width), BF16),
        scratch_shapes=[pltpu.VMEM((tm, d), BF16)],
        compiler_params=_params("parallel", "arbitrary"),
        name="attn_in_proj",
    )(x, g.reshape(1, d), sh.array, sc.array, w, gains, cos, sin, rot, grp)


def _flash_kernel(q_ref, k_ref, v_ref, lam_ref, subln_ref, o_ref,
                  vt_ref, kmax_ref, bound_ref, mrun_ref, acc_ref, den_ref, *, tq, tk, lambda_init):
    seq = k_ref.shape[0]
    n_kv = seq // tk
    lane = lax.broadcasted_iota(jnp.int32, (LANES, LANES), 0)
    first_half = jnp.where(lane < DA_HEAD_DIM, 1.0, 0.0).astype(BF16)
    second_half = jnp.where(lane >= DA_HEAD_DIM, 1.0, 0.0).astype(BF16)

    @pl.when(pl.program_id(2) == 0)
    def _():
        def prep(j, carry):
            n1, n2 = carry
            r0 = pl.multiple_of(j * tk, tk)
            vt_ref[:, pl.ds(r0, tk)] = v_ref[pl.ds(r0, tk), :].astype(F32).T.astype(BF16)
            kf = k_ref[pl.ds(r0, tk), :].astype(F32)
            sq = (kf * kf).astype(BF16)
            s1 = jnp.dot(sq, first_half, preferred_element_type=F32)
            s2 = jnp.dot(sq, second_half, preferred_element_type=F32)
            return (jnp.maximum(n1, jnp.max(s1, axis=0, keepdims=True)),
                    jnp.maximum(n2, jnp.max(s2, axis=0, keepdims=True)))

        zero = jnp.zeros((1, LANES), F32)
        n1, n2 = lax.fori_loop(0, n_kv, prep, (zero, zero))
        kmax_ref[0:1, :] = n1
        kmax_ref[1:2, :] = n2

    q = q_ref[...]
    qlane = lax.broadcasted_iota(jnp.int32, q.shape, 1)
    zero = jnp.zeros_like(q)
    qq = jnp.concatenate([jnp.where(qlane < DA_HEAD_DIM, q, zero),
                          jnp.where(qlane >= DA_HEAD_DIM, q, zero)], axis=0)
    qf = qq.astype(F32)
    nt = (((1,), (1,)), ((), ()))
    qn = lax.dot_general(jnp.ones((8, LANES), BF16), (qf * qf).astype(BF16), nt,
                         preferred_element_type=F32)[0:1]
    col = lax.broadcasted_iota(jnp.int32, qn.shape, 1)
    kn = jnp.where(col < tq, kmax_ref[0:1, 0:1], kmax_ref[1:2, 0:1])
    bound = jnp.sqrt(qn * kn) * 1.01
    bound_ref[...] = bound
    acc_ref[...] = jnp.zeros(acc_ref.shape, F32)
    den_ref[...] = jnp.zeros(den_ref.shape, F32)
    safe = jnp.max(bound) <= 60.0

    def scores(j):
        r0 = pl.multiple_of(j * tk, tk)
        st = lax.dot_general(k_ref[pl.ds(r0, tk), :], qq, nt, preferred_element_type=F32)
        return st, vt_ref[:, pl.ds(r0, tk)]

    @pl.when(safe)
    def _():
        def body(j, carry):
            st, vt = scores(j)
            p = jnp.exp2(st - bound_ref[...])
            den_ref[...] += jnp.sum(p, axis=0, keepdims=True)
            acc_ref[...] += jnp.dot(vt, p.astype(BF16), preferred_element_type=F32)
            return carry

        lax.fori_loop(0, n_kv, body, 0, unroll=2)

    @pl.when(jnp.logical_not(safe))
    def _():
        mrun_ref[...] = jnp.full(mrun_ref.shape, -jnp.inf, F32)

        def body(j, carry):
            st, vt = scores(j)
            m_old = mrun_ref[...]
            m_new = jnp.maximum(m_old, jnp.max(st, axis=0, keepdims=True))
            alpha = jnp.exp2(m_old - m_new)
            p = jnp.exp2(st - m_new)
            den_ref[...] = alpha * den_ref[...] + jnp.sum(p, axis=0, keepdims=True)
            acc_ref[...] = alpha * acc_ref[...] + jnp.dot(vt, p.astype(BF16), preferred_element_type=F32)
            mrun_ref[...] = m_new
            return carry

        lax.fori_loop(0, n_kv, body, 0)

    lf = lam_ref[...]
    lam = (jnp.exp(jnp.sum(lf[0:1] * lf[1:2], axis=1, keepdims=True))
           - jnp.exp(jnp.sum(lf[2:3] * lf[3:4], axis=1, keepdims=True)) + lambda_init)
    inv = 1.0 / den_ref[...]
    o1 = acc_ref[:, 0:tq] * inv[:, 0:tq]
    o2 = acc_ref[:, tq:2 * tq] * inv[:, tq:2 * tq]
    ot = o1 - lam * o2
    ms = jnp.mean(ot * ot, axis=0, keepdims=True)
    o = (ot * lax.rsqrt(ms + EPS)).T * (subln_ref[...] * (1.0 - lambda_init))
    o_ref[...] = o.astype(BF16)


def _flash(qkv, lam_vecs, subln_g, *, batch, seq, lambda_init, tq, tk):
    t = batch * seq
    width = DA_HEADS * LANES
    q_tiles = seq // tq
    return pl.pallas_call(
        functools.partial(_flash_kernel, tq=tq, tk=tk, lambda_init=lambda_init),
        grid=(batch, DA_HEADS, q_tiles),
        in_specs=[
            pl.BlockSpec((None, tq, LANES), lambda b, h, i: (0, b * q_tiles + i, h)),
            pl.BlockSpec((None, seq, LANES), lambda b, h, i: (1, b, h)),
            pl.BlockSpec((None, seq, LANES), lambda b, h, i: (2, b, h)),
            _const_spec((4, DA_HEAD_DIM)),
            _const_spec((1, DA_V_DIM)),
        ],
        out_specs=pl.BlockSpec((tq, LANES), lambda b, h, i: (b * q_tiles + i, h)),
        out_shape=jax.ShapeDtypeStruct((t, width), BF16),
        scratch_shapes=[
            pltpu.VMEM((LANES, seq), BF16),
            pltpu.VMEM((8, LANES), F32),
            pltpu.VMEM((1, 2 * tq), F32),
            pltpu.VMEM((1, 2 * tq), F32),
            pltpu.VMEM((LANES, 2 * tq), F32),
            pltpu.VMEM((1, 2 * tq), F32),
        ],
        compiler_params=_params("parallel", "parallel", "arbitrary"),
        name="diff_flash_attention",
    )(qkv, qkv, qkv, lam_vecs.astype(F32), subln_g.reshape(1, DA_V_DIM).astype(F32))


def _identity_prologue(o_ref):
    return o_ref[...]


def _diff_attention_layer(x, g, shift, scale, gate, w_in, q_norm, k_norm, lam_vecs, subln_g, w_out,
                          lambda_init, *, batch, seq):
    cos, sin, rot, grp = _rope_constants(seq)
    qkv = _attn_in_proj(x, g, shift, scale, w_in.astype(BF16), q_norm, k_norm, cos, sin, rot, grp,
                        seq=seq, tm=min(1024, seq))
    o = _flash(qkv, lam_vecs, subln_g, batch=batch, seq=seq, lambda_init=lambda_init,
               tq=min(1024, seq), tk=min(512, seq))
    tm = min(512, seq)
    return _out_proj(_identity_prologue, [o], [pl.BlockSpec((tm, o.shape[1]), lambda i: (i, 0))],
                     w_out.astype(BF16), x, gate, seq=seq, tm=tm)


def _hgrn_scan_kernel(q_ref, f_ref, v_ref, lb_ref, o_ref, st_ref, *, reverse, layer):
    c = HG_CHUNK
    n_chunks = q_ref.shape[0] // c

    @pl.when(pl.program_id(1) == 0)
    def _():
        st_ref[...] = jnp.zeros(st_ref.shape, F32)

    lbw = lb_ref[...]
    e = jnp.exp(lbw - jnp.max(lbw, axis=0, keepdims=True))
    lb = jnp.sum(e[1:layer + 1], axis=0, keepdims=True) / jnp.sum(e, axis=0, keepdims=True)

    row = lax.broadcasted_iota(jnp.int32, (c, c), 0)
    col = lax.broadcasted_iota(jnp.int32, (c, c), 1)
    tri = (col >= row) if reverse else (col <= row)
    tri = tri.astype(F32)
    n_sub = c // HG_SUB
    half = HG_SUB // 2
    col8 = lax.broadcasted_iota(jnp.int32, (half, c), 1)
    row8 = lax.broadcasted_iota(jnp.int32, (half, c), 0)

    def key_rows(i, r):
        t0 = i * HG_SUB + r * half
        return range(t0, (i + 1) * HG_SUB) if reverse else range(i * HG_SUB, t0 + half)

    def placement(i, r, s):
        t0 = i * HG_SUB + r * half
        reached = (row8 + t0 <= s) if reverse else (row8 + t0 >= s)
        return jnp.logical_and(col8 == s, reached)

    place = {(i, r, s): placement(i, r, s) for i in range(n_sub) for r in range(2) for s in key_rows(i, r)}

    def chunk(ci, carry):
        cidx = (n_chunks - 1 - ci) if reverse else ci
        r0 = pl.multiple_of(cidx * c, c)
        qs = _silu(q_ref[pl.ds(r0, c), :])
        forget = lb + (1.0 - lb) * _sigmoid(f_ref[pl.ds(r0, c), :])
        kk = 1.0 - forget
        vb = v_ref[pl.ds(r0, c), :].astype(BF16)
        b = jnp.dot(tri, jnp.log2(forget), preferred_element_type=F32, precision=HIGHEST)
        b_edge = b[0:1] if reverse else b[c - 1:c]
        q_in = (qs * jnp.exp2(b)).astype(BF16)
        k_out = (kk * jnp.exp2(b_edge - b)).astype(BF16)
        dec = jnp.exp2(b_edge)
        key_shift = b - jnp.log2(kk)

        pieces = [[jnp.zeros((half, c), F32) for _ in range(2 * n_sub)] for _ in range(HG_HEADS)]
        for i in range(n_sub):
            for r in range(2):
                t0 = i * HG_SUB + r * half
                q8 = qs[t0:t0 + half]
                b8 = b[t0:t0 + half]
                for s in key_rows(i, r):
                    term = q8 * jnp.exp2(b8 - key_shift[s:s + 1])
                    for h in range(HG_HEADS):
                        a_col = jnp.sum(term[:, h * LANES:(h + 1) * LANES], axis=1, keepdims=True)
                        pieces[h][2 * i + r] = jnp.where(place[i, r, s], a_col, pieces[h][2 * i + r])

        for h in range(HG_HEADS):
            sl = slice(h * LANES, (h + 1) * LANES)
            blocks = []
            for i in range(n_sub):
                rows = slice(i * HG_SUB, (i + 1) * HG_SUB)
                passed = slice((i + 1) * HG_SUB, c) if reverse else slice(0, i * HG_SUB)
                n_passed = passed.stop - passed.start
                if n_passed == 0:
                    blocks.append(jnp.zeros((HG_SUB, c), F32))
                    continue
                ref_row = (i + 1) * HG_SUB if reverse else i * HG_SUB - 1
                b_ref_row = b[ref_row:ref_row + 1, sl]
                q_t = (qs[rows, sl] * jnp.exp2(b[rows, sl] - b_ref_row)).astype(BF16)
                k_p = (kk[passed, sl] * jnp.exp2(b_ref_row - b[passed, sl])).astype(BF16)
                rest = jnp.zeros((c - n_passed, LANES), BF16)
                k_t = jnp.concatenate([rest, k_p] if reverse else [k_p, rest], axis=0)
                blocks.append(lax.dot_general(q_t, k_t, (((1,), (1,)), ((), ())),
                                              preferred_element_type=F32))
            a = jnp.concatenate(pieces[h], axis=0) + jnp.concatenate(blocks, axis=0)
            st = st_ref[h]
            o_h = jnp.dot(a.astype(BF16), vb[:, sl], preferred_element_type=F32)
            o_h = o_h + lax.dot_general(q_in[:, sl], st.astype(BF16), (((1,), (1,)), ((), ())),
                                        preferred_element_type=F32)
            o_ref[pl.ds(r0, c), sl] = o_h.astype(o_ref.dtype)
            upd = lax.dot_general(vb[:, sl], k_out[:, sl], (((0,), (0,)), ((), ())),
                                  preferred_element_type=F32)
            st_ref[h] = st * dec[:, sl] + upd
        return carry

    lax.fori_loop(0, n_chunks, chunk, 0, unroll=2)


def _hgrn_scan(proj, lower_bounds, *, batch, seq, layer, reverse, tt):
    t = batch * seq
    width = HG_HEADS * LANES
    nt = seq // tt
    depth = lower_bounds.shape[1]
    d_idx = 1 if reverse else 0

    def rows(b, i):
        return b * nt + ((nt - 1 - i) if reverse else i)

    return pl.pallas_call(
        functools.partial(_hgrn_scan_kernel, reverse=reverse, layer=layer),
        grid=(batch, nt),
        in_specs=[
            pl.BlockSpec((tt, width), lambda b, i: (rows(b, i), 0)),
            pl.BlockSpec((tt, width), lambda b, i: (rows(b, i), 1 + d_idx)),
            pl.BlockSpec((tt, width), lambda b, i: (rows(b, i), 3)),
            pl.BlockSpec((None, depth, width), lambda b, i: (d_idx, 0, 0)),
        ],
        out_specs=pl.BlockSpec((tt, width), lambda b, i: (rows(b, i), 0)),
        out_shape=jax.ShapeDtypeStruct((t, width), BF16),
        scratch_shapes=[pltpu.VMEM((HG_HEADS, LANES, LANES), F32)],
        compiler_params=_params("parallel", "arbitrary"),
        name="hgrn2_scan_bwd" if reverse else "hgrn2_scan_fwd",
    )(proj, proj, proj, lower_bounds.astype(F32))


def _hgrn_prologue(of_ref, ob_ref, g_ref, ng_ref):
    ng = ng_ref[...]
    outs = []
    for h in range(HG_HEADS):
        sl = slice(h * LANES, (h + 1) * LANES)
        o = of_ref[:, sl].astype(F32) + ob_ref[:, sl].astype(F32)
        ms = jnp.mean(o * o, axis=-1, keepdims=True)
        outs.append((o * lax.rsqrt(ms + EPS) * ng * _silu(g_ref[:, sl])).astype(BF16))
    return jnp.concatenate(outs, axis=1)


def _hgrn_layer(x, g, shift, scale, gate, w_in, lower_bounds, norm_g, w_out, layer, *, batch, seq):
    width = HG_HEADS * LANES
    proj = _nm_matmul(x, g, shift, scale, w_in.astype(BF16), seq=seq,
                      tm=min(1024, seq), tn=1024, out_dtype=F32)
    tt = min(512, seq)
    o_f = _hgrn_scan(proj, lower_bounds, batch=batch, seq=seq, layer=layer, reverse=False, tt=tt)
    o_b = _hgrn_scan(proj, lower_bounds, batch=batch, seq=seq, layer=layer, reverse=True, tt=tt)
    tm = min(512, seq)
    ins = [o_f, o_b, proj, norm_g.reshape(1, LANES).astype(F32)]
    specs = [
        pl.BlockSpec((tm, width), lambda i: (i, 0)),
        pl.BlockSpec((tm, width), lambda i: (i, 0)),
        pl.BlockSpec((tm, width), lambda i: (i, 4)),
        _const_spec((1, LANES)),
    ]
    return _out_proj(_hgrn_prologue, ins, specs, w_out.astype(BF16), x, gate, seq=seq, tm=tm)


def _lru_scan_kernel(prev_ref, main_ref, next_ref, cw_ref, cb_ref, wg_ref, ba_ref, bx_ref, lam_ref,
                     o_ref, a_ref, u_ref, h_ref, *, batch, nt):
    d = pl.program_id(0)
    i = pl.program_id(1)
    ti = i + d * (nt - 1 - 2 * i)
    rows, width = main_ref.shape
    tt = rows // batch

    @pl.when(i == 0)
    def _():
        h_ref[...] = jnp.zeros(h_ref.shape, F32)

    prev = jnp.where(ti == 0, 0.0, prev_ref[...])
    nxt = jnp.where(ti == nt - 1, 0.0, next_ref[...])
    ext = jnp.concatenate([prev, main_ref[...], nxt], axis=0)
    cw = cw_ref[...]
    xc = cb_ref[...] + cw[0:1] * ext[0:rows]
    for j in range(1, CONV_WIDTH):
        xc = xc + cw[j:j + 1] * ext[j * batch:j * batch + rows]

    neg_softplus = -LRU_C * (jnp.maximum(-lam_ref[...], 0.0) + jnp.log1p(jnp.exp(-jnp.abs(lam_ref[...]))))
    for n in range(width // LRU_BLOCK_W):
        sl = slice(n * LRU_BLOCK_W, (n + 1) * LRU_BLOCK_W)
        xb = xc[:, sl]
        gates = jnp.dot(xb.astype(BF16), wg_ref[n], preferred_element_type=F32)
        r = _sigmoid_tanh(gates[:, :LRU_BLOCK_W] + ba_ref[:, sl])
        ig = _sigmoid_tanh(gates[:, LRU_BLOCK_W:] + bx_ref[:, sl])
        a = jnp.exp(r * neg_softplus[:, sl])
        a_ref[:, sl] = a
        t = 1.0 - a * a
        u_ref[:, sl] = (t * lax.rsqrt(jnp.maximum(t, 1e-30))) * (ig * xb)

    def step(t, h):
        tl = t + d * (tt - 1 - 2 * t)
        r0 = pl.multiple_of(tl * batch, batch)
        h = a_ref[pl.ds(r0, batch), :] * h + u_ref[pl.ds(r0, batch), :]
        o_ref[pl.ds(r0, batch), :] = h
        return h

    h_ref[...] = lax.fori_loop(0, tt, step, h_ref[...])


def _lru_scan(proj, conv_w, conv_b, w_a, b_a, w_x, b_x, lam, *, batch, seq, tt):
    t = batch * seq
    width = conv_w.shape[1]
    n_blocks = width // LRU_BLOCK_W
    nt = seq // tt
    rows = tt * batch
    assert tt % 2 == 0

    def tile(d, i):
        return i + d * (nt - 1 - 2 * i)

    wg = jnp.concatenate([w_a, w_x], axis=-1).astype(BF16)
    vec = lambda a: a.reshape(2, 1, width).astype(F32)
    return pl.pallas_call(
        functools.partial(_lru_scan_kernel, batch=batch, nt=nt),
        grid=(2, nt),
        in_specs=[
            pl.BlockSpec((batch, width), lambda d, i: (jnp.maximum(tile(d, i) * tt - 1, 0), 1)),
            pl.BlockSpec((rows, width), lambda d, i: (tile(d, i), 1)),
            pl.BlockSpec((2 * batch, width),
                         lambda d, i: (jnp.minimum((tile(d, i) + 1) * (tt // 2), seq // 2 - 1), 1)),
            _const_spec((CONV_WIDTH, width)),
            _const_spec((1, width)),
            pl.BlockSpec((None, n_blocks, LRU_BLOCK_W, 2 * LRU_BLOCK_W), lambda d, i: (d, 0, 0, 0)),
            pl.BlockSpec((None, 1, width), lambda d, i: (d, 0, 0)),
            pl.BlockSpec((None, 1, width), lambda d, i: (d, 0, 0)),
            pl.BlockSpec((None, 1, width), lambda d, i: (d, 0, 0)),
        ],
        out_specs=pl.BlockSpec((None, rows, width), lambda d, i: (d, tile(d, i), 0)),
        out_shape=jax.ShapeDtypeStruct((2, t, width), F32),
        scratch_shapes=[
            pltpu.VMEM((rows, width), F32),
            pltpu.VMEM((rows, width), F32),
            pltpu.VMEM((batch, width), F32),
        ],
        compiler_params=_params("arbitrary", "arbitrary"),
        name="rglru_scan",
    )(proj, proj, proj, conv_w.astype(F32), conv_b.reshape(1, width).astype(F32), wg,
      vec(b_a), vec(b_x), vec(lam))


def _gelu_tanh(x):
    return 0.5 * x * (1.0 + jnp.tanh(math.sqrt(2.0 / math.pi) * (x + 0.044715 * (x * x * x))))


def _lru_in_kernel(x_ref, g_ref, sh_ref, sc_ref, w_ref, o_ref, xm_ref, h_ref, *, batch, ts):
    n_lane_blocks = x_ref.shape[2] // LANES
    for b in range(batch):
        xm = _modulated(x_ref[b], g_ref[...], sh_ref[b:b + 1, :], sc_ref[b:b + 1, :])
        for c in range(n_lane_blocks):
            xm_ref[c, b * ts:(b + 1) * ts, :] = xm[:, c * LANES:(c + 1) * LANES]
    for s in range(ts):
        for c in range(n_lane_blocks):
            h_ref[s * batch:(s + 1) * batch, c * LANES:(c + 1) * LANES] = \
                xm_ref[c, pl.ds(s, batch, stride=ts), :]
    o_ref[...] = jnp.dot(h_ref[...].astype(BF16), w_ref[...], preferred_element_type=F32)


def _lru_in_proj(x, g, shift, scale, w, *, batch, seq, ts):
    t, d = x.shape
    n = w.shape[1]
    rows = ts * batch
    return pl.pallas_call(
        functools.partial(_lru_in_kernel, batch=batch, ts=ts),
        grid=(seq // ts,),
        in_specs=[
            pl.BlockSpec((batch, ts, d), lambda i: (0, i, 0)),
            _const_spec((1, d)),
            _const_spec((batch, d)),
            _const_spec((batch, d)),
            _const_spec((d, n)),
        ],
        out_specs=pl.BlockSpec((rows, n), lambda i: (i, 0)),
        out_shape=jax.ShapeDtypeStruct((t, n), F32),
        scratch_shapes=[pltpu.VMEM((d // LANES, rows, LANES), F32), pltpu.VMEM((rows, d), F32)],
        compiler_params=_params("parallel"),
        name="rglru_in_proj",
    )(x.reshape(batch, seq, d), g.reshape(1, d), shift, scale, w)


def _lru_out_kernel(hf_ref, hb_ref, y_ref, w_ref, x_ref, gate_ref, o_ref, mix_ref, *, batch, ts):
    a = ((hf_ref[...] + hb_ref[...]) * _gelu_tanh(y_ref[...])).astype(BF16)
    mix = jnp.dot(a, w_ref[...], preferred_element_type=F32)
    n_lane_blocks = mix.shape[1] // LANES
    for c in range(n_lane_blocks):
        mix_ref[c] = mix[:, c * LANES:(c + 1) * LANES]
    for b in range(batch):
        for c in range(n_lane_blocks):
            sl = slice(c * LANES, (c + 1) * LANES)
            o_ref[b, :, sl] = x_ref[b, :, sl] + gate_ref[b:b + 1, sl] * mix_ref[c, pl.ds(b, ts, stride=batch), :]


def _lru_out_proj(h2, proj, w, x, gate, *, batch, seq, ts):
    t, d = x.shape
    width = w.shape[0]
    rows = ts * batch
    out = pl.pallas_call(
        functools.partial(_lru_out_kernel, batch=batch, ts=ts),
        grid=(seq // ts,),
        in_specs=[
            pl.BlockSpec((None, rows, width), lambda i: (0, i, 0)),
            pl.BlockSpec((None, rows, width), lambda i: (1, i, 0)),
            pl.BlockSpec((rows, width), lambda i: (i, 0)),
            _const_spec((width, d)),
            pl.BlockSpec((batch, ts, d), lambda i: (0, i, 0)),
            _const_spec((batch, d)),
        ],
        out_specs=pl.BlockSpec((batch, ts, d), lambda i: (0, i, 0)),
        out_shape=jax.ShapeDtypeStruct((batch, seq, d), F32),
        scratch_shapes=[pltpu.VMEM((d // LANES, rows, LANES), F32)],
        compiler_params=_params("parallel"),
        name="rglru_out_proj",
    )(h2, h2, proj, w, x.reshape(batch, seq, d), gate)
    return out.reshape(t, d)


def _rglru_layer(x, g, shift, scale, gate, w_in, conv_w, conv_b, w_a, b_a, w_x, b_x, lam, w_out,
                 *, batch, seq):
    width = conv_w.shape[1]
    ts = min(64, seq)
    proj = _lru_in_proj(x, g, shift, scale, w_in.astype(BF16), batch=batch, seq=seq, ts=ts)
    h2 = _lru_scan(proj, conv_w, conv_b, w_a, b_a, w_x, b_x, lam, batch=batch, seq=seq, tt=min(64, seq))
    return _lru_out_proj(h2, proj, w_out.astype(BF16), x, gate, batch=batch, seq=seq, ts=ts)


def _ffn_kernel(x_ref, g_ref, sh_ref, sc_ref, wg_ref, wu_ref, wd_ref, gate_ref, o_ref, h_ref, acc_ref):
    j = pl.program_id(1)

    @pl.when(j == 0)
    def _():
        h_ref[...] = _modulated(x_ref[...], g_ref[...], sh_ref[...], sc_ref[...]).astype(BF16)
        acc_ref[...] = jnp.zeros(acc_ref.shape, F32)

    h = h_ref[...]
    gt = jnp.dot(h, wg_ref[...], preferred_element_type=F32)
    up = jnp.dot(h, wu_ref[...], preferred_element_type=F32)
    act = (_silu(gt) * up).astype(BF16)
    acc_ref[...] += jnp.dot(act, wd_ref[...], preferred_element_type=F32)

    @pl.when(j == pl.num_programs(1) - 1)
    def _():
        o_ref[...] = x_ref[...] + gate_ref[...] * acc_ref[...]


def _dense_ffn(x, g, shift, scale, gate, w_gu, w_down, layer, *, seq, tm, tf):
    t, d = x.shape
    f = w_down.shape[1]
    assert f % tf == 0 and seq % tm == 0
    nf = f // tf
    sh = _RowVec(shift, tm, seq)
    sc = _RowVec(scale, tm, seq)
    gv = _RowVec(gate, tm, seq)
    weight_buffers = pl.Buffered(1) if nf == 1 else pl.Buffered(2)
    return pl.pallas_call(
        _ffn_kernel,
        grid=(t // tm, nf),
        in_specs=[
            pl.BlockSpec((tm, d), lambda i, j: (i, 0)),
            _const_spec((1, d)),
            sh.spec, sc.spec,
            pl.BlockSpec((None, d, tf), lambda i, j: (layer, 0, j), pipeline_mode=weight_buffers),
            pl.BlockSpec((None, d, tf), lambda i, j: (layer, 0, nf + j), pipeline_mode=weight_buffers),
            pl.BlockSpec((None, tf, d), lambda i, j: (layer, j, 0), pipeline_mode=weight_buffers),
            gv.spec,
        ],
        out_specs=pl.BlockSpec((tm, d), lambda i, j: (i, 0)),
        out_shape=jax.ShapeDtypeStruct((t, d), F32),
        scratch_shapes=[pltpu.VMEM((tm, d), BF16), pltpu.VMEM((tm, d), F32)],
        compiler_params=_params("parallel", "arbitrary"),
        name="dense_swiglu",
    )(x, g.reshape(1, d), sh.array, sc.array, w_gu, w_gu, w_down, gv.array)


MOE_BLOCK = 512
MOE_CHUNK = 16
MOE_TILE = 1024


def _route_kernel(x_ref, g_ref, sh_ref, sc_ref, rhi_ref, rlo_ref, xs_ref, meta_ref, cnt_ref):
    tb = x_ref.shape[0]
    local_rows = xs_ref.shape[0]
    hf = _modulated(x_ref[...], g_ref[...], sh_ref[...], sc_ref[...])
    h_hi = hf.astype(BF16)
    h_lo = (hf - h_hi.astype(F32)).astype(BF16)
    logits = (jnp.dot(h_hi, rhi_ref[...], preferred_element_type=F32)
              + jnp.dot(h_lo, rhi_ref[...], preferred_element_type=F32)
              + jnp.dot(h_hi, rlo_ref[...], preferred_element_type=F32))
    lane = lax.broadcasted_iota(jnp.int32, logits.shape, 1).astype(F32)
    logits = jnp.where(lane < N_EXPERTS, logits, -jnp.inf)
    m1 = jnp.max(logits, axis=1, keepdims=True)
    e1 = jnp.min(jnp.where(logits == m1, lane, float(LANES)), axis=1, keepdims=True)
    rest = jnp.where(lane == e1, -jnp.inf, logits)
    m2 = jnp.max(rest, axis=1, keepdims=True)
    e2 = jnp.min(jnp.where(rest == m2, lane, float(LANES)), axis=1, keepdims=True)
    gate2 = 1.0 / (1.0 + jnp.exp(m1 - m2))
    gate1 = 1.0 - gate2
    hot1 = (lane == e1).astype(F32)
    hot2 = (lane == e2).astype(F32)
    sel = hot1 + hot2
    row = lax.broadcasted_iota(jnp.int32, (tb, tb), 0)
    col = lax.broadcasted_iota(jnp.int32, (tb, tb), 1)
    before = (col < row).astype(BF16)
    seen = jnp.dot(before, sel.astype(BF16), preferred_element_type=F32)
    chunks = jnp.floor((jnp.sum(sel, axis=0, keepdims=True) + (MOE_CHUNK - 1.0)) * (1.0 / MOE_CHUNK))
    cnt_ref[...] = chunks
    lrow = lax.broadcasted_iota(jnp.int32, (LANES, LANES), 0)
    lcol = lax.broadcasted_iota(jnp.int32, (LANES, LANES), 1)
    lower_experts = (lrow < lcol).astype(BF16)
    first_chunk = jnp.dot(jnp.broadcast_to(chunks, (8, LANES)).astype(BF16), lower_experts,
                          preferred_element_type=F32)[0:1]
    base = first_chunk * float(MOE_CHUNK) + seen
    ld1 = jnp.sum(hot1 * base, axis=1, keepdims=True)
    ld2 = jnp.sum(hot2 * base, axis=1, keepdims=True)
    meta = jnp.zeros(logits.shape, F32)
    for k, val in enumerate((e1, e2, gate1, gate2, ld1, ld2)):
        meta = jnp.where(lane == float(k), val, meta)
    meta_ref[...] = meta
    eye = row == col
    ld1_row = jnp.sum(jnp.where(eye, ld1, 0.0), axis=0, keepdims=True)
    ld2_row = jnp.sum(jnp.where(eye, ld2, 0.0), axis=0, keepdims=True)
    r_iota = lax.broadcasted_iota(jnp.int32, (local_rows, tb), 0).astype(F32)
    pick = jnp.where(r_iota == ld1_row, 1.0, jnp.where(r_iota == ld2_row, 1.0, 0.0)).astype(BF16)
    xs_ref[...] = jnp.dot(pick, h_hi, preferred_element_type=F32).astype(BF16)


def _route(x, g, shift, scale, router, *, seq):
    t, d = x.shape
    tb = min(MOE_BLOCK, seq)
    local_rows = TOP_K * tb + N_EXPERTS * MOE_CHUNK
    nb = t // tb
    sh = _RowVec(shift, tb, seq)
    sc = _RowVec(scale, tb, seq)
    router_pad = jnp.zeros((d, LANES), F32).at[:, :N_EXPERTS].set(router.astype(F32))
    router_hi = router_pad.astype(BF16)
    router_lo = (router_pad - router_hi.astype(F32)).astype(BF16)
    return pl.pallas_call(
        _route_kernel,
        grid=(nb,),
        in_specs=[
            pl.BlockSpec((tb, d), lambda i: (i, 0)),
            _const_spec((1, d)),
            sh.spec, sc.spec,
            _const_spec((d, LANES)),
            _const_spec((d, LANES)),
        ],
        out_specs=[
            pl.BlockSpec((local_rows, d), lambda i: (i, 0)),
            pl.BlockSpec((tb, LANES), lambda i: (i, 0)),
            pl.BlockSpec((None, 1, LANES), lambda i: (i, 0, 0)),
        ],
        out_shape=[
            jax.ShapeDtypeStruct((nb * local_rows, d), BF16),
            jax.ShapeDtypeStruct((t, LANES), F32),
            jax.ShapeDtypeStruct((nb, 1, LANES), F32),
        ],
        compiler_params=_params("parallel"),
        name="moe_route",
    )(x, g.reshape(1, d), sh.array, sc.array, router_hi, router_lo)


def _start_chunk_gather(row_smem, slot, src_hbm, dst_vmem, sem):
    n = dst_vmem.shape[1] // MOE_CHUNK

    def start(k, c):
        r0 = pl.multiple_of(row_smem[slot, k], MOE_CHUNK)
        pltpu.make_async_copy(src_hbm.at[pl.ds(r0, MOE_CHUNK)],
                              dst_vmem.at[slot, pl.ds(pl.multiple_of(k * MOE_CHUNK, MOE_CHUNK), MOE_CHUNK)],
                              sem.at[slot]).start()
        return c

    lax.fori_loop(0, n, start, 0, unroll=8)


def _wait_chunk_gather(slot, src_hbm, dst_vmem, sem):
    n = dst_vmem.shape[1]
    pltpu.make_async_copy(src_hbm.at[pl.ds(0, n)], dst_vmem.at[slot], sem.at[slot]).wait()


def _expert_kernel(te_ref, nu_ref, src_ref, xs_ref, wg_ref, wu_ref, wd_ref, o_ref,
                   idx_ref, xg_ref, acc_ref, sem_idx, sem_rows):
    i = pl.program_id(0)
    j = pl.program_id(1)
    n_used = nu_ref[0]
    used = i < n_used
    slot = i % 2

    def idx_copy(tile, into):
        return pltpu.make_async_copy(src_ref.at[tile], idx_ref.at[into], sem_idx)

    @pl.when(jnp.logical_and(used, j == 0))
    def _():
        @pl.when(i == 0)
        def _():
            idx_copy(0, 0).start()
            idx_copy(0, 0).wait()
            _start_chunk_gather(idx_ref, 0, xs_ref, xg_ref, sem_rows)

        has_next = i + 1 < n_used

        @pl.when(has_next)
        def _():
            idx_copy(i + 1, 1 - slot).start()

        _wait_chunk_gather(slot, xs_ref, xg_ref, sem_rows)
        acc_ref[...] = jnp.zeros(acc_ref.shape, F32)

        @pl.when(has_next)
        def _():
            idx_copy(i + 1, 1 - slot).wait()
            _start_chunk_gather(idx_ref, 1 - slot, xs_ref, xg_ref, sem_rows)

    @pl.when(used)
    def _():
        h = xg_ref[slot]
        gt = jnp.dot(h, wg_ref[...], preferred_element_type=F32)
        up = jnp.dot(h, wu_ref[...], preferred_element_type=F32)
        act = (_silu(gt) * up).astype(BF16)
        acc_ref[...] += jnp.dot(act, wd_ref[...], preferred_element_type=F32)

    @pl.when(j == pl.num_programs(1) - 1)
    def _():
        o_ref[...] = jnp.where(used, acc_ref[...], 0.0).astype(o_ref.dtype)


def _experts(xs, src_tiles, tile_expert, n_used, w_gu, w_down, layer, *, tm, tf):
    d = xs.shape[1]
    n_tiles, chunks_per_tile = src_tiles.shape
    assert chunks_per_tile * MOE_CHUNK == tm
    f = w_down.shape[2]
    nf = f // tf

    def jj(i, j, nu):
        return jnp.where(i < nu[0], j, nf - 1)

    grid_spec = pltpu.PrefetchScalarGridSpec(
        num_scalar_prefetch=2,
        grid=(n_tiles, nf),
        in_specs=[
            pl.BlockSpec(memory_space=pl.ANY),
            pl.BlockSpec(memory_space=pl.ANY),
            pl.BlockSpec((None, None, d, tf), lambda i, j, te, nu: (layer, te[i], 0, jj(i, j, nu))),
            pl.BlockSpec((None, None, d, tf), lambda i, j, te, nu: (layer, te[i], 0, nf + jj(i, j, nu))),
            pl.BlockSpec((None, None, tf, d), lambda i, j, te, nu: (layer, te[i], jj(i, j, nu), 0)),
        ],
        out_specs=pl.BlockSpec((tm, d), lambda i, j, te, nu: (i, 0)),
        scratch_shapes=[
            pltpu.SMEM((2, chunks_per_tile), jnp.int32),
            pltpu.VMEM((2, tm, d), BF16),
            pltpu.VMEM((tm, d), F32),
            pltpu.SemaphoreType.DMA,
            pltpu.SemaphoreType.DMA((2,)),
        ],
    )
    return pl.pallas_call(
        _expert_kernel,
        grid_spec=grid_spec,
        out_shape=jax.ShapeDtypeStruct((n_tiles * tm, d), BF16),
        compiler_params=_params("arbitrary", "arbitrary"),
        name="moe_experts",
    )(tile_expert, n_used, src_tiles, xs, w_gu, w_gu, w_down)


def _combine_kernel(src_ref, ys_ref, x_ref, meta_ref, gate_ref, o_ref, idx_ref, yl_ref, sem_idx, sem_rows):
    i = pl.program_id(0)
    slot = i % 2
    has_next = i + 1 < pl.num_programs(0)

    def idx_copy(block, into):
        return pltpu.make_async_copy(src_ref.at[block], idx_ref.at[into], sem_idx)

    @pl.when(i == 0)
    def _():
        idx_copy(0, 0).start()
        idx_copy(0, 0).wait()
        _start_chunk_gather(idx_ref, 0, ys_ref, yl_ref, sem_rows)

    @pl.when(has_next)
    def _():
        idx_copy(i + 1, 1 - slot).start()

    _wait_chunk_gather(slot, ys_ref, yl_ref, sem_rows)

    @pl.when(has_next)
    def _():
        idx_copy(i + 1, 1 - slot).wait()
        _start_chunk_gather(idx_ref, 1 - slot, ys_ref, yl_ref, sem_rows)

    meta = meta_ref[...]
    yl = yl_ref[slot]
    lane = lax.broadcasted_iota(jnp.int32, (meta.shape[0], yl.shape[0]), 1).astype(F32)
    y1 = jnp.dot(jnp.where(lane == meta[:, 4:5], 1.0, 0.0).astype(BF16), yl, preferred_element_type=F32)
    y2 = jnp.dot(jnp.where(lane == meta[:, 5:6], 1.0, 0.0).astype(BF16), yl, preferred_element_type=F32)
    o_ref[...] = x_ref[...] + gate_ref[...] * (meta[:, 2:3] * y1 + meta[:, 3:4] * y2)


def _combine(ys, src_blocks, x, meta, gate, *, seq):
    t, d = x.shape
    nb, chunks_per_block = src_blocks.shape
    tb = t // nb
    local_rows = chunks_per_block * MOE_CHUNK
    gv = _RowVec(gate, tb, seq)
    return pl.pallas_call(
        _combine_kernel,
        grid=(nb,),
        in_specs=[
            pl.BlockSpec(memory_space=pl.ANY),
            pl.BlockSpec(memory_space=pl.ANY),
            pl.BlockSpec((tb, d), lambda i: (i, 0)),
            pl.BlockSpec((tb, LANES), lambda i: (i, 0)),
            gv.spec,
        ],
        out_specs=pl.BlockSpec((tb, d), lambda i: (i, 0)),
        out_shape=jax.ShapeDtypeStruct((t, d), F32),
        scratch_shapes=[
            pltpu.SMEM((2, chunks_per_block), jnp.int32),
            pltpu.VMEM((2, local_rows, d), BF16),
            pltpu.SemaphoreType.DMA,
            pltpu.SemaphoreType.DMA((2,)),
        ],
        compiler_params=_params("arbitrary"),
        name="moe_combine",
    )(src_blocks, ys, x, meta, gv.array)


def _moe_ffn(x, g, shift, scale, gate, router, w_gu, w_down, layer, *, seq):
    t, d = x.shape
    tb = min(MOE_BLOCK, seq)
    tm_e = min(MOE_TILE, seq)
    xs, meta, chunks = _route(x, g, shift, scale, router, seq=seq)
    nb = t // tb
    cpb = (TOP_K * tb + N_EXPERTS * MOE_CHUNK) // MOE_CHUNK
    cpt = tm_e // MOE_CHUNK

    nch = chunks[:, 0, :N_EXPERTS].astype(jnp.int32)
    local_first = jnp.cumsum(nch, axis=1) - nch
    per_expert = jnp.sum(nch, axis=0)
    tiles = (per_expert + cpt - 1) // cpt
    sorted_end = jnp.cumsum(tiles) * cpt
    sorted_first = sorted_end - tiles * cpt
    in_expert_end = jnp.cumsum(nch, axis=0)
    in_expert_first = in_expert_end - nch

    n_tiles = -(-(TOP_K * t // MOE_CHUNK + nb * N_EXPERTS) // cpt) + N_EXPERTS
    slot = jnp.arange(n_tiles * cpt, dtype=jnp.int32)
    e_of = jnp.minimum(jnp.sum(slot[:, None] >= sorted_end[None, :], axis=1), N_EXPERTS - 1)
    is_e = e_of[:, None] == jnp.arange(N_EXPERTS)[None, :]
    by_expert = lambda v: jnp.sum(jnp.where(is_e, v[None, :], 0), axis=1)
    off = slot - by_expert(sorted_first)
    valid = off < by_expert(per_expert)
    end_of_blocks = jnp.sum(jnp.where(is_e[:, :, None], in_expert_end.T[None], 0), axis=1)
    b_of = jnp.minimum(jnp.sum(off[:, None] >= end_of_blocks, axis=1), nb - 1)
    is_b = b_of[:, None] == jnp.arange(nb)[None, :]
    shift_tab = (local_first - in_expert_first).T
    shift = jnp.sum(jnp.where(is_b, jnp.sum(jnp.where(is_e[:, :, None], shift_tab[None], 0), axis=1), 0), axis=1)
    src_rows = jnp.where(valid, b_of * cpb + shift + off, 0) * MOE_CHUNK
    n_used = jnp.sum(tiles).astype(jnp.int32)
    tile_id = jnp.arange(n_tiles, dtype=jnp.int32)
    frozen = jnp.minimum(tile_id, jnp.maximum(n_used - 1, 0)) * cpt
    tile_expert = jnp.minimum(jnp.sum(frozen[:, None] >= sorted_end[None, :], axis=1), N_EXPERTS - 1)
    tile_expert = tile_expert.astype(jnp.int32)

    lc = jnp.arange(cpb, dtype=jnp.int32)
    local_end = local_first + nch
    e_loc = jnp.sum(lc[None, :, None] >= local_end[:, None, :], axis=2)
    used_loc = e_loc < N_EXPERTS
    is_e_loc = e_loc[:, :, None] == jnp.arange(N_EXPERTS)[None, None, :]
    back_tab = sorted_first[None, :] + in_expert_first - local_first
    back_chunk = jnp.sum(jnp.where(is_e_loc, back_tab[:, None, :], 0), axis=2) + lc[None, :]
    back_rows = jnp.where(used_loc, back_chunk, 0) * MOE_CHUNK

    ys = _experts(xs, src_rows.reshape(n_tiles, cpt), tile_expert, n_used.reshape(1),
                  w_gu, w_down, layer, tm=tm_e, tf=512)
    return _combine(ys, back_rows.astype(jnp.int32), x, meta, gate, seq=seq)


def kernel(x, c, ada_w, ada_b, norm1_g, norm2_g, at_w_in, at_q_norm, at_k_norm, at_lam, at_subln, at_w_out, hg_w_in, hg_lower_bounds, hg_norm_g, hg_w_out, lru_w_in, lru_conv_w, lru_conv_b, lru_w_a, lru_b_a, lru_w_x, lru_b_x, lru_lambda, lru_w_out, ff_w_gu, ff_w_down, moe_router, moe_w_gu, moe_w_down):
    batch, seq, d = x.shape
    depth = ada_w.shape[0]
    t = batch * seq
    mods = _mods(c, ada_w, ada_b)
    ff_w_gu, ff_w_down = ff_w_gu.astype(BF16), ff_w_down.astype(BF16)
    moe_w_gu, moe_w_down = moe_w_gu.astype(BF16), moe_w_down.astype(BF16)
    xr = x.reshape(t, d)
    for i in range(depth):
        sh1, sc1, g1, sh2, sc2, g2 = [mods[i, :, k * d:(k + 1) * d] for k in range(6)]
        kind, j = i % N_MIXERS, i // N_MIXERS
        if kind == 0:
            lambda_init = 0.8 - 0.6 * math.exp(-0.3 * i)
            xr = _diff_attention_layer(xr, norm1_g[i], sh1, sc1, g1, at_w_in[j], at_q_norm[j], at_k_norm[j],
                                       at_lam[j], at_subln[j], at_w_out[j], lambda_init, batch=batch, seq=seq)
        elif kind == 1:
            xr = _hgrn_layer(xr, norm1_g[i], sh1, sc1, g1, hg_w_in[j], hg_lower_bounds, hg_norm_g[j],
                             hg_w_out[j], i, batch=batch, seq=seq)
        else:
            xr = _rglru_layer(xr, norm1_g[i], sh1, sc1, g1, lru_w_in[j], lru_conv_w[j], lru_conv_b[j],
                              lru_w_a[j], lru_b_a[j], lru_w_x[j], lru_b_x[j], lru_lambda[j], lru_w_out[j],
                              batch=batch, seq=seq)
        m = i // 2
        if i % 2 == 0:
            xr = _dense_ffn(xr, norm2_g[i], sh2, sc2, g2, ff_w_gu, ff_w_down, m,
                            seq=seq, tm=min(512, seq), tf=ff_w_down.shape[1])
        else:
            xr = _moe_ffn(xr, norm2_g[i], sh2, sc2, g2, moe_router[m], moe_w_gu, moe_w_down, m, seq=seq)
    return xr.reshape(batch, seq, d)
```

```python
import functools
import math

import numpy as np
import jax
import jax.numpy as jnp
from jax import lax
from jax.experimental import pallas as pl
from jax.experimental.pallas import tpu as pltpu

F32 = jnp.float32
BF16 = jnp.bfloat16
HIGHEST = lax.Precision.HIGHEST

EPS = 1e-6
LANES = 128
VMEM_LIMIT_BYTES = 56 * 2**20

N_MIXERS = 3
DA_HEADS = 8
DA_HEAD_DIM = 64
DA_V_DIM = 2 * DA_HEAD_DIM
ROPE_THETA = 10000.0
HG_HEADS = 8
HG_CHUNK = 64
HG_SUB = 16
LRU_BLOCK_W = 128
CONV_WIDTH = 4
LRU_C = 8.0
N_EXPERTS = 8
TOP_K = 2


def _params(*sem):
    return pltpu.CompilerParams(dimension_semantics=sem, vmem_limit_bytes=VMEM_LIMIT_BYTES)


def _sigmoid(x):
    return 1.0 / (1.0 + jnp.exp(-x))


def _silu(x):
    return x * _sigmoid(x)


def _sigmoid_tanh(x):
    return 0.5 * jnp.tanh(0.5 * x) + 0.5


def _modulated(x, g, shift, scale):
    ms = jnp.mean(x * x, axis=-1, keepdims=True)
    return x * lax.rsqrt(ms + EPS) * g * (1.0 + scale) + shift


def _mods_kernel(c_ref, w_ref, b_ref, o_ref):
    c = c_ref[...]
    o_ref[...] = jnp.dot(_silu(c), w_ref[...], preferred_element_type=F32, precision=HIGHEST) + b_ref[...]


def _mods(c, ada_w, ada_b):
    depth, d, n = ada_w.shape
    b = c.shape[0]
    tn = 2048
    return pl.pallas_call(
        _mods_kernel,
        grid=(depth, n // tn),
        in_specs=[
            pl.BlockSpec((b, d), lambda l, j: (0, 0)),
            pl.BlockSpec((None, d, tn), lambda l, j: (l, 0, j)),
            pl.BlockSpec((None, 1, tn), lambda l, j: (l, 0, j)),
        ],
        out_specs=pl.BlockSpec((None, b, tn), lambda l, j: (l, 0, j)),
        out_shape=jax.ShapeDtypeStruct((depth, b, n), F32),
        compiler_params=_params("parallel", "parallel"),
        name="adaln_mods",
    )(c, ada_w, ada_b.reshape(depth, 1, n))


class _RowVec:
    def __init__(self, vec, tm, seq):
        b, d = vec.shape
        assert seq % tm == 0
        tiles_per_batch = seq // tm
        self.array = vec.reshape(b, 1, d)
        self.spec = pl.BlockSpec((None, 1, d), lambda i, *_: (i // tiles_per_batch, 0, 0))


def _const_spec(shape):
    nd = len(shape)
    return pl.BlockSpec(shape, lambda *_: (0,) * nd)


def _nm_matmul_kernel(x_ref, g_ref, sh_ref, sc_ref, w_ref, o_ref, h_ref):
    @pl.when(pl.program_id(1) == 0)
    def _():
        h_ref[...] = _modulated(x_ref[...], g_ref[...], sh_ref[...], sc_ref[...]).astype(BF16)

    o_ref[...] = jnp.dot(h_ref[...], w_ref[...], preferred_element_type=F32).astype(o_ref.dtype)


def _nm_matmul(x, g, shift, scale, w, *, seq, tm, tn, out_dtype):
    t, d = x.shape
    n = w.shape[1]
    assert t % tm == 0 and n % tn == 0
    sh = _RowVec(shift, tm, seq)
    sc = _RowVec(scale, tm, seq)
    weight_buffers = pl.Buffered(1) if n == tn else pl.Buffered(2)
    return pl.pallas_call(
        _nm_matmul_kernel,
        grid=(t // tm, n // tn),
        in_specs=[
            pl.BlockSpec((tm, d), lambda i, j: (i, 0)),
            _const_spec((1, d)),
            sh.spec, sc.spec,
            pl.BlockSpec((d, tn), lambda i, j: (0, j), pipeline_mode=weight_buffers),
        ],
        out_specs=pl.BlockSpec((tm, tn), lambda i, j: (i, j)),
        out_shape=jax.ShapeDtypeStruct((t, n), out_dtype),
        scratch_shapes=[pltpu.VMEM((tm, d), BF16)],
        compiler_params=_params("parallel", "arbitrary"),
        name="modulate_matmul",
    )(x, g.reshape(1, d), sh.array, sc.array, w)


def _out_proj_kernel(*refs, prologue, n_in):
    ins = refs[:n_in]
    w_ref, x_ref, gate_ref, o_ref = refs[n_in:]
    a = prologue(*ins)
    mix = jnp.dot(a, w_ref[...], preferred_element_type=F32)
    o_ref[...] = x_ref[...] + gate_ref[...] * mix


def _out_proj(prologue, ins, in_specs, w, x, gate, *, seq, tm):
    t, d = x.shape
    k = w.shape[0]
    gv = _RowVec(gate, tm, seq)
    return pl.pallas_call(
        functools.partial(_out_proj_kernel, prologue=prologue, n_in=len(ins)),
        grid=(t // tm,),
        in_specs=list(in_specs) + [
            _const_spec((k, d)),
            pl.BlockSpec((tm, d), lambda i: (i, 0)),
            gv.spec,
        ],
        out_specs=pl.BlockSpec((tm, d), lambda i: (i, 0)),
        out_shape=jax.ShapeDtypeStruct((t, d), F32),
        compiler_params=_params("parallel"),
        name="out_proj_residual",
    )(*ins, w, x, gv.array)


QK_PREP_WIDTH = 2 * LANES


def _rope_constants(seq):
    half = DA_HEAD_DIM // 2
    inv = 1.0 / (ROPE_THETA ** (jnp.arange(0, DA_HEAD_DIM, 2, dtype=F32) / DA_HEAD_DIM))
    ang = jnp.arange(seq, dtype=F32)[:, None] * inv[None, :]
    reps = QK_PREP_WIDTH // half
    cos = jnp.tile(jnp.cos(ang), (1, reps))
    sin = jnp.tile(jnp.sin(ang), (1, reps))
    lane = np.arange(QK_PREP_WIDTH)
    first = (lane % DA_HEAD_DIM) < half
    rot = np.zeros((QK_PREP_WIDTH, QK_PREP_WIDTH), np.float32)
    rot[(lane + half)[first], lane[first]] = -1.0
    rot[(lane - half)[~first], lane[~first]] = 1.0
    group = (lane[:, None] // DA_HEAD_DIM == lane[None, :] // DA_HEAD_DIM).astype(np.float32)
    return cos, sin, jnp.asarray(rot, BF16), jnp.asarray(group, BF16)


def _attn_in_kernel(x_ref, g_ref, sh_ref, sc_ref, w_ref, gain_ref, cos_ref, sin_ref, rot_ref, grp_ref,
                    o_ref, h_ref):
    j = pl.program_id(1)

    @pl.when(j == 0)
    def _():
        h_ref[...] = _modulated(x_ref[...], g_ref[...], sh_ref[...], sc_ref[...]).astype(BF16)

    y_all = jnp.dot(h_ref[...], w_ref[...], preferred_element_type=F32)

    @pl.when(j == 2)
    def _():
        o_ref[...] = y_all.astype(BF16)

    @pl.when(j < 2)
    def _():
        out_scale = jnp.where(j == 0, DA_HEAD_DIM ** -0.5 * math.log2(math.e), 1.0).astype(F32)
        cos = cos_ref[...]
        sin = sin_ref[...]
        gain = gain_ref[...]
        for h in range(y_all.shape[1] // QK_PREP_WIDTH):
            sl = slice(h * QK_PREP_WIDTH, (h + 1) * QK_PREP_WIDTH)
            y = y_all[:, sl]
            ss = jnp.dot((y * y).astype(BF16), grp_ref[...], preferred_element_type=F32)
            n = y * lax.rsqrt(ss * (1.0 / DA_HEAD_DIM) + EPS) * gain
            r = jnp.dot(n.astype(BF16), rot_ref[...], preferred_element_type=F32)
            o_ref[:, sl] = ((n * cos + r * sin) * out_scale).astype(BF16)


def _attn_in_proj(x, g, shift, scale, w, q_gain, k_gain, cos, sin, rot, grp, *, seq, tm):
    t, d = x.shape
    width = DA_HEADS * LANES
    assert w.shape[1] == 3 * width
    reps = QK_PREP_WIDTH // DA_HEAD_DIM
    gains = jnp.stack([jnp.tile(q_gain, reps), jnp.tile(k_gain, reps)]).reshape(2, 1, QK_PREP_WIDTH).astype(F32)
    tiles_per_seq = seq // tm
    sh = _RowVec(shift, tm, seq)
    sc = _RowVec(scale, tm, seq)
    return pl.pallas_call(
        _attn_in_kernel,
        grid=(t // tm, 3),
        in_specs=[
            pl.BlockSpec((tm, d), lambda i, j: (i, 0)),
            _const_spec((1, d)),
            sh.spec, sc.spec,
            pl.BlockSpec((d, width), lambda i, j: (0, j)),
            pl.BlockSpec((None, 1, QK_PREP_WIDTH), lambda i, j: (jnp.minimum(j, 1), 0, 0)),
            pl.BlockSpec((tm, QK_PREP_WIDTH), lambda i, j: (i % tiles_per_seq, 0)),
            pl.BlockSpec((tm, QK_PREP_WIDTH), lambda i, j: (i % tiles_per_seq, 0)),
            _const_spec((QK_PREP_WIDTH, QK_PREP_WIDTH)),
            _const_spec((QK_PREP_WIDTH, QK_PREP_WIDTH)),
        ],
        out_specs=pl.BlockSpec((None, tm, width), lambda i, j: (j, i, 0)),
        out_shape=jax.ShapeDtypeStruct((3, t, width), BF16),
        scratch_shapes=[pltpu.VMEM((tm, d), BF16)],
        compiler_params=_params("parallel", "arbitrary"),
        name="attn_in_proj",
    )(x, g.reshape(1, d), sh.array, sc.array, w, gains, cos, sin, rot, grp)


def _flash_kernel(q_ref, k_ref, v_ref, lam_ref, subln_ref, o_ref,
                  vt_ref, kmax_ref, bound_ref, mrun_ref, acc_ref, den_ref, *, tq, tk, lambda_init):
    seq = k_ref.shape[0]
    n_kv = seq // tk
    lane = lax.broadcasted_iota(jnp.int32, (LANES, LANES), 0)
    first_half = jnp.where(lane < DA_HEAD_DIM, 1.0, 0.0).astype(BF16)
    second_half = jnp.where(lane >= DA_HEAD_DIM, 1.0, 0.0).astype(BF16)

    @pl.when(pl.program_id(2) == 0)
    def _():
        def prep(j, carry):
            n1, n2 = carry
            r0 = pl.multiple_of(j * tk, tk)
            vt_ref[:, pl.ds(r0, tk)] = v_ref[pl.ds(r0, tk), :].astype(F32).T.astype(BF16)
            kf = k_ref[pl.ds(r0, tk), :].astype(F32)
            sq = (kf * kf).astype(BF16)
            s1 = jnp.dot(sq, first_half, preferred_element_type=F32)
            s2 = jnp.dot(sq, second_half, preferred_element_type=F32)
            return (jnp.maximum(n1, jnp.max(s1, axis=0, keepdims=True)),
                    jnp.maximum(n2, jnp.max(s2, axis=0, keepdims=True)))

        zero = jnp.zeros((1, LANES), F32)
        n1, n2 = lax.fori_loop(0, n_kv, prep, (zero, zero))
        kmax_ref[0:1, :] = n1
        kmax_ref[1:2, :] = n2

    q = q_ref[...]
    qlane = lax.broadcasted_iota(jnp.int32, q.shape, 1)
    zero = jnp.zeros_like(q)
    qq = jnp.concatenate([jnp.where(qlane < DA_HEAD_DIM, q, zero),
                          jnp.where(qlane >= DA_HEAD_DIM, q, zero)], axis=0)
    qf = qq.astype(F32)
    nt = (((1,), (1,)), ((), ()))
    qn = lax.dot_general(jnp.ones((8, LANES), BF16), (qf * qf).astype(BF16), nt,
                         preferred_element_type=F32)[0:1]
    col = lax.broadcasted_iota(jnp.int32, qn.shape, 1)
    kn = jnp.where(col < tq, kmax_ref[0:1, 0:1], kmax_ref[1:2, 0:1])
    bound = jnp.sqrt(qn * kn) * 1.01
    bound_ref[...] = bound
    acc_ref[...] = jnp.zeros(acc_ref.shape, F32)
    den_ref[...] = jnp.zeros(den_ref.shape, F32)
    safe = jnp.max(bound) <= 60.0

    def scores(j):
        r0 = pl.multiple_of(j * tk, tk)
        st = lax.dot_general(k_ref[pl.ds(r0, tk), :], qq, nt, preferred_element_type=F32)
        return st, vt_ref[:, pl.ds(r0, tk)]

    @pl.when(safe)
    def _():
        def body(j, carry):
            st, vt = scores(j)
            p = jnp.exp2(st - bound_ref[...])
            den_ref[...] += jnp.sum(p, axis=0, keepdims=True)
            acc_ref[...] += jnp.dot(vt, p.astype(BF16), preferred_element_type=F32)
            return carry

        lax.fori_loop(0, n_kv, body, 0, unroll=2)

    @pl.when(jnp.logical_not(safe))
    def _():
        mrun_ref[...] = jnp.full(mrun_ref.shape, -jnp.inf, F32)

        def body(j, carry):
            st, vt = scores(j)
            m_old = mrun_ref[...]
            m_new = jnp.maximum(m_old, jnp.max(st, axis=0, keepdims=True))
            alpha = jnp.exp2(m_old - m_new)
            p = jnp.exp2(st - m_new)
            den_ref[...] = alpha * den_ref[...] + jnp.sum(p, axis=0, keepdims=True)
            acc_ref[...] = alpha * acc_ref[...] + jnp.dot(vt, p.astype(BF16), preferred_element_type=F32)
            mrun_ref[...] = m_new
            return carry

        lax.fori_loop(0, n_kv, body, 0)

    lf = lam_ref[...]
    lam = (jnp.exp(jnp.sum(lf[0:1] * lf[1:2], axis=1, keepdims=True))
           - jnp.exp(jnp.sum(lf[2:3] * lf[3:4], axis=1, keepdims=True)) + lambda_init)
    inv = 1.0 / den_ref[...]
    o1 = acc_ref[:, 0:tq] * inv[:, 0:tq]
    o2 = acc_ref[:, tq:2 * tq] * inv[:, tq:2 * tq]
    ot = o1 - lam * o2
    ms = jnp.mean(ot * ot, axis=0, keepdims=True)
    o = (ot * lax.rsqrt(ms + EPS)).T * (subln_ref[...] * (1.0 - lambda_init))
    o_ref[...] = o.astype(BF16)


def _flash(qkv, lam_vecs, subln_g, *, batch, seq, lambda_init, tq, tk):
    t = batch * seq
    width = DA_HEADS * LANES
    q_tiles = seq // tq
    return pl.pallas_call(
        functools.partial(_flash_kernel, tq=tq, tk=tk, lambda_init=lambda_init),
        grid=(batch, DA_HEADS, q_tiles),
        in_specs=[
            pl.BlockSpec((None, tq, LANES), lambda b, h, i: (0, b * q_tiles + i, h)),
            pl.BlockSpec((None, seq, LANES), lambda b, h, i: (1, b, h)),
            pl.BlockSpec((None, seq, LANES), lambda b, h, i: (2, b, h)),
            _const_spec((4, DA_HEAD_DIM)),
            _const_spec((1, DA_V_DIM)),
        ],
        out_specs=pl.BlockSpec((tq, LANES), lambda b, h, i: (b * q_tiles + i, h)),
        out_shape=jax.ShapeDtypeStruct((t, width), BF16),
        scratch_shapes=[
            pltpu.VMEM((LANES, seq), BF16),
            pltpu.VMEM((8, LANES), F32),
            pltpu.VMEM((1, 2 * tq), F32),
            pltpu.VMEM((1, 2 * tq), F32),
            pltpu.VMEM((LANES, 2 * tq), F32),
            pltpu.VMEM((1, 2 * tq), F32),
        ],
        compiler_params=_params("parallel", "parallel", "arbitrary"),
        name="diff_flash_attention",
    )(qkv, qkv, qkv, lam_vecs.astype(F32), subln_g.reshape(1, DA_V_DIM).astype(F32))


def _identity_prologue(o_ref):
    return o_ref[...]


def _diff_attention_layer(x, g, shift, scale, gate, w_in, q_norm, k_norm, lam_vecs, subln_g, w_out,
                          lambda_init, *, batch, seq):
    cos, sin, rot, grp = _rope_constants(seq)
    qkv = _attn_in_proj(x, g, shift, scale, w_in.astype(BF16), q_norm, k_norm, cos, sin, rot, grp,
                        seq=seq, tm=min(1024, seq))
    o = _flash(qkv, lam_vecs, subln_g, batch=batch, seq=seq, lambda_init=lambda_init,
               tq=min(1024, seq), tk=min(512, seq))
    tm = min(512, seq)
    return _out_proj(_identity_prologue, [o], [pl.BlockSpec((tm, o.shape[1]), lambda i: (i, 0))],
                     w_out.astype(BF16), x, gate, seq=seq, tm=tm)


def _hgrn_scan_kernel(q_ref, f_ref, v_ref, lb_ref, o_ref, st_ref, *, reverse, layer):
    c = HG_CHUNK
    n_chunks = q_ref.shape[0] // c

    @pl.when(pl.program_id(1) == 0)
    def _():
        st_ref[...] = jnp.zeros(st_ref.shape, F32)

    lbw = lb_ref[...]
    e = jnp.exp(lbw - jnp.max(lbw, axis=0, keepdims=True))
    lb = jnp.sum(e[1:layer + 1], axis=0, keepdims=True) / jnp.sum(e, axis=0, keepdims=True)

    row = lax.broadcasted_iota(jnp.int32, (c, c), 0)
    col = lax.broadcasted_iota(jnp.int32, (c, c), 1)
    tri = (col >= row) if reverse else (col <= row)
    tri = tri.astype(F32)
    n_sub = c // HG_SUB
    half = HG_SUB // 2
    col8 = lax.broadcasted_iota(jnp.int32, (half, c), 1)
    row8 = lax.broadcasted_iota(jnp.int32, (half, c), 0)

    def key_rows(i, r):
        t0 = i * HG_SUB + r * half
        return range(t0, (i + 1) * HG_SUB) if reverse else range(i * HG_SUB, t0 + half)

    def placement(i, r, s):
        t0 = i * HG_SUB + r * half
        reached = (row8 + t0 <= s) if reverse else (row8 + t0 >= s)
        return jnp.logical_and(col8 == s, reached)

    place = {(i, r, s): placement(i, r, s) for i in range(n_sub) for r in range(2) for s in key_rows(i, r)}

    def chunk(ci, carry):
        cidx = (n_chunks - 1 - ci) if reverse else ci
        r0 = pl.multiple_of(cidx * c, c)
        qs = _silu(q_ref[pl.ds(r0, c), :])
        forget = lb + (1.0 - lb) * _sigmoid(f_ref[pl.ds(r0, c), :])
        kk = 1.0 - forget
        vb = v_ref[pl.ds(r0, c), :].astype(BF16)
        b = jnp.dot(tri, jnp.log2(forget), preferred_element_type=F32, precision=HIGHEST)
        b_edge = b[0:1] if reverse else b[c - 1:c]
        q_in = (qs * jnp.exp2(b)).astype(BF16)
        k_out = (kk * jnp.exp2(b_edge - b)).astype(BF16)
        dec = jnp.exp2(b_edge)
        key_shift = b - jnp.log2(kk)

        pieces = [[jnp.zeros((half, c), F32) for _ in range(2 * n_sub)] for _ in range(HG_HEADS)]
        for i in range(n_sub):
            for r in range(2):
                t0 = i * HG_SUB + r * half
                q8 = qs[t0:t0 + half]
                b8 = b[t0:t0 + half]
                for s in key_rows(i, r):
                    term = q8 * jnp.exp2(b8 - key_shift[s:s + 1])
                    for h in range(HG_HEADS):
                        a_col = jnp.sum(term[:, h * LANES:(h + 1) * LANES], axis=1, keepdims=True)
                        pieces[h][2 * i + r] = jnp.where(place[i, r, s], a_col, pieces[h][2 * i + r])

        for h in range(HG_HEADS):
            sl = slice(h * LANES, (h + 1) * LANES)
            blocks = []
            for i in range(n_sub):
                rows = slice(i * HG_SUB, (i + 1) * HG_SUB)
                passed = slice((i + 1) * HG_SUB, c) if reverse else slice(0, i * HG_SUB)
                n_passed = passed.stop - passed.start
                if n_passed == 0:
                    blocks.append(jnp.zeros((HG_SUB, c), F32))
                    continue
                ref_row = (i + 1) * HG_SUB if reverse else i * HG_SUB - 1
                b_ref_row = b[ref_row:ref_row + 1, sl]
                q_t = (qs[rows, sl] * jnp.exp2(b[rows, sl] - b_ref_row)).astype(BF16)
                k_p = (kk[passed, sl] * jnp.exp2(b_ref_row - b[passed, sl])).astype(BF16)
                rest = jnp.zeros((c - n_passed, LANES), BF16)
                k_t = jnp.concatenate([rest, k_p] if reverse else [k_p, rest], axis=0)
                blocks.append(lax.dot_general(q_t, k_t, (((1,), (1,)), ((), ())),
                                              preferred_element_type=F32))
            a = jnp.concatenate(pieces[h], axis=0) + jnp.concatenate(blocks, axis=0)
            st = st_ref[h]
            o_h = jnp.dot(a.astype(BF16), vb[:, sl], preferred_element_type=F32)
            o_h = o_h + lax.dot_general(q_in[:, sl], st.astype(BF16), (((1,), (1,)), ((), ())),
                                        preferred_element_type=F32)
            o_ref[pl.ds(r0, c), sl] = o_h.astype(o_ref.dtype)
            upd = lax.dot_general(vb[:, sl], k_out[:, sl], (((0,), (0,)), ((), ())),
                                  preferred_element_type=F32)
            st_ref[h] = st * dec[:, sl] + upd
        return carry

    lax.fori_loop(0, n_chunks, chunk, 0, unroll=2)


def _hgrn_scan(proj, lower_bounds, *, batch, seq, layer, reverse, tt):
    t = batch * seq
    width = HG_HEADS * LANES
    nt = seq // tt
    depth = lower_bounds.shape[1]
    d_idx = 1 if reverse else 0

    def rows(b, i):
        return b * nt + ((nt - 1 - i) if reverse else i)

    return pl.pallas_call(
        functools.partial(_hgrn_scan_kernel, reverse=reverse, layer=layer),
        grid=(batch, nt),
        in_specs=[
            pl.BlockSpec((tt, width), lambda b, i: (rows(b, i), 0)),
            pl.BlockSpec((tt, width), lambda b, i: (rows(b, i), 1 + d_idx)),
            pl.BlockSpec((tt, width), lambda b, i: (rows(b, i), 3)),
            pl.BlockSpec((None, depth, width), lambda b, i: (d_idx, 0, 0)),
        ],
        out_specs=pl.BlockSpec((tt, width), lambda b, i: (rows(b, i), 0)),
        out_shape=jax.ShapeDtypeStruct((t, width), BF16),
        scratch_shapes=[pltpu.VMEM((HG_HEADS, LANES, LANES), F32)],
        compiler_params=_params("parallel", "arbitrary"),
        name="hgrn2_scan_bwd" if reverse else "hgrn2_scan_fwd",
    )(proj, proj, proj, lower_bounds.astype(F32))


def _hgrn_prologue(of_ref, ob_ref, g_ref, ng_ref):
    ng = ng_ref[...]
    outs = []
    for h in range(HG_HEADS):
        sl = slice(h * LANES, (h + 1) * LANES)
        o = of_ref[:, sl].astype(F32) + ob_ref[:, sl].astype(F32)
        ms = jnp.mean(o * o, axis=-1, keepdims=True)
        outs.append((o * lax.rsqrt(ms + EPS) * ng * _silu(g_ref[:, sl])).astype(BF16))
    return jnp.concatenate(outs, axis=1)


def _hgrn_layer(x, g, shift, scale, gate, w_in, lower_bounds, norm_g, w_out, layer, *, batch, seq):
    width = HG_HEADS * LANES
    proj = _nm_matmul(x, g, shift, scale, w_in.astype(BF16), seq=seq,
                      tm=min(512, seq), tn=w_in.shape[1], out_dtype=F32)
    tt = min(512, seq)
    o_f = _hgrn_scan(proj, lower_bounds, batch=batch, seq=seq, layer=layer, reverse=False, tt=tt)
    o_b = _hgrn_scan(proj, lower_bounds, batch=batch, seq=seq, layer=layer, reverse=True, tt=tt)
    tm = min(512, seq)
    ins = [o_f, o_b, proj, norm_g.reshape(1, LANES).astype(F32)]
    specs = [
        pl.BlockSpec((tm, width), lambda i: (i, 0)),
        pl.BlockSpec((tm, width), lambda i: (i, 0)),
        pl.BlockSpec((tm, width), lambda i: (i, 4)),
        _const_spec((1, LANES)),
    ]
    return _out_proj(_hgrn_prologue, ins, specs, w_out.astype(BF16), x, gate, seq=seq, tm=tm)


def _lru_scan_kernel(prev_ref, main_ref, next_ref, cw_ref, cb_ref, wg_ref, ba_ref, bx_ref, lam_ref,
                     o_ref, a_ref, u_ref, h_ref, *, batch, nt):
    d = pl.program_id(0)
    i = pl.program_id(1)
    ti = i + d * (nt - 1 - 2 * i)
    rows, width = main_ref.shape
    tt = rows // batch

    @pl.when(i == 0)
    def _():
        h_ref[...] = jnp.zeros(h_ref.shape, F32)

    prev = jnp.where(ti == 0, 0.0, prev_ref[...])
    nxt = jnp.where(ti == nt - 1, 0.0, next_ref[...])
    ext = jnp.concatenate([prev, main_ref[...], nxt], axis=0)
    cw = cw_ref[...]
    xc = cb_ref[...] + cw[0:1] * ext[0:rows]
    for j in range(1, CONV_WIDTH):
        xc = xc + cw[j:j + 1] * ext[j * batch:j * batch + rows]

    neg_softplus = -LRU_C * (jnp.maximum(-lam_ref[...], 0.0) + jnp.log1p(jnp.exp(-jnp.abs(lam_ref[...]))))
    for n in range(width // LRU_BLOCK_W):
        sl = slice(n * LRU_BLOCK_W, (n + 1) * LRU_BLOCK_W)
        xb = xc[:, sl]
        gates = jnp.dot(xb.astype(BF16), wg_ref[n], preferred_element_type=F32)
        r = _sigmoid_tanh(gates[:, :LRU_BLOCK_W] + ba_ref[:, sl])
        ig = _sigmoid_tanh(gates[:, LRU_BLOCK_W:] + bx_ref[:, sl])
        a = jnp.exp(r * neg_softplus[:, sl])
        a_ref[:, sl] = a
        t = 1.0 - a * a
        u_ref[:, sl] = (t * lax.rsqrt(jnp.maximum(t, 1e-30))) * (ig * xb)

    def step(t, h):
        tl = t + d * (tt - 1 - 2 * t)
        r0 = pl.multiple_of(tl * batch, batch)
        h = a_ref[pl.ds(r0, batch), :] * h + u_ref[pl.ds(r0, batch), :]
        o_ref[pl.ds(r0, batch), :] = h
        return h

    h_ref[...] = lax.fori_loop(0, tt, step, h_ref[...])


def _lru_scan(proj, conv_w, conv_b, w_a, b_a, w_x, b_x, lam, *, batch, seq, tt):
    t = batch * seq
    width = conv_w.shape[1]
    n_blocks = width // LRU_BLOCK_W
    nt = seq // tt
    rows = tt * batch
    assert tt % 2 == 0

    def tile(d, i):
        return i + d * (nt - 1 - 2 * i)

    wg = jnp.concatenate([w_a, w_x], axis=-1).astype(BF16)
    vec = lambda a: a.reshape(2, 1, width).astype(F32)
    return pl.pallas_call(
        functools.partial(_lru_scan_kernel, batch=batch, nt=nt),
        grid=(2, nt),
        in_specs=[
            pl.BlockSpec((batch, width), lambda d, i: (jnp.maximum(tile(d, i) * tt - 1, 0), 1)),
            pl.BlockSpec((rows, width), lambda d, i: (tile(d, i), 1)),
            pl.BlockSpec((2 * batch, width),
                         lambda d, i: (jnp.minimum((tile(d, i) + 1) * (tt // 2), seq // 2 - 1), 1)),
            _const_spec((CONV_WIDTH, width)),
            _const_spec((1, width)),
            pl.BlockSpec((None, n_blocks, LRU_BLOCK_W, 2 * LRU_BLOCK_W), lambda d, i: (d, 0, 0, 0)),
            pl.BlockSpec((None, 1, width), lambda d, i: (d, 0, 0)),
            pl.BlockSpec((None, 1, width), lambda d, i: (d, 0, 0)),
            pl.BlockSpec((None, 1, width), lambda d, i: (d, 0, 0)),
        ],
        out_specs=pl.BlockSpec((None, rows, width), lambda d, i: (d, tile(d, i), 0)),
        out_shape=jax.ShapeDtypeStruct((2, t, width), F32),
        scratch_shapes=[
            pltpu.VMEM((rows, width), F32),
            pltpu.VMEM((rows, width), F32),
            pltpu.VMEM((batch, width), F32),
        ],
        compiler_params=_params("arbitrary", "arbitrary"),
        name="rglru_scan",
    )(proj, proj, proj, conv_w.astype(F32), conv_b.reshape(1, width).astype(F32), wg,
      vec(b_a), vec(b_x), vec(lam))


def _gelu_tanh(x):
    return 0.5 * x * (1.0 + jnp.tanh(math.sqrt(2.0 / math.pi) * (x + 0.044715 * (x * x * x))))


def _lru_in_kernel(x_ref, g_ref, sh_ref, sc_ref, w_ref, o_ref, xm_ref, h_ref, *, batch, ts):
    n_lane_blocks = x_ref.shape[2] // LANES
    for b in range(batch):
        xm = _modulated(x_ref[b], g_ref[...], sh_ref[b:b + 1, :], sc_ref[b:b + 1, :])
        for c in range(n_lane_blocks):
            xm_ref[c, b * ts:(b + 1) * ts, :] = xm[:, c * LANES:(c + 1) * LANES]
    for s in range(ts):
        for c in range(n_lane_blocks):
            h_ref[s * batch:(s + 1) * batch, c * LANES:(c + 1) * LANES] = \
                xm_ref[c, pl.ds(s, batch, stride=ts), :]
    o_ref[...] = jnp.dot(h_ref[...].astype(BF16), w_ref[...], preferred_element_type=F32)


def _lru_in_proj(x, g, shift, scale, w, *, batch, seq, ts):
    t, d = x.shape
    n = w.shape[1]
    rows = ts * batch
    return pl.pallas_call(
        functools.partial(_lru_in_kernel, batch=batch, ts=ts),
        grid=(seq // ts,),
        in_specs=[
            pl.BlockSpec((batch, ts, d), lambda i: (0, i, 0)),
            _const_spec((1, d)),
            _const_spec((batch, d)),
            _const_spec((batch, d)),
            _const_spec((d, n)),
        ],
        out_specs=pl.BlockSpec((rows, n), lambda i: (i, 0)),
        out_shape=jax.ShapeDtypeStruct((t, n), F32),
        scratch_shapes=[pltpu.VMEM((d // LANES, rows, LANES), F32), pltpu.VMEM((rows, d), F32)],
        compiler_params=_params("parallel"),
        name="rglru_in_proj",
    )(x.reshape(batch, seq, d), g.reshape(1, d), shift, scale, w)


def _lru_out_kernel(hf_ref, hb_ref, y_ref, w_ref, x_ref, gate_ref, o_ref, mix_ref, *, batch, ts):
    a = ((hf_ref[...] + hb_ref[...]) * _gelu_tanh(y_ref[...])).astype(BF16)
    mix = jnp.dot(a, w_ref[...], preferred_element_type=F32)
    n_lane_blocks = mix.shape[1] // LANES
    for c in range(n_lane_blocks):
        mix_ref[c] = mix[:, c * LANES:(c + 1) * LANES]
    for b in range(batch):
        for c in range(n_lane_blocks):
            sl = slice(c * LANES, (c + 1) * LANES)
            o_ref[b, :, sl] = x_ref[b, :, sl] + gate_ref[b:b + 1, sl] * mix_ref[c, pl.ds(b, ts, stride=batch), :]


def _lru_out_proj(h2, proj, w, x, gate, *, batch, seq, ts):
    t, d = x.shape
    width = w.shape[0]
    rows = ts * batch
    out = pl.pallas_call(
        functools.partial(_lru_out_kernel, batch=batch, ts=ts),
        grid=(seq // ts,),
        in_specs=[
            pl.BlockSpec((None, rows, width), lambda i: (0, i, 0)),
            pl.BlockSpec((None, rows, width), lambda i: (1, i, 0)),
            pl.BlockSpec((rows, width), lambda i: (i, 0)),
            _const_spec((width, d)),
            pl.BlockSpec((batch, ts, d), lambda i: (0, i, 0)),
            _const_spec((batch, d)),
        ],
        out_specs=pl.BlockSpec((batch, ts, d), lambda i: (0, i, 0)),
        out_shape=jax.ShapeDtypeStruct((batch, seq, d), F32),
        scratch_shapes=[pltpu.VMEM((d // LANES, rows, LANES), F32)],
        compiler_params=_params("parallel"),
        name="rglru_out_proj",
    )(h2, h2, proj, w, x.reshape(batch, seq, d), gate)
    return out.reshape(t, d)


def _rglru_layer(x, g, shift, scale, gate, w_in, conv_w, conv_b, w_a, b_a, w_x, b_x, lam, w_out,
                 *, batch, seq):
    width = conv_w.shape[1]
    ts = min(64, seq)
    proj = _lru_in_proj(x, g, shift, scale, w_in.astype(BF16), batch=batch, seq=seq, ts=ts)
    h2 = _lru_scan(proj, conv_w, conv_b, w_a, b_a, w_x, b_x, lam, batch=batch, seq=seq, tt=min(64, seq))
    return _lru_out_proj(h2, proj, w_out.astype(BF16), x, gate, batch=batch, seq=seq, ts=ts)


def _ffn_kernel(x_ref, g_ref, sh_ref, sc_ref, wg_ref, wu_ref, wd_ref, gate_ref, o_ref, h_ref, acc_ref):
    j = pl.program_id(1)

    @pl.when(j == 0)
    def _():
        h_ref[...] = _modulated(x_ref[...], g_ref[...], sh_ref[...], sc_ref[...]).astype(BF16)
        acc_ref[...] = jnp.zeros(acc_ref.shape, F32)

    h = h_ref[...]
    gt = jnp.dot(h, wg_ref[...], preferred_element_type=F32)
    up = jnp.dot(h, wu_ref[...], preferred_element_type=F32)
    act = (_silu(gt) * up).astype(BF16)
    acc_ref[...] += jnp.dot(act, wd_ref[...], preferred_element_type=F32)

    @pl.when(j == pl.num_programs(1) - 1)
    def _():
        o_ref[...] = x_ref[...] + gate_ref[...] * acc_ref[...]


def _dense_ffn(x, g, shift, scale, gate, w_gu, w_down, layer, *, seq, tm, tf):
    t, d = x.shape
    f = w_down.shape[1]
    assert f % tf == 0 and seq % tm == 0
    nf = f // tf
    sh = _RowVec(shift, tm, seq)
    sc = _RowVec(scale, tm, seq)
    gv = _RowVec(gate, tm, seq)
    weight_buffers = pl.Buffered(1) if nf == 1 else pl.Buffered(2)
    return pl.pallas_call(
        _ffn_kernel,
        grid=(t // tm, nf),
        in_specs=[
            pl.BlockSpec((tm, d), lambda i, j: (i, 0)),
            _const_spec((1, d)),
            sh.spec, sc.spec,
            pl.BlockSpec((None, d, tf), lambda i, j: (layer, 0, j), pipeline_mode=weight_buffers),
            pl.BlockSpec((None, d, tf), lambda i, j: (layer, 0, nf + j), pipeline_mode=weight_buffers),
            pl.BlockSpec((None, tf, d), lambda i, j: (layer, j, 0), pipeline_mode=weight_buffers),
            gv.spec,
        ],
        out_specs=pl.BlockSpec((tm, d), lambda i, j: (i, 0)),
        out_shape=jax.ShapeDtypeStruct((t, d), F32),
        scratch_shapes=[pltpu.VMEM((tm, d), BF16), pltpu.VMEM((tm, d), F32)],
        compiler_params=_params("parallel", "arbitrary"),
        name="dense_swiglu",
    )(x, g.reshape(1, d), sh.array, sc.array, w_gu, w_gu, w_down, gv.array)


MOE_BLOCK = 512
MOE_CHUNK = 16
MOE_TILE = 1024


def _route_kernel(x_ref, g_ref, sh_ref, sc_ref, rhi_ref, rlo_ref, xs_ref, meta_ref, cnt_ref):
    tb = x_ref.shape[0]
    local_rows = xs_ref.shape[0]
    hf = _modulated(x_ref[...], g_ref[...], sh_ref[...], sc_ref[...])
    h_hi = hf.astype(BF16)
    h_lo = (hf - h_hi.astype(F32)).astype(BF16)
    logits = (jnp.dot(h_hi, rhi_ref[...], preferred_element_type=F32)
              + jnp.dot(h_lo, rhi_ref[...], preferred_element_type=F32)
              + jnp.dot(h_hi, rlo_ref[...], preferred_element_type=F32))
    lane = lax.broadcasted_iota(jnp.int32, logits.shape, 1).astype(F32)
    logits = jnp.where(lane < N_EXPERTS, logits, -jnp.inf)
    m1 = jnp.max(logits, axis=1, keepdims=True)
    e1 = jnp.min(jnp.where(logits == m1, lane, float(LANES)), axis=1, keepdims=True)
    rest = jnp.where(lane == e1, -jnp.inf, logits)
    m2 = jnp.max(rest, axis=1, keepdims=True)
    e2 = jnp.min(jnp.where(rest == m2, lane, float(LANES)), axis=1, keepdims=True)
    gate2 = 1.0 / (1.0 + jnp.exp(m1 - m2))
    gate1 = 1.0 - gate2
    hot1 = (lane == e1).astype(F32)
    hot2 = (lane == e2).astype(F32)
    sel = hot1 + hot2
    row = lax.broadcasted_iota(jnp.int32, (tb, tb), 0)
    col = lax.broadcasted_iota(jnp.int32, (tb, tb), 1)
    before = (col < row).astype(BF16)
    seen = jnp.dot(before, sel.astype(BF16), preferred_element_type=F32)
    chunks = jnp.floor((jnp.sum(sel, axis=0, keepdims=True) + (MOE_CHUNK - 1.0)) * (1.0 / MOE_CHUNK))
    cnt_ref[...] = chunks
    lrow = lax.broadcasted_iota(jnp.int32, (LANES, LANES), 0)
    lcol = lax.broadcasted_iota(jnp.int32, (LANES, LANES), 1)
    lower_experts = (lrow < lcol).astype(BF16)
    first_chunk = jnp.dot(jnp.broadcast_to(chunks, (8, LANES)).astype(BF16), lower_experts,
                          preferred_element_type=F32)[0:1]
    base = first_chunk * float(MOE_CHUNK) + seen
    ld1 = jnp.sum(hot1 * base, axis=1, keepdims=True)
    ld2 = jnp.sum(hot2 * base, axis=1, keepdims=True)
    meta = jnp.zeros(logits.shape, F32)
    for k, val in enumerate((e1, e2, gate1, gate2, ld1, ld2)):
        meta = jnp.where(lane == float(k), val, meta)
    meta_ref[...] = meta
    eye = row == col
    ld1_row = jnp.sum(jnp.where(eye, ld1, 0.0), axis=0, keepdims=True)
    ld2_row = jnp.sum(jnp.where(eye, ld2, 0.0), axis=0, keepdims=True)
    r_iota = lax.broadcasted_iota(jnp.int32, (local_rows, tb), 0).astype(F32)
    pick = jnp.where(r_iota == ld1_row, 1.0, jnp.where(r_iota == ld2_row, 1.0, 0.0)).astype(BF16)
    xs_ref[...] = jnp.dot(pick, h_hi, preferred_element_type=F32).astype(BF16)


def _route(x, g, shift, scale, router, *, seq):
    t, d = x.shape
    tb = min(MOE_BLOCK, seq)
    local_rows = TOP_K * tb + N_EXPERTS * MOE_CHUNK
    nb = t // tb
    sh = _RowVec(shift, tb, seq)
    sc = _RowVec(scale, tb, seq)
    router_pad = jnp.zeros((d, LANES), F32).at[:, :N_EXPERTS].set(router.astype(F32))
    router_hi = router_pad.astype(BF16)
    router_lo = (router_pad - router_hi.astype(F32)).astype(BF16)
    return pl.pallas_call(
        _route_kernel,
        grid=(nb,),
        in_specs=[
            pl.BlockSpec((tb, d), lambda i: (i, 0)),
            _const_spec((1, d)),
            sh.spec, sc.spec,
            _const_spec((d, LANES)),
            _const_spec((d, LANES)),
        ],
        out_specs=[
            pl.BlockSpec((local_rows, d), lambda i: (i, 0)),
            pl.BlockSpec((tb, LANES), lambda i: (i, 0)),
            pl.BlockSpec((None, 1, LANES), lambda i: (i, 0, 0)),
        ],
        out_shape=[
            jax.ShapeDtypeStruct((nb * local_rows, d), BF16),
            jax.ShapeDtypeStruct((t, LANES), F32),
            jax.ShapeDtypeStruct((nb, 1, LANES), F32),
        ],
        compiler_params=_params("parallel"),
        name="moe_route",
    )(x, g.reshape(1, d), sh.array, sc.array, router_hi, router_lo)


def _start_chunk_gather(row_smem, slot, src_hbm, dst_vmem, sem):
    n = dst_vmem.shape[1] // MOE_CHUNK

    def start(k, c):
        r0 = pl.multiple_of(row_smem[slot, k], MOE_CHUNK)
        pltpu.make_async_copy(src_hbm.at[pl.ds(r0, MOE_CHUNK)],
                              dst_vmem.at[slot, pl.ds(pl.multiple_of(k * MOE_CHUNK, MOE_CHUNK), MOE_CHUNK)],
                              sem.at[slot]).start()
        return c

    lax.fori_loop(0, n, start, 0, unroll=8)


def _wait_chunk_gather(slot, src_hbm, dst_vmem, sem):
    n = dst_vmem.shape[1]
    pltpu.make_async_copy(src_hbm.at[pl.ds(0, n)], dst_vmem.at[slot], sem.at[slot]).wait()


def _expert_kernel(te_ref, nu_ref, src_ref, xs_ref, wgu_ref, wdn_ref, o_ref,
                   idx_ref, xg_ref, wg_ref, wu_ref, wd_ref, acc_ref, sem_idx, sem_rows, sem_w, *, layer, tf):
    i = pl.program_id(0)
    n_used = nu_ref[0]
    used = i < n_used
    slot = i % 2
    f = wdn_ref.shape[2]
    nf = f // tf

    def idx_copy(tile, into):
        return pltpu.make_async_copy(src_ref.at[tile], idx_ref.at[into], sem_idx)

    def weight_copies(tile, j, into):
        e = te_ref[tile]
        c0 = pl.multiple_of(j * tf, tf)
        return (pltpu.make_async_copy(wgu_ref.at[layer, e, :, pl.ds(c0, tf)], wg_ref.at[into], sem_w.at[0, into]),
                pltpu.make_async_copy(wgu_ref.at[layer, e, :, pl.ds(f + c0, tf)], wu_ref.at[into], sem_w.at[1, into]),
                pltpu.make_async_copy(wdn_ref.at[layer, e, pl.ds(c0, tf), :], wd_ref.at[into], sem_w.at[2, into]))

    @pl.when(used)
    def _():
        @pl.when(i == 0)
        def _():
            idx_copy(0, 0).start()
            idx_copy(0, 0).wait()
            _start_chunk_gather(idx_ref, 0, xs_ref, xg_ref, sem_rows)
            for cp in weight_copies(0, 0, 0):
                cp.start()

        has_next = i + 1 < n_used

        @pl.when(has_next)
        def _():
            idx_copy(i + 1, 1 - slot).start()

        _wait_chunk_gather(slot, xs_ref, xg_ref, sem_rows)
        acc_ref[...] = jnp.zeros(acc_ref.shape, F32)

        @pl.when(has_next)
        def _():
            idx_copy(i + 1, 1 - slot).wait()
            _start_chunk_gather(idx_ref, 1 - slot, xs_ref, xg_ref, sem_rows)

        h = xg_ref[slot]

        def hidden_slice(j, carry):
            wslot = (i * nf + j) % 2

            @pl.when(j + 1 < nf)
            def _():
                for cp in weight_copies(i, j + 1, 1 - wslot):
                    cp.start()

            @pl.when(jnp.logical_and(j + 1 == nf, has_next))
            def _():
                for cp in weight_copies(i + 1, 0, 1 - wslot):
                    cp.start()

            for cp in weight_copies(i, j, wslot):
                cp.wait()
            gt = jnp.dot(h, wg_ref[wslot], preferred_element_type=F32)
            up = jnp.dot(h, wu_ref[wslot], preferred_element_type=F32)
            act = (_silu(gt) * up).astype(BF16)
            acc_ref[...] += jnp.dot(act, wd_ref[wslot], preferred_element_type=F32)
            return carry

        lax.fori_loop(0, nf, hidden_slice, 0)

    o_ref[...] = jnp.where(used, acc_ref[...], 0.0).astype(o_ref.dtype)


def _experts(xs, src_tiles, tile_expert, n_used, w_gu, w_down, layer, *, tm, tf):
    d = xs.shape[1]
    n_tiles, chunks_per_tile = src_tiles.shape
    assert chunks_per_tile * MOE_CHUNK == tm
    assert w_down.shape[2] % tf == 0
    grid_spec = pltpu.PrefetchScalarGridSpec(
        num_scalar_prefetch=2,
        grid=(n_tiles,),
        in_specs=[pl.BlockSpec(memory_space=pl.ANY)] * 4,
        out_specs=pl.BlockSpec((tm, d), lambda i, te, nu: (i, 0)),
        scratch_shapes=[
            pltpu.SMEM((2, chunks_per_tile), jnp.int32),
            pltpu.VMEM((2, tm, d), BF16),
            pltpu.VMEM((2, d, tf), BF16),
            pltpu.VMEM((2, d, tf), BF16),
            pltpu.VMEM((2, tf, d), BF16),
            pltpu.VMEM((tm, d), F32),
            pltpu.SemaphoreType.DMA,
            pltpu.SemaphoreType.DMA((2,)),
            pltpu.SemaphoreType.DMA((3, 2)),
        ],
    )
    return pl.pallas_call(
        functools.partial(_expert_kernel, layer=layer, tf=tf),
        grid_spec=grid_spec,
        out_shape=jax.ShapeDtypeStruct((n_tiles * tm, d), BF16),
        compiler_params=_params("arbitrary"),
        name="moe_experts",
    )(tile_expert, n_used, src_tiles, xs, w_gu, w_down)


def _combine_kernel(src_ref, ys_ref, x_ref, meta_ref, gate_ref, o_ref, idx_ref, yl_ref, sem_idx, sem_rows):
    i = pl.program_id(0)
    slot = i % 2
    has_next = i + 1 < pl.num_programs(0)

    def idx_copy(block, into):
        return pltpu.make_async_copy(src_ref.at[block], idx_ref.at[into], sem_idx)

    @pl.when(i == 0)
    def _():
        idx_copy(0, 0).start()
        idx_copy(0, 0).wait()
        _start_chunk_gather(idx_ref, 0, ys_ref, yl_ref, sem_rows)

    @pl.when(has_next)
    def _():
        idx_copy(i + 1, 1 - slot).start()

    _wait_chunk_gather(slot, ys_ref, yl_ref, sem_rows)

    @pl.when(has_next)
    def _():
        idx_copy(i + 1, 1 - slot).wait()
        _start_chunk_gather(idx_ref, 1 - slot, ys_ref, yl_ref, sem_rows)

    meta = meta_ref[...]
    yl = yl_ref[slot]
    lane = lax.broadcasted_iota(jnp.int32, (meta.shape[0], yl.shape[0]), 1).astype(F32)
    y1 = jnp.dot(jnp.where(lane == meta[:, 4:5], 1.0, 0.0).astype(BF16), yl, preferred_element_type=F32)
    y2 = jnp.dot(jnp.where(lane == meta[:, 5:6], 1.0, 0.0).astype(BF16), yl, preferred_element_type=F32)
    o_ref[...] = x_ref[...] + gate_ref[...] * (meta[:, 2:3] * y1 + meta[:, 3:4] * y2)


def _combine(ys, src_blocks, x, meta, gate, *, seq):
    t, d = x.shape
    nb, chunks_per_block = src_blocks.shape
    tb = t // nb
    local_rows = chunks_per_block * MOE_CHUNK
    gv = _RowVec(gate, tb, seq)
    return pl.pallas_call(
        _combine_kernel,
        grid=(nb,),
        in_specs=[
            pl.BlockSpec(memory_space=pl.ANY),
            pl.BlockSpec(memory_space=pl.ANY),
            pl.BlockSpec((tb, d), lambda i: (i, 0)),
            pl.BlockSpec((tb, LANES), lambda i: (i, 0)),
            gv.spec,
        ],
        out_specs=pl.BlockSpec((tb, d), lambda i: (i, 0)),
        out_shape=jax.ShapeDtypeStruct((t, d), F32),
        scratch_shapes=[
            pltpu.SMEM((2, chunks_per_block), jnp.int32),
            pltpu.VMEM((2, local_rows, d), BF16),
            pltpu.SemaphoreType.DMA,
            pltpu.SemaphoreType.DMA((2,)),
        ],
        compiler_params=_params("arbitrary"),
        name="moe_combine",
    )(src_blocks, ys, x, meta, gv.array)


def _moe_ffn(x, g, shift, scale, gate, router, w_gu, w_down, layer, *, seq):
    t, d = x.shape
    tb = min(MOE_BLOCK, seq)
    tm_e = min(MOE_TILE, seq)
    xs, meta, chunks = _route(x, g, shift, scale, router, seq=seq)
    nb = t // tb
    cpb = (TOP_K * tb + N_EXPERTS * MOE_CHUNK) // MOE_CHUNK
    cpt = tm_e // MOE_CHUNK

    nch = chunks[:, 0, :N_EXPERTS].astype(jnp.int32)
    local_first = jnp.cumsum(nch, axis=1) - nch
    per_expert = jnp.sum(nch, axis=0)
    tiles = (per_expert + cpt - 1) // cpt
    sorted_end = jnp.cumsum(tiles) * cpt
    sorted_first = sorted_end - tiles * cpt
    in_expert_end = jnp.cumsum(nch, axis=0)
    in_expert_first = in_expert_end - nch

    n_tiles = -(-(TOP_K * t // MOE_CHUNK + nb * N_EXPERTS) // cpt) + N_EXPERTS
    slot = jnp.arange(n_tiles * cpt, dtype=jnp.int32)
    e_of = jnp.minimum(jnp.sum(slot[:, None] >= sorted_end[None, :], axis=1), N_EXPERTS - 1)
    is_e = e_of[:, None] == jnp.arange(N_EXPERTS)[None, :]
    by_expert = lambda v: jnp.sum(jnp.where(is_e, v[None, :], 0), axis=1)
    off = slot - by_expert(sorted_first)
    valid = off < by_expert(per_expert)
    end_of_blocks = jnp.sum(jnp.where(is_e[:, :, None], in_expert_end.T[None], 0), axis=1)
    b_of = jnp.minimum(jnp.sum(off[:, None] >= end_of_blocks, axis=1), nb - 1)
    is_b = b_of[:, None] == jnp.arange(nb)[None, :]
    shift_tab = (local_first - in_expert_first).T
    shift = jnp.sum(jnp.where(is_b, jnp.sum(jnp.where(is_e[:, :, None], shift_tab[None], 0), axis=1), 0), axis=1)
    src_rows = jnp.where(valid, b_of * cpb + shift + off, 0) * MOE_CHUNK
    n_used = jnp.sum(tiles).astype(jnp.int32)
    tile_id = jnp.arange(n_tiles, dtype=jnp.int32)
    frozen = jnp.minimum(tile_id, jnp.maximum(n_used - 1, 0)) * cpt
    tile_expert = jnp.minimum(jnp.sum(frozen[:, None] >= sorted_end[None, :], axis=1), N_EXPERTS - 1)
    tile_expert = tile_expert.astype(jnp.int32)

    lc = jnp.arange(cpb, dtype=jnp.int32)
    local_end = local_first + nch
    e_loc = jnp.sum(lc[None, :, None] >= local_end[:, None, :], axis=2)
    used_loc = e_loc < N_EXPERTS
    is_e_loc = e_loc[:, :, None] == jnp.arange(N_EXPERTS)[None, None, :]
    back_tab = sorted_first[None, :] + in_expert_first - local_first
    back_chunk = jnp.sum(jnp.where(is_e_loc, back_tab[:, None, :], 0), axis=2) + lc[None, :]
    back_rows = jnp.where(used_loc, back_chunk, 0) * MOE_CHUNK

    ys = _experts(xs, src_rows.reshape(n_tiles, cpt), tile_expert, n_used.reshape(1),
                  w_gu, w_down, layer, tm=tm_e, tf=w_down.shape[2] // 2)
    return _combine(ys, back_rows.astype(jnp.int32), x, meta, gate, seq=seq)


def kernel(x, c, ada_w, ada_b, norm1_g, norm2_g, at_w_in, at_q_norm, at_k_norm, at_lam, at_subln, at_w_out, hg_w_in, hg_lower_bounds, hg_norm_g, hg_w_out, lru_w_in, lru_conv_w, lru_conv_b, lru_w_a, lru_b_a, lru_w_x, lru_b_x, lru_lambda, lru_w_out, ff_w_gu, ff_w_down, moe_router, moe_w_gu, moe_w_down):
    batch, seq, d = x.shape
    depth = ada_w.shape[0]
    t = batch * seq
    mods = _mods(c, ada_w, ada_b)
    ff_w_gu, ff_w_down = ff_w_gu.astype(BF16), ff_w_down.astype(BF16)
    moe_w_gu, moe_w_down = moe_w_gu.astype(BF16), moe_w_down.astype(BF16)
    xr = x.reshape(t, d)
    for i in range(depth):
        sh1, sc1, g1, sh2, sc2, g2 = [mods[i, :, k * d:(k + 1) * d] for k in range(6)]
        kind, j = i % N_MIXERS, i // N_MIXERS
        if kind == 0:
            lambda_init = 0.8 - 0.6 * math.exp(-0.3 * i)
            xr = _diff_attention_layer(xr, norm1_g[i], sh1, sc1, g1, at_w_in[j], at_q_norm[j], at_k_norm[j],
                                       at_lam[j], at_subln[j], at_w_out[j], lambda_init, batch=batch, seq=seq)
        elif kind == 1:
            xr = _hgrn_layer(xr, norm1_g[i], sh1, sc1, g1, hg_w_in[j], hg_lower_bounds, hg_norm_g[j],
                             hg_w_out[j], i, batch=batch, seq=seq)
        else:
            xr = _rglru_layer(xr, norm1_g[i], sh1, sc1, g1, lru_w_in[j], lru_conv_w[j], lru_conv_b[j],
                              lru_w_a[j], lru_b_a[j], lru_w_x[j], lru_b_x[j], lru_lambda[j], lru_w_out[j],
                              batch=batch, seq=seq)
        m = i // 2
        if i % 2 == 0:
            xr = _dense_ffn(xr, norm2_g[i], sh2, sc2, g2, ff_w_gu, ff_w_down, m,
                            seq=seq, tm=min(512, seq), tf=ff_w_down.shape[1])
        else:
            xr = _moe_ffn(xr, norm2_g[i], sh2, sc2, g2, moe_router[m], moe_w_gu, moe_w_down, m, seq=seq)
    return xr.reshape(batch, seq, d)
```

```python
import functools
import math

import numpy as np
import jax
import jax.numpy as jnp
from jax import lax
from jax.experimental import pallas as pl
from jax.experimental.pallas import tpu as pltpu

F32 = jnp.float32
BF16 = jnp.bfloat16
HIGHEST = lax.Precision.HIGHEST

EPS = 1e-6
LANES = 128
VMEM_LIMIT_BYTES = 56 * 2**20

N_MIXERS = 3
DA_HEADS = 8
DA_HEAD_DIM = 64
DA_V_DIM = 2 * DA_HEAD_DIM
ROPE_THETA = 10000.0
HG_HEADS = 8
HG_CHUNK = 64
HG_SUB = 16
LRU_BLOCK_W = 128
CONV_WIDTH = 4
LRU_C = 8.0
N_EXPERTS = 8
TOP_K = 2


def _params(*sem):
    return pltpu.CompilerParams(dimension_semantics=sem, vmem_limit_bytes=VMEM_LIMIT_BYTES)


def _sigmoid(x):
    return 1.0 / (1.0 + jnp.exp(-x))


def _silu(x):
    return x * _sigmoid(x)


def _sigmoid_tanh(x):
    return 0.5 * jnp.tanh(0.5 * x) + 0.5


def _modulated(x, g, shift, scale):
    ms = jnp.mean(x * x, axis=-1, keepdims=True)
    return x * lax.rsqrt(ms + EPS) * g * (1.0 + scale) + shift


def _mods_kernel(c_ref, w_ref, b_ref, o_ref):
    c = c_ref[...]
    o_ref[...] = jnp.dot(_silu(c), w_ref[...], preferred_element_type=F32, precision=HIGHEST) + b_ref[...]


def _mods(c, ada_w, ada_b):
    depth, d, n = ada_w.shape
    b = c.shape[0]
    tn = 2048
    return pl.pallas_call(
        _mods_kernel,
        grid=(depth, n // tn),
        in_specs=[
            pl.BlockSpec((b, d), lambda l, j: (0, 0)),
            pl.BlockSpec((None, d, tn), lambda l, j: (l, 0, j)),
            pl.BlockSpec((None, 1, tn), lambda l, j: (l, 0, j)),
        ],
        out_specs=pl.BlockSpec((None, b, tn), lambda l, j: (l, 0, j)),
        out_shape=jax.ShapeDtypeStruct((depth, b, n), F32),
        compiler_params=_params("parallel", "parallel"),
        name="adaln_mods",
    )(c, ada_w, ada_b.reshape(depth, 1, n))


class _RowVec:
    def __init__(self, vec, tm, seq):
        b, d = vec.shape
        assert seq % tm == 0
        tiles_per_batch = seq // tm
        self.array = vec.reshape(b, 1, d)
        self.spec = pl.BlockSpec((None, 1, d), lambda i, *_: (i // tiles_per_batch, 0, 0))


def _const_spec(shape):
    nd = len(shape)
    return pl.BlockSpec(shape, lambda *_: (0,) * nd)


def _nm_matmul_kernel(x_ref, g_ref, sh_ref, sc_ref, w_ref, o_ref, h_ref):
    @pl.when(pl.program_id(1) == 0)
    def _():
        h_ref[...] = _modulated(x_ref[...], g_ref[...], sh_ref[...], sc_ref[...]).astype(BF16)

    o_ref[...] = jnp.dot(h_ref[...], w_ref[...], preferred_element_type=F32).astype(o_ref.dtype)


def _nm_matmul(x, g, shift, scale, w, *, seq, tm, tn, out_dtype):
    t, d = x.shape
    n = w.shape[1]
    assert t % tm == 0 and n % tn == 0
    sh = _RowVec(shift, tm, seq)
    sc = _RowVec(scale, tm, seq)
    weight_buffers = pl.Buffered(1) if n == tn else pl.Buffered(2)
    return pl.pallas_call(
        _nm_matmul_kernel,
        grid=(t // tm, n // tn),
        in_specs=[
            pl.BlockSpec((tm, d), lambda i, j: (i, 0)),
            _const_spec((1, d)),
            sh.spec, sc.spec,
            pl.BlockSpec((d, tn), lambda i, j: (0, j), pipeline_mode=weight_buffers),
        ],
        out_specs=pl.BlockSpec((tm, tn), lambda i, j: (i, j)),
        out_shape=jax.ShapeDtypeStruct((t, n), out_dtype),
        scratch_shapes=[pltpu.VMEM((tm, d), BF16)],
        compiler_params=_params("parallel", "arbitrary"),
        name="modulate_matmul",
    )(x, g.reshape(1, d), sh.array, sc.array, w)


def _out_proj_kernel(*refs, prologue, n_in):
    ins = refs[:n_in]
    w_ref, x_ref, gate_ref, o_ref = refs[n_in:]
    a = prologue(*ins)
    mix = jnp.dot(a, w_ref[...], preferred_element_type=F32)
    o_ref[...] = x_ref[...] + gate_ref[...] * mix


def _out_proj(prologue, ins, in_specs, w, x, gate, *, seq, tm):
    t, d = x.shape
    k = w.shape[0]
    gv = _RowVec(gate, tm, seq)
    return pl.pallas_call(
        functools.partial(_out_proj_kernel, prologue=prologue, n_in=len(ins)),
        grid=(t // tm,),
        in_specs=list(in_specs) + [
            _const_spec((k, d)),
            pl.BlockSpec((tm, d), lambda i: (i, 0)),
            gv.spec,
        ],
        out_specs=pl.BlockSpec((tm, d), lambda i: (i, 0)),
        out_shape=jax.ShapeDtypeStruct((t, d), F32),
        compiler_params=_params("parallel"),
        name="out_proj_residual",
    )(*ins, w, x, gv.array)


QK_PREP_WIDTH = 2 * LANES


def _rope_constants(seq):
    half = DA_HEAD_DIM // 2
    inv = 1.0 / (ROPE_THETA ** (jnp.arange(0, DA_HEAD_DIM, 2, dtype=F32) / DA_HEAD_DIM))
    ang = jnp.arange(seq, dtype=F32)[:, None] * inv[None, :]
    reps = QK_PREP_WIDTH // half
    cos = jnp.tile(jnp.cos(ang), (1, reps))
    sin = jnp.tile(jnp.sin(ang), (1, reps))
    lane = np.arange(QK_PREP_WIDTH)
    first = (lane % DA_HEAD_DIM) < half
    rot = np.zeros((QK_PREP_WIDTH, QK_PREP_WIDTH), np.float32)
    rot[(lane + half)[first], lane[first]] = -1.0
    rot[(lane - half)[~first], lane[~first]] = 1.0
    group = (lane[:, None] // DA_HEAD_DIM == lane[None, :] // DA_HEAD_DIM).astype(np.float32)
    return cos, sin, jnp.asarray(rot, BF16), jnp.asarray(group, BF16)


def _attn_in_kernel(x_ref, g_ref, sh_ref, sc_ref, w_ref, gain_ref, cos_ref, sin_ref, rot_ref, grp_ref,
                    o_ref, h_ref):
    j = pl.program_id(1)

    @pl.when(j == 0)
    def _():
        h_ref[...] = _modulated(x_ref[...], g_ref[...], sh_ref[...], sc_ref[...]).astype(BF16)

    y_all = jnp.dot(h_ref[...], w_ref[...], preferred_element_type=F32)

    @pl.when(j == 2)
    def _():
        o_ref[...] = y_all.astype(BF16)

    @pl.when(j < 2)
    def _():
        out_scale = jnp.where(j == 0, DA_HEAD_DIM ** -0.5 * math.log2(math.e), 1.0).astype(F32)
        cos = cos_ref[...]
        sin = sin_ref[...]
        gain = gain_ref[...]
        for h in range(y_all.shape[1] // QK_PREP_WIDTH):
            sl = slice(h * QK_PREP_WIDTH, (h + 1) * QK_PREP_WIDTH)
            y = y_all[:, sl]
            ss = jnp.dot((y * y).astype(BF16), grp_ref[...], preferred_element_type=F32)
            n = y * lax.rsqrt(ss * (1.0 / DA_HEAD_DIM) + EPS) * gain
            r = jnp.dot(n.astype(BF16), rot_ref[...], preferred_element_type=F32)
            o_ref[:, sl] = ((n * cos + r * sin) * out_scale).astype(BF16)


def _attn_in_proj(x, g, shift, scale, w, q_gain, k_gain, cos, sin, rot, grp, *, seq, tm):
    t, d = x.shape
    width = DA_HEADS * LANES
    assert w.shape[1] == 3 * width
    reps = QK_PREP_WIDTH // DA_HEAD_DIM
    gains = jnp.stack([jnp.tile(q_gain, reps), jnp.tile(k_gain, reps)]).reshape(2, 1, QK_PREP_WIDTH).astype(F32)
    tiles_per_seq = seq // tm
    sh = _RowVec(shift, tm, seq)
    sc = _RowVec(scale, tm, seq)
    return pl.pallas_call(
        _attn_in_kernel,
        grid=(t // tm, 3),
        in_specs=[
            pl.BlockSpec((tm, d), lambda i, j: (i, 0)),
            _const_spec((1, d)),
            sh.spec, sc.spec,
            pl.BlockSpec((d, width), lambda i, j: (0, j)),
            pl.BlockSpec((None, 1, QK_PREP_WIDTH), lambda i, j: (jnp.minimum(j, 1), 0, 0)),
            pl.BlockSpec((tm, QK_PREP_WIDTH), lambda i, j: (i % tiles_per_seq, 0)),
            pl.BlockSpec((tm, QK_PREP_WIDTH), lambda i, j: (i % tiles_per_seq, 0)),
            _const_spec((QK_PREP_WIDTH, QK_PREP_WIDTH)),
            _const_spec((QK_PREP_WIDTH, QK_PREP_WIDTH)),
        ],
        out_specs=pl.BlockSpec((None, tm, width), lambda i, j: (j, i, 0)),
        out_shape=jax.ShapeDtypeStruct((3, t, width), BF16),
        scratch_shapes=[pltpu.VMEM((tm, d), BF16)],
        compiler_params=_params("parallel", "arbitrary"),
        name="attn_in_proj",
    )(x, g.reshape(1, d), sh.array, sc.array, w, gains, cos, sin, rot, grp)


def _flash_kernel(q_ref, k_ref, v_ref, lam_ref, subln_ref, o_ref,
                  vt_ref, kmax_ref, bound_ref, mrun_ref, acc_ref, den_ref, *, tq, tk, lambda_init):
    seq = k_ref.shape[0]
    n_kv = seq // tk
    lane = lax.broadcasted_iota(jnp.int32, (LANES, LANES), 0)
    first_half = jnp.where(lane < DA_HEAD_DIM, 1.0, 0.0).astype(BF16)
    second_half = jnp.where(lane >= DA_HEAD_DIM, 1.0, 0.0).astype(BF16)

    @pl.when(pl.program_id(2) == 0)
    def _():
        def prep(j, carry):
            n1, n2 = carry
            r0 = pl.multiple_of(j * tk, tk)
            vt_ref[:, pl.ds(r0, tk)] = v_ref[pl.ds(r0, tk), :].astype(F32).T.astype(BF16)
            kf = k_ref[pl.ds(r0, tk), :].astype(F32)
            sq = (kf * kf).astype(BF16)
            s1 = jnp.dot(sq, first_half, preferred_element_type=F32)
            s2 = jnp.dot(sq, second_half, preferred_element_type=F32)
            return (jnp.maximum(n1, jnp.max(s1, axis=0, keepdims=True)),
                    jnp.maximum(n2, jnp.max(s2, axis=0, keepdims=True)))

        zero = jnp.zeros((1, LANES), F32)
        n1, n2 = lax.fori_loop(0, n_kv, prep, (zero, zero))
        kmax_ref[0:1, :] = n1
        kmax_ref[1:2, :] = n2

    q = q_ref[...]
    qlane = lax.broadcasted_iota(jnp.int32, q.shape, 1)
    zero = jnp.zeros_like(q)
    qq = jnp.concatenate([jnp.where(qlane < DA_HEAD_DIM, q, zero),
                          jnp.where(qlane >= DA_HEAD_DIM, q, zero)], axis=0)
    qf = qq.astype(F32)
    nt = (((1,), (1,)), ((), ()))
    qn = lax.dot_general(jnp.ones((8, LANES), BF16), (qf * qf).astype(BF16), nt,
                         preferred_element_type=F32)[0:1]
    col = lax.broadcasted_iota(jnp.int32, qn.shape, 1)
    kn = jnp.where(col < tq, kmax_ref[0:1, 0:1], kmax_ref[1:2, 0:1])
    bound = jnp.sqrt(qn * kn) * 1.01
    bound_ref[...] = bound
    acc_ref[...] = jnp.zeros(acc_ref.shape, F32)
    den_ref[...] = jnp.zeros(den_ref.shape, F32)
    safe = jnp.max(bound) <= 60.0

    def scores(j):
        r0 = pl.multiple_of(j * tk, tk)
        st = lax.dot_general(k_ref[pl.ds(r0, tk), :], qq, nt, preferred_element_type=F32)
        return st, vt_ref[:, pl.ds(r0, tk)]

    @pl.when(safe)
    def _():
        def body(j, carry):
            st, vt = scores(j)
            p = jnp.exp2(st - bound_ref[...])
            den_ref[...] += jnp.sum(p, axis=0, keepdims=True)
            acc_ref[...] += jnp.dot(vt, p.astype(BF16), preferred_element_type=F32)
            return carry

        lax.fori_loop(0, n_kv, body, 0, unroll=2)

    @pl.when(jnp.logical_not(safe))
    def _():
        mrun_ref[...] = jnp.full(mrun_ref.shape, -jnp.inf, F32)

        def body(j, carry):
            st, vt = scores(j)
            m_old = mrun_ref[...]
            m_new = jnp.maximum(m_old, jnp.max(st, axis=0, keepdims=True))
            alpha = jnp.exp2(m_old - m_new)
            p = jnp.exp2(st - m_new)
            den_ref[...] = alpha * den_ref[...] + jnp.sum(p, axis=0, keepdims=True)
            acc_ref[...] = alpha * acc_ref[...] + jnp.dot(vt, p.astype(BF16), preferred_element_type=F32)
            mrun_ref[...] = m_new
            return carry

        lax.fori_loop(0, n_kv, body, 0)

    lf = lam_ref[...]
    lam = (jnp.exp(jnp.sum(lf[0:1] * lf[1:2], axis=1, keepdims=True))
           - jnp.exp(jnp.sum(lf[2:3] * lf[3:4], axis=1, keepdims=True)) + lambda_init)
    inv = 1.0 / den_ref[...]
    o1 = acc_ref[:, 0:tq] * inv[:, 0:tq]
    o2 = acc_ref[:, tq:2 * tq] * inv[:, tq:2 * tq]
    ot = o1 - lam * o2
    ms = jnp.mean(ot * ot, axis=0, keepdims=True)
    o = (ot * lax.rsqrt(ms + EPS)).T * (subln_ref[...] * (1.0 - lambda_init))
    o_ref[...] = o.astype(BF16)


def _flash(qkv, lam_vecs, subln_g, *, batch, seq, lambda_init, tq, tk):
    t = batch * seq
    width = DA_HEADS * LANES
    q_tiles = seq // tq
    return pl.pallas_call(
        functools.partial(_flash_kernel, tq=tq, tk=tk, lambda_init=lambda_init),
        grid=(batch, DA_HEADS, q_tiles),
        in_specs=[
            pl.BlockSpec((None, tq, LANES), lambda b, h, i: (0, b * q_tiles + i, h)),
            pl.BlockSpec((None, seq, LANES), lambda b, h, i: (1, b, h)),
            pl.BlockSpec((None, seq, LANES), lambda b, h, i: (2, b, h)),
            _const_spec((4, DA_HEAD_DIM)),
            _const_spec((1, DA_V_DIM)),
        ],
        out_specs=pl.BlockSpec((tq, LANES), lambda b, h, i: (b * q_tiles + i, h)),
        out_shape=jax.ShapeDtypeStruct((t, width), BF16),
        scratch_shapes=[
            pltpu.VMEM((LANES, seq), BF16),
            pltpu.VMEM((8, LANES), F32),
            pltpu.VMEM((1, 2 * tq), F32),
            pltpu.VMEM((1, 2 * tq), F32),
            pltpu.VMEM((LANES, 2 * tq), F32),
            pltpu.VMEM((1, 2 * tq), F32),
        ],
        compiler_params=_params("parallel", "parallel", "arbitrary"),
        name="diff_flash_attention",
    )(qkv, qkv, qkv, lam_vecs.astype(F32), subln_g.reshape(1, DA_V_DIM).astype(F32))


def _identity_prologue(o_ref):
    return o_ref[...]


def _diff_attention_layer(x, g, shift, scale, gate, w_in, q_norm, k_norm, lam_vecs, subln_g, w_out,
                          lambda_init, *, batch, seq):
    cos, sin, rot, grp = _rope_constants(seq)
    qkv = _attn_in_proj(x, g, shift, scale, w_in.astype(BF16), q_norm, k_norm, cos, sin, rot, grp,
                        seq=seq, tm=min(1024, seq))
    o = _flash(qkv, lam_vecs, subln_g, batch=batch, seq=seq, lambda_init=lambda_init,
               tq=min(1024, seq), tk=min(2048, seq))
    tm = min(512, seq)
    return _out_proj(_identity_prologue, [o], [pl.BlockSpec((tm, o.shape[1]), lambda i: (i, 0))],
                     w_out.astype(BF16), x, gate, seq=seq, tm=tm)


def _hgrn_scan_kernel(q_ref, f_ref, v_ref, lb_ref, o_ref, st_ref, *, reverse, layer):
    c = HG_CHUNK
    n_chunks = q_ref.shape[0] // c

    @pl.when(pl.program_id(1) == 0)
    def _():
        st_ref[...] = jnp.zeros(st_ref.shape, F32)

    lbw = lb_ref[...]
    e = jnp.exp(lbw - jnp.max(lbw, axis=0, keepdims=True))
    lb = jnp.sum(e[1:layer + 1], axis=0, keepdims=True) / jnp.sum(e, axis=0, keepdims=True)

    row = lax.broadcasted_iota(jnp.int32, (c, c), 0)
    col = lax.broadcasted_iota(jnp.int32, (c, c), 1)
    tri = (col >= row) if reverse else (col <= row)
    tri = tri.astype(F32)
    n_sub = c // HG_SUB
    half = HG_SUB // 2
    col8 = lax.broadcasted_iota(jnp.int32, (half, c), 1)
    row8 = lax.broadcasted_iota(jnp.int32, (half, c), 0)

    def key_rows(i, r):
        t0 = i * HG_SUB + r * half
        return range(t0, (i + 1) * HG_SUB) if reverse else range(i * HG_SUB, t0 + half)

    def placement(i, r, s):
        t0 = i * HG_SUB + r * half
        reached = (row8 + t0 <= s) if reverse else (row8 + t0 >= s)
        return jnp.logical_and(col8 == s, reached)

    place = {(i, r, s): placement(i, r, s) for i in range(n_sub) for r in range(2) for s in key_rows(i, r)}

    def chunk(ci, carry):
        cidx = (n_chunks - 1 - ci) if reverse else ci
        r0 = pl.multiple_of(cidx * c, c)
        qs = _silu(q_ref[pl.ds(r0, c), :])
        forget = lb + (1.0 - lb) * _sigmoid(f_ref[pl.ds(r0, c), :])
        kk = 1.0 - forget
        vb = v_ref[pl.ds(r0, c), :].astype(BF16)
        b = jnp.dot(tri, jnp.log2(forget), preferred_element_type=F32, precision=HIGHEST)
        b_edge = b[0:1] if reverse else b[c - 1:c]
        q_in = (qs * jnp.exp2(b)).astype(BF16)
        k_out = (kk * jnp.exp2(b_edge - b)).astype(BF16)
        dec = jnp.exp2(b_edge)
        key_shift = b - jnp.log2(kk)

        pieces = [[jnp.zeros((half, c), F32) for _ in range(2 * n_sub)] for _ in range(HG_HEADS)]
        for i in range(n_sub):
            for r in range(2):
                t0 = i * HG_SUB + r * half
                q8 = qs[t0:t0 + half]
                b8 = b[t0:t0 + half]
                for s in key_rows(i, r):
                    term = q8 * jnp.exp2(b8 - key_shift[s:s + 1])
                    for h in range(HG_HEADS):
                        a_col = jnp.sum(term[:, h * LANES:(h + 1) * LANES], axis=1, keepdims=True)
                        pieces[h][2 * i + r] = jnp.where(place[i, r, s], a_col, pieces[h][2 * i + r])

        for h in range(HG_HEADS):
            sl = slice(h * LANES, (h + 1) * LANES)
            blocks = []
            for i in range(n_sub):
                rows = slice(i * HG_SUB, (i + 1) * HG_SUB)
                passed = slice((i + 1) * HG_SUB, c) if reverse else slice(0, i * HG_SUB)
                n_passed = passed.stop - passed.start
                if n_passed == 0:
                    blocks.append(jnp.zeros((HG_SUB, c), F32))
                    continue
                ref_row = (i + 1) * HG_SUB if reverse else i * HG_SUB - 1
                b_ref_row = b[ref_row:ref_row + 1, sl]
                q_t = (qs[rows, sl] * jnp.exp2(b[rows, sl] - b_ref_row)).astype(BF16)
                k_p = (kk[passed, sl] * jnp.exp2(b_ref_row - b[passed, sl])).astype(BF16)
                rest = jnp.zeros((c - n_passed, LANES), BF16)
                k_t = jnp.concatenate([rest, k_p] if reverse else [k_p, rest], axis=0)
                blocks.append(lax.dot_general(q_t, k_t, (((1,), (1,)), ((), ())),
                                              preferred_element_type=F32))
            a = jnp.concatenate(pieces[h], axis=0) + jnp.concatenate(blocks, axis=0)
            st = st_ref[h]
            o_h = jnp.dot(a.astype(BF16), vb[:, sl], preferred_element_type=F32)
            o_h = o_h + lax.dot_general(q_in[:, sl], st.astype(BF16), (((1,), (1,)), ((), ())),
                                        preferred_element_type=F32)
            o_ref[pl.ds(r0, c), sl] = o_h.astype(o_ref.dtype)
            upd = lax.dot_general(vb[:, sl], k_out[:, sl], (((0,), (0,)), ((), ())),
                                  preferred_element_type=F32)
            st_ref[h] = st * dec[:, sl] + upd
        return carry

    lax.fori_loop(0, n_chunks, chunk, 0, unroll=2)


def _hgrn_scan(proj, lower_bounds, *, batch, seq, layer, reverse, tt):
    t = batch * seq
    width = HG_HEADS * LANES
    nt = seq // tt
    depth = lower_bounds.shape[1]
    d_idx = 1 if reverse else 0

    def rows(b, i):
        return b * nt + ((nt - 1 - i) if reverse else i)

    return pl.pallas_call(
        functools.partial(_hgrn_scan_kernel, reverse=reverse, layer=layer),
        grid=(batch, nt),
        in_specs=[
            pl.BlockSpec((tt, width), lambda b, i: (rows(b, i), 0)),
            pl.BlockSpec((tt, width), lambda b, i: (rows(b, i), 1 + d_idx)),
            pl.BlockSpec((tt, width), lambda b, i: (rows(b, i), 3)),
            pl.BlockSpec((None, depth, width), lambda b, i: (d_idx, 0, 0)),
        ],
        out_specs=pl.BlockSpec((tt, width), lambda b, i: (rows(b, i), 0)),
        out_shape=jax.ShapeDtypeStruct((t, width), BF16),
        scratch_shapes=[pltpu.VMEM((HG_HEADS, LANES, LANES), F32)],
        compiler_params=_params("parallel", "arbitrary"),
        name="hgrn2_scan_bwd" if reverse else "hgrn2_scan_fwd",
    )(proj, proj, proj, lower_bounds.astype(F32))


def _hgrn_prologue(of_ref, ob_ref, g_ref, ng_ref):
    ng = ng_ref[...]
    outs = []
    for h in range(HG_HEADS):
        sl = slice(h * LANES, (h + 1) * LANES)
        o = of_ref[:, sl].astype(F32) + ob_ref[:, sl].astype(F32)
        ms = jnp.mean(o * o, axis=-1, keepdims=True)
        outs.append((o * lax.rsqrt(ms + EPS) * ng * _silu(g_ref[:, sl])).astype(BF16))
    return jnp.concatenate(outs, axis=1)


def _hgrn_layer(x, g, shift, scale, gate, w_in, lower_bounds, norm_g, w_out, layer, *, batch, seq):
    width = HG_HEADS * LANES
    proj = _nm_matmul(x, g, shift, scale, w_in.astype(BF16), seq=seq,
                      tm=min(512, seq), tn=w_in.shape[1], out_dtype=F32)
    tt = min(512, seq)
    o_f = _hgrn_scan(proj, lower_bounds, batch=batch, seq=seq, layer=layer, reverse=False, tt=tt)
    o_b = _hgrn_scan(proj, lower_bounds, batch=batch, seq=seq, layer=layer, reverse=True, tt=tt)
    tm = min(512, seq)
    ins = [o_f, o_b, proj, norm_g.reshape(1, LANES).astype(F32)]
    specs = [
        pl.BlockSpec((tm, width), lambda i: (i, 0)),
        pl.BlockSpec((tm, width), lambda i: (i, 0)),
        pl.BlockSpec((tm, width), lambda i: (i, 4)),
        _const_spec((1, LANES)),
    ]
    return _out_proj(_hgrn_prologue, ins, specs, w_out.astype(BF16), x, gate, seq=seq, tm=tm)


def _lru_scan_kernel(prev_ref, main_ref, next_ref, cw_ref, cb_ref, wg_ref, ba_ref, bx_ref, lam_ref,
                     o_ref, a_ref, u_ref, h_ref, *, batch, nt):
    d = pl.program_id(0)
    i = pl.program_id(1)
    ti = i + d * (nt - 1 - 2 * i)
    rows, width = main_ref.shape
    tt = rows // batch

    @pl.when(i == 0)
    def _():
        h_ref[...] = jnp.zeros(h_ref.shape, F32)

    prev = jnp.where(ti == 0, 0.0, prev_ref[...])
    nxt = jnp.where(ti == nt - 1, 0.0, next_ref[...])
    ext = jnp.concatenate([prev, main_ref[...], nxt], axis=0)
    cw = cw_ref[...]
    xc = cb_ref[...] + cw[0:1] * ext[0:rows]
    for j in range(1, CONV_WIDTH):
        xc = xc + cw[j:j + 1] * ext[j * batch:j * batch + rows]

    neg_softplus = -LRU_C * (jnp.maximum(-lam_ref[...], 0.0) + jnp.log1p(jnp.exp(-jnp.abs(lam_ref[...]))))
    for n in range(width // LRU_BLOCK_W):
        sl = slice(n * LRU_BLOCK_W, (n + 1) * LRU_BLOCK_W)
        xb = xc[:, sl]
        gates = jnp.dot(xb.astype(BF16), wg_ref[n], preferred_element_type=F32)
        r = _sigmoid_tanh(gates[:, :LRU_BLOCK_W] + ba_ref[:, sl])
        ig = _sigmoid_tanh(gates[:, LRU_BLOCK_W:] + bx_ref[:, sl])
        a = jnp.exp(r * neg_softplus[:, sl])
        a_ref[:, sl] = a
        t = 1.0 - a * a
        u_ref[:, sl] = (t * lax.rsqrt(jnp.maximum(t, 1e-30))) * (ig * xb)

    def step(t, h):
        tl = t + d * (tt - 1 - 2 * t)
        r0 = pl.multiple_of(tl * batch, batch)
        h = a_ref[pl.ds(r0, batch), :] * h + u_ref[pl.ds(r0, batch), :]
        o_ref[pl.ds(r0, batch), :] = h
        return h

    h_ref[...] = lax.fori_loop(0, tt, step, h_ref[...])


def _lru_scan(proj, conv_w, conv_b, w_a, b_a, w_x, b_x, lam, *, batch, seq, tt):
    t = batch * seq
    width = conv_w.shape[1]
    n_blocks = width // LRU_BLOCK_W
    nt = seq // tt
    rows = tt * batch
    assert tt % 2 == 0

    def tile(d, i):
        return i + d * (nt - 1 - 2 * i)

    wg = jnp.concatenate([w_a, w_x], axis=-1).astype(BF16)
    vec = lambda a: a.reshape(2, 1, width).astype(F32)
    return pl.pallas_call(
        functools.partial(_lru_scan_kernel, batch=batch, nt=nt),
        grid=(2, nt),
        in_specs=[
            pl.BlockSpec((batch, width), lambda d, i: (jnp.maximum(tile(d, i) * tt - 1, 0), 1)),
            pl.BlockSpec((rows, width), lambda d, i: (tile(d, i), 1)),
            pl.BlockSpec((2 * batch, width),
                         lambda d, i: (jnp.minimum((tile(d, i) + 1) * (tt // 2), seq // 2 - 1), 1)),
            _const_spec((CONV_WIDTH, width)),
            _const_spec((1, width)),
            pl.BlockSpec((None, n_blocks, LRU_BLOCK_W, 2 * LRU_BLOCK_W), lambda d, i: (d, 0, 0, 0)),
            pl.BlockSpec((None, 1, width), lambda d, i: (d, 0, 0)),
            pl.BlockSpec((None, 1, width), lambda d, i: (d, 0, 0)),
            pl.BlockSpec((None, 1, width), lambda d, i: (d, 0, 0)),
        ],
        out_specs=pl.BlockSpec((None, rows, width), lambda d, i: (d, tile(d, i), 0)),
        out_shape=jax.ShapeDtypeStruct((2, t, width), F32),
        scratch_shapes=[
            pltpu.VMEM((rows, width), F32),
            pltpu.VMEM((rows, width), F32),
            pltpu.VMEM((batch, width), F32),
        ],
        compiler_params=_params("arbitrary", "arbitrary"),
        name="rglru_scan",
    )(proj, proj, proj, conv_w.astype(F32), conv_b.reshape(1, width).astype(F32), wg,
      vec(b_a), vec(b_x), vec(lam))


def _gelu_tanh(x):
    return 0.5 * x * (1.0 + jnp.tanh(math.sqrt(2.0 / math.pi) * (x + 0.044715 * (x * x * x))))


def _lru_in_kernel(x_ref, g_ref, sh_ref, sc_ref, w_ref, o_ref, xm_ref, h_ref, *, batch, ts):
    n_lane_blocks = x_ref.shape[2] // LANES
    for b in range(batch):
        xm = _modulated(x_ref[b], g_ref[...], sh_ref[b:b + 1, :], sc_ref[b:b + 1, :])
        for c in range(n_lane_blocks):
            xm_ref[c, b * ts:(b + 1) * ts, :] = xm[:, c * LANES:(c + 1) * LANES]
    for s in range(ts):
        for c in range(n_lane_blocks):
            h_ref[s * batch:(s + 1) * batch, c * LANES:(c + 1) * LANES] = \
                xm_ref[c, pl.ds(s, batch, stride=ts), :]
    o_ref[...] = jnp.dot(h_ref[...].astype(BF16), w_ref[...], preferred_element_type=F32)


def _lru_in_proj(x, g, shift, scale, w, *, batch, seq, ts):
    t, d = x.shape
    n = w.shape[1]
    rows = ts * batch
    return pl.pallas_call(
        functools.partial(_lru_in_kernel, batch=batch, ts=ts),
        grid=(seq // ts,),
        in_specs=[
            pl.BlockSpec((batch, ts, d), lambda i: (0, i, 0)),
            _const_spec((1, d)),
            _const_spec((batch, d)),
            _const_spec((batch, d)),
            _const_spec((d, n)),
        ],
        out_specs=pl.BlockSpec((rows, n), lambda i: (i, 0)),
        out_shape=jax.ShapeDtypeStruct((t, n), F32),
        scratch_shapes=[pltpu.VMEM((d // LANES, rows, LANES), F32), pltpu.VMEM((rows, d), F32)],
        compiler_params=_params("parallel"),
        name="rglru_in_proj",
    )(x.reshape(batch, seq, d), g.reshape(1, d), shift, scale, w)


def _lru_out_kernel(hf_ref, hb_ref, y_ref, w_ref, x_ref, gate_ref, o_ref, mix_ref, *, batch, ts):
    a = ((hf_ref[...] + hb_ref[...]) * _gelu_tanh(y_ref[...])).astype(BF16)
    mix = jnp.dot(a, w_ref[...], preferred_element_type=F32)
    n_lane_blocks = mix.shape[1] // LANES
    for c in range(n_lane_blocks):
        mix_ref[c] = mix[:, c * LANES:(c + 1) * LANES]
    for b in range(batch):
        for c in range(n_lane_blocks):
            sl = slice(c * LANES, (c + 1) * LANES)
            o_ref[b, :, sl] = x_ref[b, :, sl] + gate_ref[b:b + 1, sl] * mix_ref[c, pl.ds(b, ts, stride=batch), :]


def _lru_out_proj(h2, proj, w, x, gate, *, batch, seq, ts):
    t, d = x.shape
    width = w.shape[0]
    rows = ts * batch
    out = pl.pallas_call(
        functools.partial(_lru_out_kernel, batch=batch, ts=ts),
        grid=(seq // ts,),
        in_specs=[
            pl.BlockSpec((None, rows, width), lambda i: (0, i, 0)),
            pl.BlockSpec((None, rows, width), lambda i: (1, i, 0)),
            pl.BlockSpec((rows, width), lambda i: (i, 0)),
            _const_spec((width, d)),
            pl.BlockSpec((batch, ts, d), lambda i: (0, i, 0)),
            _const_spec((batch, d)),
        ],
        out_specs=pl.BlockSpec((batch, ts, d), lambda i: (0, i, 0)),
        out_shape=jax.ShapeDtypeStruct((batch, seq, d), F32),
        scratch_shapes=[pltpu.VMEM((d // LANES, rows, LANES), F32)],
        compiler_params=_params("parallel"),
        name="rglru_out_proj",
    )(h2, h2, proj, w, x.reshape(batch, seq, d), gate)
    return out.reshape(t, d)


def _rglru_layer(x, g, shift, scale, gate, w_in, conv_w, conv_b, w_a, b_a, w_x, b_x, lam, w_out,
                 *, batch, seq):
    width = conv_w.shape[1]
    ts = min(64, seq)
    proj = _lru_in_proj(x, g, shift, scale, w_in.astype(BF16), batch=batch, seq=seq, ts=ts)
    h2 = _lru_scan(proj, conv_w, conv_b, w_a, b_a, w_x, b_x, lam, batch=batch, seq=seq, tt=min(64, seq))
    return _lru_out_proj(h2, proj, w_out.astype(BF16), x, gate, batch=batch, seq=seq, ts=ts)


def _ffn_kernel(x_ref, g_ref, sh_ref, sc_ref, wg_ref, wu_ref, wd_ref, gate_ref, o_ref, h_ref, acc_ref):
    j = pl.program_id(1)

    @pl.when(j == 0)
    def _():
        h_ref[...] = _modulated(x_ref[...], g_ref[...], sh_ref[...], sc_ref[...]).astype(BF16)
        acc_ref[...] = jnp.zeros(acc_ref.shape, F32)

    h = h_ref[...]
    gt = jnp.dot(h, wg_ref[...], preferred_element_type=F32)
    up = jnp.dot(h, wu_ref[...], preferred_element_type=F32)
    act = (_silu(gt) * up).astype(BF16)
    acc_ref[...] += jnp.dot(act, wd_ref[...], preferred_element_type=F32)

    @pl.when(j == pl.num_programs(1) - 1)
    def _():
        o_ref[...] = x_ref[...] + gate_ref[...] * acc_ref[...]


def _dense_ffn(x, g, shift, scale, gate, w_gu, w_down, layer, *, seq, tm, tf):
    t, d = x.shape
    f = w_down.shape[1]
    assert f % tf == 0 and seq % tm == 0
    nf = f // tf
    sh = _RowVec(shift, tm, seq)
    sc = _RowVec(scale, tm, seq)
    gv = _RowVec(gate, tm, seq)
    weight_buffers = pl.Buffered(1) if nf == 1 else pl.Buffered(2)
    return pl.pallas_call(
        _ffn_kernel,
        grid=(t // tm, nf),
        in_specs=[
            pl.BlockSpec((tm, d), lambda i, j: (i, 0)),
            _const_spec((1, d)),
            sh.spec, sc.spec,
            pl.BlockSpec((None, d, tf), lambda i, j: (layer, 0, j), pipeline_mode=weight_buffers),
            pl.BlockSpec((None, d, tf), lambda i, j: (layer, 0, nf + j), pipeline_mode=weight_buffers),
            pl.BlockSpec((None, tf, d), lambda i, j: (layer, j, 0), pipeline_mode=weight_buffers),
            gv.spec,
        ],
        out_specs=pl.BlockSpec((tm, d), lambda i, j: (i, 0)),
        out_shape=jax.ShapeDtypeStruct((t, d), F32),
        scratch_shapes=[pltpu.VMEM((tm, d), BF16), pltpu.VMEM((tm, d), F32)],
        compiler_params=_params("parallel", "arbitrary"),
        name="dense_swiglu",
    )(x, g.reshape(1, d), sh.array, sc.array, w_gu, w_gu, w_down, gv.array)


MOE_BLOCK = 512
MOE_CHUNK = 16
MOE_TILE = 1024


def _route_kernel(x_ref, g_ref, sh_ref, sc_ref, rhi_ref, rlo_ref, xs_ref, meta_ref, cnt_ref):
    tb = x_ref.shape[0]
    local_rows = xs_ref.shape[0]
    hf = _modulated(x_ref[...], g_ref[...], sh_ref[...], sc_ref[...])
    h_hi = hf.astype(BF16)
    h_lo = (hf - h_hi.astype(F32)).astype(BF16)
    logits = (jnp.dot(h_hi, rhi_ref[...], preferred_element_type=F32)
              + jnp.dot(h_lo, rhi_ref[...], preferred_element_type=F32)
              + jnp.dot(h_hi, rlo_ref[...], preferred_element_type=F32))
    lane = lax.broadcasted_iota(jnp.int32, logits.shape, 1).astype(F32)
    logits = jnp.where(lane < N_EXPERTS, logits, -jnp.inf)
    m1 = jnp.max(logits, axis=1, keepdims=True)
    e1 = jnp.min(jnp.where(logits == m1, lane, float(LANES)), axis=1, keepdims=True)
    rest = jnp.where(lane == e1, -jnp.inf, logits)
    m2 = jnp.max(rest, axis=1, keepdims=True)
    e2 = jnp.min(jnp.where(rest == m2, lane, float(LANES)), axis=1, keepdims=True)
    gate2 = 1.0 / (1.0 + jnp.exp(m1 - m2))
    gate1 = 1.0 - gate2
    hot1 = (lane == e1).astype(F32)
    hot2 = (lane == e2).astype(F32)
    sel = hot1 + hot2
    row = lax.broadcasted_iota(jnp.int32, (tb, tb), 0)
    col = lax.broadcasted_iota(jnp.int32, (tb, tb), 1)
    before = (col < row).astype(BF16)
    seen = jnp.dot(before, sel.astype(BF16), preferred_element_type=F32)
    chunks = jnp.floor((jnp.sum(sel, axis=0, keepdims=True) + (MOE_CHUNK - 1.0)) * (1.0 / MOE_CHUNK))
    cnt_ref[...] = chunks
    lrow = lax.broadcasted_iota(jnp.int32, (LANES, LANES), 0)
    lcol = lax.broadcasted_iota(jnp.int32, (LANES, LANES), 1)
    lower_experts = (lrow < lcol).astype(BF16)
    first_chunk = jnp.dot(jnp.broadcast_to(chunks, (8, LANES)).astype(BF16), lower_experts,
                          preferred_element_type=F32)[0:1]
    base = first_chunk * float(MOE_CHUNK) + seen
    ld1 = jnp.sum(hot1 * base, axis=1, keepdims=True)
    ld2 = jnp.sum(hot2 * base, axis=1, keepdims=True)
    meta = jnp.zeros(logits.shape, F32)
    for k, val in enumerate((e1, e2, gate1, gate2, ld1, ld2)):
        meta = jnp.where(lane == float(k), val, meta)
    meta_ref[...] = meta
    eye = row == col
    ld1_row = jnp.sum(jnp.where(eye, ld1, 0.0), axis=0, keepdims=True)
    ld2_row = jnp.sum(jnp.where(eye, ld2, 0.0), axis=0, keepdims=True)
    r_iota = lax.broadcasted_iota(jnp.int32, (local_rows, tb), 0).astype(F32)
    pick = jnp.where(r_iota == ld1_row, 1.0, jnp.where(r_iota == ld2_row, 1.0, 0.0)).astype(BF16)
    xs_ref[...] = jnp.dot(pick, h_hi, preferred_element_type=F32).astype(BF16)


def _route(x, g, shift, scale, router, *, seq):
    t, d = x.shape
    tb = min(MOE_BLOCK, seq)
    local_rows = TOP_K * tb + N_EXPERTS * MOE_CHUNK
    nb = t // tb
    sh = _RowVec(shift, tb, seq)
    sc = _RowVec(scale, tb, seq)
    router_pad = jnp.zeros((d, LANES), F32).at[:, :N_EXPERTS].set(router.astype(F32))
    router_hi = router_pad.astype(BF16)
    router_lo = (router_pad - router_hi.astype(F32)).astype(BF16)
    return pl.pallas_call(
        _route_kernel,
        grid=(nb,),
        in_specs=[
            pl.BlockSpec((tb, d), lambda i: (i, 0)),
            _const_spec((1, d)),
            sh.spec, sc.spec,
            _const_spec((d, LANES)),
            _const_spec((d, LANES)),
        ],
        out_specs=[
            pl.BlockSpec((local_rows, d), lambda i: (i, 0)),
            pl.BlockSpec((tb, LANES), lambda i: (i, 0)),
            pl.BlockSpec((None, 1, LANES), lambda i: (i, 0, 0)),
        ],
        out_shape=[
            jax.ShapeDtypeStruct((nb * local_rows, d), BF16),
            jax.ShapeDtypeStruct((t, LANES), F32),
            jax.ShapeDtypeStruct((nb, 1, LANES), F32),
        ],
        compiler_params=_params("parallel"),
        name="moe_route",
    )(x, g.reshape(1, d), sh.array, sc.array, router_hi, router_lo)


def _start_chunk_gather(row_smem, slot, src_hbm, dst_vmem, sem):
    n = dst_vmem.shape[1] // MOE_CHUNK

    def start(k, c):
        r0 = pl.multiple_of(row_smem[slot, k], MOE_CHUNK)
        pltpu.make_async_copy(src_hbm.at[pl.ds(r0, MOE_CHUNK)],
                              dst_vmem.at[slot, pl.ds(pl.multiple_of(k * MOE_CHUNK, MOE_CHUNK), MOE_CHUNK)],
                              sem.at[slot]).start()
        return c

    lax.fori_loop(0, n, start, 0, unroll=8)


def _wait_chunk_gather(slot, src_hbm, dst_vmem, sem):
    n = dst_vmem.shape[1]
    pltpu.make_async_copy(src_hbm.at[pl.ds(0, n)], dst_vmem.at[slot], sem.at[slot]).wait()


def _expert_kernel(te_ref, nu_ref, src_ref, xs_ref, wgu_ref, wdn_ref, o_ref,
                   idx_ref, xg_ref, wg_ref, wu_ref, wd_ref, acc_ref, sem_idx, sem_rows, sem_w, *, layer, tf):
    i = pl.program_id(0)
    n_used = nu_ref[0]
    used = i < n_used
    slot = i % 2
    f = wdn_ref.shape[2]
    nf = f // tf

    def idx_copy(tile, into):
        return pltpu.make_async_copy(src_ref.at[tile], idx_ref.at[into], sem_idx)

    def weight_copies(tile, j, into):
        e = te_ref[tile]
        c0 = pl.multiple_of(j * tf, tf)
        return (pltpu.make_async_copy(wgu_ref.at[layer, e, :, pl.ds(c0, tf)], wg_ref.at[into], sem_w.at[0, into]),
                pltpu.make_async_copy(wgu_ref.at[layer, e, :, pl.ds(f + c0, tf)], wu_ref.at[into], sem_w.at[1, into]),
                pltpu.make_async_copy(wdn_ref.at[layer, e, pl.ds(c0, tf), :], wd_ref.at[into], sem_w.at[2, into]))

    @pl.when(used)
    def _():
        @pl.when(i == 0)
        def _():
            idx_copy(0, 0).start()
            idx_copy(0, 0).wait()
            _start_chunk_gather(idx_ref, 0, xs_ref, xg_ref, sem_rows)
            for cp in weight_copies(0, 0, 0):
                cp.start()

        has_next = i + 1 < n_used

        @pl.when(has_next)
        def _():
            idx_copy(i + 1, 1 - slot).start()

        _wait_chunk_gather(slot, xs_ref, xg_ref, sem_rows)
        acc_ref[...] = jnp.zeros(acc_ref.shape, F32)

        @pl.when(has_next)
        def _():
            idx_copy(i + 1, 1 - slot).wait()
            _start_chunk_gather(idx_ref, 1 - slot, xs_ref, xg_ref, sem_rows)

        h = xg_ref[slot]

        def hidden_slice(j, carry):
            wslot = (i * nf + j) % 2

            @pl.when(j + 1 < nf)
            def _():
                for cp in weight_copies(i, j + 1, 1 - wslot):
                    cp.start()

            @pl.when(jnp.logical_and(j + 1 == nf, has_next))
            def _():
                for cp in weight_copies(i + 1, 0, 1 - wslot):
                    cp.start()

            for cp in weight_copies(i, j, wslot):
                cp.wait()
            gt = jnp.dot(h, wg_ref[wslot], preferred_element_type=F32)
            up = jnp.dot(h, wu_ref[wslot], preferred_element_type=F32)
            act = (_silu(gt) * up).astype(BF16)
            acc_ref[...] += jnp.dot(act, wd_ref[wslot], preferred_element_type=F32)
            return carry

        lax.fori_loop(0, nf, hidden_slice, 0)

    o_ref[...] = jnp.where(used, acc_ref[...], 0.0).astype(o_ref.dtype)


def _experts(xs, src_tiles, tile_expert, n_used, w_gu, w_down, layer, *, tm, tf):
    d = xs.shape[1]
    n_tiles, chunks_per_tile = src_tiles.shape
    assert chunks_per_tile * MOE_CHUNK == tm
    assert w_down.shape[2] % tf == 0
    grid_spec = pltpu.PrefetchScalarGridSpec(
        num_scalar_prefetch=2,
        grid=(n_tiles,),
        in_specs=[pl.BlockSpec(memory_space=pl.ANY)] * 4,
        out_specs=pl.BlockSpec((tm, d), lambda i, te, nu: (i, 0)),
        scratch_shapes=[
            pltpu.SMEM((2, chunks_per_tile), jnp.int32),
            pltpu.VMEM((2, tm, d), BF16),
            pltpu.VMEM((2, d, tf), BF16),
            pltpu.VMEM((2, d, tf), BF16),
            pltpu.VMEM((2, tf, d), BF16),
            pltpu.VMEM((tm, d), F32),
            pltpu.SemaphoreType.DMA,
            pltpu.SemaphoreType.DMA((2,)),
            pltpu.SemaphoreType.DMA((3, 2)),
        ],
    )
    return pl.pallas_call(
        functools.partial(_expert_kernel, layer=layer, tf=tf),
        grid_spec=grid_spec,
        out_shape=jax.ShapeDtypeStruct((n_tiles * tm, d), BF16),
        compiler_params=_params("arbitrary"),
        name="moe_experts",
    )(tile_expert, n_used, src_tiles, xs, w_gu, w_down)


def _combine_kernel(src_ref, ys_ref, x_ref, meta_ref, gate_ref, o_ref, idx_ref, yl_ref, sem_idx, sem_rows):
    i = pl.program_id(0)
    slot = i % 2
    has_next = i + 1 < pl.num_programs(0)

    def idx_copy(block, into):
        return pltpu.make_async_copy(src_ref.at[block], idx_ref.at[into], sem_idx)

    @pl.when(i == 0)
    def _():
        idx_copy(0, 0).start()
        idx_copy(0, 0).wait()
        _start_chunk_gather(idx_ref, 0, ys_ref, yl_ref, sem_rows)

    @pl.when(has_next)
    def _():
        idx_copy(i + 1, 1 - slot).start()

    _wait_chunk_gather(slot, ys_ref, yl_ref, sem_rows)

    @pl.when(has_next)
    def _():
        idx_copy(i + 1, 1 - slot).wait()
        _start_chunk_gather(idx_ref, 1 - slot, ys_ref, yl_ref, sem_rows)

    meta = meta_ref[...]
    yl = yl_ref[slot]
    lane = lax.broadcasted_iota(jnp.int32, (meta.shape[0], yl.shape[0]), 1).astype(F32)
    y1 = jnp.dot(jnp.where(lane == meta[:, 4:5], 1.0, 0.0).astype(BF16), yl, preferred_element_type=F32)
    y2 = jnp.dot(jnp.where(lane == meta[:, 5:6], 1.0, 0.0).astype(BF16), yl, preferred_element_type=F32)
    o_ref[...] = x_ref[...] + gate_ref[...] * (meta[:, 2:3] * y1 + meta[:, 3:4] * y2)


def _combine(ys, src_blocks, x, meta, gate, *, seq):
    t, d = x.shape
    nb, chunks_per_block = src_blocks.shape
    tb = t // nb
    local_rows = chunks_per_block * MOE_CHUNK
    gv = _RowVec(gate, tb, seq)
    return pl.pallas_call(
        _combine_kernel,
        grid=(nb,),
        in_specs=[
            pl.BlockSpec(memory_space=pl.ANY),
            pl.BlockSpec(memory_space=pl.ANY),
            pl.BlockSpec((tb, d), lambda i: (i, 0)),
            pl.BlockSpec((tb, LANES), lambda i: (i, 0)),
            gv.spec,
        ],
        out_specs=pl.BlockSpec((tb, d), lambda i: (i, 0)),
        out_shape=jax.ShapeDtypeStruct((t, d), F32),
        scratch_shapes=[
            pltpu.SMEM((2, chunks_per_block), jnp.int32),
            pltpu.VMEM((2, local_rows, d), BF16),
            pltpu.SemaphoreType.DMA,
            pltpu.SemaphoreType.DMA((2,)),
        ],
        compiler_params=_params("arbitrary"),
        name="moe_combine",
    )(src_blocks, ys, x, meta, gv.array)


def _moe_ffn(x, g, shift, scale, gate, router, w_gu, w_down, layer, *, seq):
    t, d = x.shape
    tb = min(MOE_BLOCK, seq)
    tm_e = min(MOE_TILE, seq)
    xs, meta, chunks = _route(x, g, shift, scale, router, seq=seq)
    nb = t // tb
    cpb = (TOP_K * tb + N_EXPERTS * MOE_CHUNK) // MOE_CHUNK
    cpt = tm_e // MOE_CHUNK

    nch = chunks[:, 0, :N_EXPERTS].astype(jnp.int32)
    local_first = jnp.cumsum(nch, axis=1) - nch
    per_expert = jnp.sum(nch, axis=0)
    tiles = (per_expert + cpt - 1) // cpt
    sorted_end = jnp.cumsum(tiles) * cpt
    sorted_first = sorted_end - tiles * cpt
    in_expert_end = jnp.cumsum(nch, axis=0)
    in_expert_first = in_expert_end - nch

    n_tiles = -(-(TOP_K * t // MOE_CHUNK + nb * N_EXPERTS) // cpt) + N_EXPERTS
    slot = jnp.arange(n_tiles * cpt, dtype=jnp.int32)
    e_of = jnp.minimum(jnp.sum(slot[:, None] >= sorted_end[None, :], axis=1), N_EXPERTS - 1)
    is_e = e_of[:, None] == jnp.arange(N_EXPERTS)[None, :]
    by_expert = lambda v: jnp.sum(jnp.where(is_e, v[None, :], 0), axis=1)
    off = slot - by_expert(sorted_first)
    valid = off < by_expert(per_expert)
    end_of_blocks = jnp.sum(jnp.where(is_e[:, :, None], in_expert_end.T[None], 0), axis=1)
    b_of = jnp.minimum(jnp.sum(off[:, None] >= end_of_blocks, axis=1), nb - 1)
    is_b = b_of[:, None] == jnp.arange(nb)[None, :]
    shift_tab = (local_first - in_expert_first).T
    shift = jnp.sum(jnp.where(is_b, jnp.sum(jnp.where(is_e[:, :, None], shift_tab[None], 0), axis=1), 0), axis=1)
    src_rows = jnp.where(valid, b_of * cpb + shift + off, 0) * MOE_CHUNK
    n_used = jnp.sum(tiles).astype(jnp.int32)
    tile_id = jnp.arange(n_tiles, dtype=jnp.int32)
    frozen = jnp.minimum(tile_id, jnp.maximum(n_used - 1, 0)) * cpt
    tile_expert = jnp.minimum(jnp.sum(frozen[:, None] >= sorted_end[None, :], axis=1), N_EXPERTS - 1)
    tile_expert = tile_expert.astype(jnp.int32)

    lc = jnp.arange(cpb, dtype=jnp.int32)
    local_end = local_first + nch
    e_loc = jnp.sum(lc[None, :, None] >= local_end[:, None, :], axis=2)
    used_loc = e_loc < N_EXPERTS
    is_e_loc = e_loc[:, :, None] == jnp.arange(N_EXPERTS)[None, None, :]
    back_tab = sorted_first[None, :] + in_expert_first - local_first
    back_chunk = jnp.sum(jnp.where(is_e_loc, back_tab[:, None, :], 0), axis=2) + lc[None, :]
    back_rows = jnp.where(used_loc, back_chunk, 0) * MOE_CHUNK

    ys = _experts(xs, src_rows.reshape(n_tiles, cpt), tile_expert, n_used.reshape(1),
                  w_gu, w_down, layer, tm=tm_e, tf=w_down.shape[2] // 2)
    return _combine(ys, back_rows.astype(jnp.int32), x, meta, gate, seq=seq)


def kernel(x, c, ada_w, ada_b, norm1_g, norm2_g, at_w_in, at_q_norm, at_k_norm, at_lam, at_subln, at_w_out, hg_w_in, hg_lower_bounds, hg_norm_g, hg_w_out, lru_w_in, lru_conv_w, lru_conv_b, lru_w_a, lru_b_a, lru_w_x, lru_b_x, lru_lambda, lru_w_out, ff_w_gu, ff_w_down, moe_router, moe_w_gu, moe_w_down):
    batch, seq, d = x.shape
    depth = ada_w.shape[0]
    t = batch * seq
    mods = _mods(c, ada_w, ada_b)
    ff_w_gu, ff_w_down = ff_w_gu.astype(BF16), ff_w_down.astype(BF16)
    moe_w_gu, moe_w_down = moe_w_gu.astype(BF16), moe_w_down.astype(BF16)
    xr = x.reshape(t, d)
    for i in range(depth):
        sh1, sc1, g1, sh2, sc2, g2 = [mods[i, :, k * d:(k + 1) * d] for k in range(6)]
        kind, j = i % N_MIXERS, i // N_MIXERS
        if kind == 0:
            lambda_init = 0.8 - 0.6 * math.exp(-0.3 * i)
            xr = _diff_attention_layer(xr, norm1_g[i], sh1, sc1, g1, at_w_in[j], at_q_norm[j], at_k_norm[j],
                                       at_lam[j], at_subln[j], at_w_out[j], lambda_init, batch=batch, seq=seq)
        elif kind == 1:
            xr = _hgrn_layer(xr, norm1_g[i], sh1, sc1, g1, hg_w_in[j], hg_lower_bounds, hg_norm_g[j],
                             hg_w_out[j], i, batch=batch, seq=seq)
        else:
            xr = _rglru_layer(xr, norm1_g[i], sh1, sc1, g1, lru_w_in[j], lru_conv_w[j], lru_conv_b[j],
                              lru_w_a[j], lru_b_a[j], lru_w_x[j], lru_b_x[j], lru_lambda[j], lru_w_out[j],
                              batch=batch, seq=seq)
        m = i // 2
        if i % 2 == 0:
            xr = _dense_ffn(xr, norm2_g[i], sh2, sc2, g2, ff_w_gu, ff_w_down, m,
                            seq=seq, tm=min(512, seq), tf=ff_w_down.shape[1])
        else:
            xr = _moe_ffn(xr, norm2_g[i], sh2, sc2, g2, moe_router[m], moe_w_gu, moe_w_down, m, seq=seq)
    return xr.reshape(batch, seq, d)
```

```python
import functools
import math

import numpy as np
import jax
import jax.numpy as jnp
from jax import lax
from jax.experimental import pallas as pl
from jax.experimental.pallas import tpu as pltpu

F32 = jnp.float32
BF16 = jnp.bfloat16
HIGHEST = lax.Precision.HIGHEST

EPS = 1e-6
LANES = 128
VMEM_LIMIT_BYTES = 56 * 2**20

N_MIXERS = 3
DA_HEADS = 8
DA_HEAD_DIM = 64
DA_V_DIM = 2 * DA_HEAD_DIM
ROPE_THETA = 10000.0
HG_HEADS = 8
HG_CHUNK = 64
HG_SUB = 16
LRU_BLOCK_W = 128
CONV_WIDTH = 4
LRU_C = 8.0
N_EXPERTS = 8
TOP_K = 2


def _params(*sem):
    return pltpu.CompilerParams(dimension_semantics=sem, vmem_limit_bytes=VMEM_LIMIT_BYTES)


def _sigmoid(x):
    return 1.0 / (1.0 + jnp.exp(-x))


def _silu(x):
    return x * _sigmoid(x)


def _sigmoid_tanh(x):
    return 0.5 * jnp.tanh(0.5 * x) + 0.5


def _modulated(x, g, shift, scale):
    ms = jnp.mean(x * x, axis=-1, keepdims=True)
    return x * lax.rsqrt(ms + EPS) * g * (1.0 + scale) + shift


def _mods_kernel(c_ref, w_ref, b_ref, o_ref):
    c = c_ref[...]
    o_ref[...] = jnp.dot(_silu(c), w_ref[...], preferred_element_type=F32, precision=HIGHEST) + b_ref[...]


def _mods(c, ada_w, ada_b):
    depth, d, n = ada_w.shape
    b = c.shape[0]
    tn = 2048
    return pl.pallas_call(
        _mods_kernel,
        grid=(depth, n // tn),
        in_specs=[
            pl.BlockSpec((b, d), lambda l, j: (0, 0)),
            pl.BlockSpec((None, d, tn), lambda l, j: (l, 0, j)),
            pl.BlockSpec((None, 1, tn), lambda l, j: (l, 0, j)),
        ],
        out_specs=pl.BlockSpec((None, b, tn), lambda l, j: (l, 0, j)),
        out_shape=jax.ShapeDtypeStruct((depth, b, n), F32),
        compiler_params=_params("parallel", "parallel"),
        name="adaln_mods",
    )(c, ada_w, ada_b.reshape(depth, 1, n))


class _RowVec:
    def __init__(self, vec, tm, seq):
        b, d = vec.shape
        assert seq % tm == 0
        tiles_per_batch = seq // tm
        self.array = vec.reshape(b, 1, d)
        self.spec = pl.BlockSpec((None, 1, d), lambda i, *_: (i // tiles_per_batch, 0, 0))


def _const_spec(shape):
    nd = len(shape)
    return pl.BlockSpec(shape, lambda *_: (0,) * nd)


def _nm_matmul_kernel(x_ref, g_ref, sh_ref, sc_ref, w_ref, o_ref, h_ref):
    @pl.when(pl.program_id(1) == 0)
    def _():
        h_ref[...] = _modulated(x_ref[...], g_ref[...], sh_ref[...], sc_ref[...]).astype(BF16)

    o_ref[...] = jnp.dot(h_ref[...], w_ref[...], preferred_element_type=F32).astype(o_ref.dtype)


def _nm_matmul(x, g, shift, scale, w, *, seq, tm, tn, out_dtype):
    t, d = x.shape
    n = w.shape[1]
    assert t % tm == 0 and n % tn == 0
    sh = _RowVec(shift, tm, seq)
    sc = _RowVec(scale, tm, seq)
    weight_buffers = pl.Buffered(1) if n == tn else pl.Buffered(2)
    return pl.pallas_call(
        _nm_matmul_kernel,
        grid=(t // tm, n // tn),
        in_specs=[
            pl.BlockSpec((tm, d), lambda i, j: (i, 0)),
            _const_spec((1, d)),
            sh.spec, sc.spec,
            pl.BlockSpec((d, tn), lambda i, j: (0, j), pipeline_mode=weight_buffers),
        ],
        out_specs=pl.BlockSpec((tm, tn), lambda i, j: (i, j)),
        out_shape=jax.ShapeDtypeStruct((t, n), out_dtype),
        scratch_shapes=[pltpu.VMEM((tm, d), BF16)],
        compiler_params=_params("parallel", "arbitrary"),
        name="modulate_matmul",
    )(x, g.reshape(1, d), sh.array, sc.array, w)


def _out_proj_kernel(*refs, prologue, n_in):
    ins = refs[:n_in]
    w_ref, x_ref, gate_ref, o_ref = refs[n_in:]
    a = prologue(*ins)
    mix = jnp.dot(a, w_ref[...], preferred_element_type=F32)
    o_ref[...] = x_ref[...] + gate_ref[...] * mix


def _out_proj(prologue, ins, in_specs, w, x, gate, *, seq, tm):
    t, d = x.shape
    k = w.shape[0]
    gv = _RowVec(gate, tm, seq)
    return pl.pallas_call(
        functools.partial(_out_proj_kernel, prologue=prologue, n_in=len(ins)),
        grid=(t // tm,),
        in_specs=list(in_specs) + [
            _const_spec((k, d)),
            pl.BlockSpec((tm, d), lambda i: (i, 0)),
            gv.spec,
        ],
        out_specs=pl.BlockSpec((tm, d), lambda i: (i, 0)),
        out_shape=jax.ShapeDtypeStruct((t, d), F32),
        compiler_params=_params("parallel"),
        name="out_proj_residual",
    )(*ins, w, x, gv.array)


QK_PREP_WIDTH = 2 * LANES


def _rope_constants(seq):
    half = DA_HEAD_DIM // 2
    inv = 1.0 / (ROPE_THETA ** (jnp.arange(0, DA_HEAD_DIM, 2, dtype=F32) / DA_HEAD_DIM))
    ang = jnp.arange(seq, dtype=F32)[:, None] * inv[None, :]
    reps = QK_PREP_WIDTH // half
    cos = jnp.tile(jnp.cos(ang), (1, reps))
    sin = jnp.tile(jnp.sin(ang), (1, reps))
    lane = np.arange(QK_PREP_WIDTH)
    first = (lane % DA_HEAD_DIM) < half
    rot = np.zeros((QK_PREP_WIDTH, QK_PREP_WIDTH), np.float32)
    rot[(lane + half)[first], lane[first]] = -1.0
    rot[(lane - half)[~first], lane[~first]] = 1.0
    group = (lane[:, None] // DA_HEAD_DIM == lane[None, :] // DA_HEAD_DIM).astype(np.float32)
    return cos, sin, jnp.asarray(rot, BF16), jnp.asarray(group, BF16)


def _attn_in_kernel(x_ref, g_ref, sh_ref, sc_ref, w_ref, gain_ref, cos_ref, sin_ref, rot_ref, grp_ref,
                    o_ref, h_ref):
    j = pl.program_id(1)

    @pl.when(j == 0)
    def _():
        h_ref[...] = _modulated(x_ref[...], g_ref[...], sh_ref[...], sc_ref[...]).astype(BF16)

    y_all = jnp.dot(h_ref[...], w_ref[...], preferred_element_type=F32)

    @pl.when(j == 2)
    def _():
        o_ref[...] = y_all.astype(BF16)

    @pl.when(j < 2)
    def _():
        out_scale = jnp.where(j == 0, DA_HEAD_DIM ** -0.5 * math.log2(math.e), 1.0).astype(F32)
        cos = cos_ref[...]
        sin = sin_ref[...]
        gain = gain_ref[...]
        for h in range(y_all.shape[1] // QK_PREP_WIDTH):
            sl = slice(h * QK_PREP_WIDTH, (h + 1) * QK_PREP_WIDTH)
            y = y_all[:, sl]
            ss = jnp.dot((y * y).astype(BF16), grp_ref[...], preferred_element_type=F32)
            n = y * lax.rsqrt(ss * (1.0 / DA_HEAD_DIM) + EPS) * gain
            r = jnp.dot(n.astype(BF16), rot_ref[...], preferred_element_type=F32)
            o_ref[:, sl] = ((n * cos + r * sin) * out_scale).astype(BF16)


def _attn_in_proj(x, g, shift, scale, w, q_gain, k_gain, cos, sin, rot, grp, *, seq, tm):
    t, d = x.shape
    width = DA_HEADS * LANES
    assert w.shape[1] == 3 * width
    reps = QK_PREP_WIDTH // DA_HEAD_DIM
    gains = jnp.stack([jnp.tile(q_gain, reps), jnp.tile(k_gain, reps)]).reshape(2, 1, QK_PREP_WIDTH).astype(F32)
    tiles_per_seq = seq // tm
    sh = _RowVec(shift, tm, seq)
    sc = _RowVec(scale, tm, seq)
    return pl.pallas_call(
        _attn_in_kernel,
        grid=(t // tm, 3),
        in_specs=[
            pl.BlockSpec((tm, d), lambda i, j: (i, 0)),
            _const_spec((1, d)),
            sh.spec, sc.spec,
            pl.BlockSpec((d, width), lambda i, j: (0, j)),
            pl.BlockSpec((None, 1, QK_PREP_WIDTH), lambda i, j: (jnp.minimum(j, 1), 0, 0)),
            pl.BlockSpec((tm, QK_PREP_WIDTH), lambda i, j: (i % tiles_per_seq, 0)),
            pl.BlockSpec((tm, QK_PREP_WIDTH), lambda i, j: (i % tiles_per_seq, 0)),
            _const_spec((QK_PREP_WIDTH, QK_PREP_WIDTH)),
            _const_spec((QK_PREP_WIDTH, QK_PREP_WIDTH)),
        ],
        out_specs=pl.BlockSpec((None, tm, width), lambda i, j: (j, i, 0)),
        out_shape=jax.ShapeDtypeStruct((3, t, width), BF16),
        scratch_shapes=[pltpu.VMEM((tm, d), BF16)],
        compiler_params=_params("parallel", "arbitrary"),
        name="attn_in_proj",
    )(x, g.reshape(1, d), sh.array, sc.array, w, gains, cos, sin, rot, grp)


def _flash_kernel(q_ref, k_ref, v_ref, lam_ref, subln_ref, o_ref,
                  vt_ref, kmax_ref, bound_ref, mrun_ref, acc_ref, den_ref, *, tq, tk, lambda_init):
    seq = k_ref.shape[0]
    n_kv = seq // tk
    lane = lax.broadcasted_iota(jnp.int32, (LANES, LANES), 0)
    first_half = jnp.where(lane < DA_HEAD_DIM, 1.0, 0.0).astype(BF16)
    second_half = jnp.where(lane >= DA_HEAD_DIM, 1.0, 0.0).astype(BF16)

    @pl.when(pl.program_id(2) == 0)
    def _():
        def prep(j, carry):
            n1, n2 = carry
            r0 = pl.multiple_of(j * tk, tk)
            vt_ref[:, pl.ds(r0, tk)] = v_ref[pl.ds(r0, tk), :].astype(F32).T.astype(BF16)
            kf = k_ref[pl.ds(r0, tk), :].astype(F32)
            sq = (kf * kf).astype(BF16)
            s1 = jnp.dot(sq, first_half, preferred_element_type=F32)
            s2 = jnp.dot(sq, second_half, preferred_element_type=F32)
            return (jnp.maximum(n1, jnp.max(s1, axis=0, keepdims=True)),
                    jnp.maximum(n2, jnp.max(s2, axis=0, keepdims=True)))

        zero = jnp.zeros((1, LANES), F32)
        n1, n2 = lax.fori_loop(0, n_kv, prep, (zero, zero))
        kmax_ref[0:1, :] = n1
        kmax_ref[1:2, :] = n2

    q = q_ref[...]
    qlane = lax.broadcasted_iota(jnp.int32, q.shape, 1)
    zero = jnp.zeros_like(q)
    qq = jnp.concatenate([jnp.where(qlane < DA_HEAD_DIM, q, zero),
                          jnp.where(qlane >= DA_HEAD_DIM, q, zero)], axis=0)
    qf = qq.astype(F32)
    nt = (((1,), (1,)), ((), ()))
    qn = lax.dot_general(jnp.ones((8, LANES), BF16), (qf * qf).astype(BF16), nt,
                         preferred_element_type=F32)[0:1]
    col = lax.broadcasted_iota(jnp.int32, qn.shape, 1)
    kn = jnp.where(col < tq, kmax_ref[0:1, 0:1], kmax_ref[1:2, 0:1])
    bound = jnp.sqrt(qn * kn) * 1.01
    bound_ref[...] = bound
    acc_ref[...] = jnp.zeros(acc_ref.shape, F32)
    den_ref[...] = jnp.zeros(den_ref.shape, F32)
    safe = jnp.max(bound) <= 60.0

    def scores(j):
        r0 = pl.multiple_of(j * tk, tk)
        st = lax.dot_general(k_ref[pl.ds(r0, tk), :], qq, nt, preferred_element_type=F32)
        return st, vt_ref[:, pl.ds(r0, tk)]

    @pl.when(safe)
    def _():
        def body(j, carry):
            st, vt = scores(j)
            p = jnp.exp2(st - bound_ref[...])
            den_ref[...] += jnp.sum(p, axis=0, keepdims=True)
            acc_ref[...] += jnp.dot(vt, p.astype(BF16), preferred_element_type=F32)
            return carry

        lax.fori_loop(0, n_kv, body, 0, unroll=4)

    @pl.when(jnp.logical_not(safe))
    def _():
        mrun_ref[...] = jnp.full(mrun_ref.shape, -jnp.inf, F32)

        def body(j, carry):
            st, vt = scores(j)
            m_old = mrun_ref[...]
            m_new = jnp.maximum(m_old, jnp.max(st, axis=0, keepdims=True))
            alpha = jnp.exp2(m_old - m_new)
            p = jnp.exp2(st - m_new)
            den_ref[...] = alpha * den_ref[...] + jnp.sum(p, axis=0, keepdims=True)
            acc_ref[...] = alpha * acc_ref[...] + jnp.dot(vt, p.astype(BF16), preferred_element_type=F32)
            mrun_ref[...] = m_new
            return carry

        lax.fori_loop(0, n_kv, body, 0)

    lf = lam_ref[...]
    lam = (jnp.exp(jnp.sum(lf[0:1] * lf[1:2], axis=1, keepdims=True))
           - jnp.exp(jnp.sum(lf[2:3] * lf[3:4], axis=1, keepdims=True)) + lambda_init)
    inv = 1.0 / den_ref[...]
    o1 = acc_ref[:, 0:tq] * inv[:, 0:tq]
    o2 = acc_ref[:, tq:2 * tq] * inv[:, tq:2 * tq]
    ot = o1 - lam * o2
    ms = jnp.mean(ot * ot, axis=0, keepdims=True)
    o = (ot * lax.rsqrt(ms + EPS)).T * (subln_ref[...] * (1.0 - lambda_init))
    o_ref[...] = o.astype(BF16)


def _flash(qkv, lam_vecs, subln_g, *, batch, seq, lambda_init, tq, tk):
    t = batch * seq
    width = DA_HEADS * LANES
    q_tiles = seq // tq
    return pl.pallas_call(
        functools.partial(_flash_kernel, tq=tq, tk=tk, lambda_init=lambda_init),
        grid=(batch, DA_HEADS, q_tiles),
        in_specs=[
            pl.BlockSpec((None, tq, LANES), lambda b, h, i: (0, b * q_tiles + i, h)),
            pl.BlockSpec((None, seq, LANES), lambda b, h, i: (1, b, h)),
            pl.BlockSpec((None, seq, LANES), lambda b, h, i: (2, b, h)),
            _const_spec((4, DA_HEAD_DIM)),
            _const_spec((1, DA_V_DIM)),
        ],
        out_specs=pl.BlockSpec((tq, LANES), lambda b, h, i: (b * q_tiles + i, h)),
        out_shape=jax.ShapeDtypeStruct((t, width), BF16),
        scratch_shapes=[
            pltpu.VMEM((LANES, seq), BF16),
            pltpu.VMEM((8, LANES), F32),
            pltpu.VMEM((1, 2 * tq), F32),
            pltpu.VMEM((1, 2 * tq), F32),
            pltpu.VMEM((LANES, 2 * tq), F32),
            pltpu.VMEM((1, 2 * tq), F32),
        ],
        compiler_params=_params("parallel", "parallel", "arbitrary"),
        name="diff_flash_attention",
    )(qkv, qkv, qkv, lam_vecs.astype(F32), subln_g.reshape(1, DA_V_DIM).astype(F32))


def _identity_prologue(o_ref):
    return o_ref[...]


def _diff_attention_layer(x, g, shift, scale, gate, w_in, q_norm, k_norm, lam_vecs, subln_g, w_out,
                          lambda_init, *, batch, seq):
    cos, sin, rot, grp = _rope_constants(seq)
    qkv = _attn_in_proj(x, g, shift, scale, w_in.astype(BF16), q_norm, k_norm, cos, sin, rot, grp,
                        seq=seq, tm=min(1024, seq))
    o = _flash(qkv, lam_vecs, subln_g, batch=batch, seq=seq, lambda_init=lambda_init,
               tq=min(1024, seq), tk=min(2048, seq))
    tm = min(512, seq)
    return _out_proj(_identity_prologue, [o], [pl.BlockSpec((tm, o.shape[1]), lambda i: (i, 0))],
                     w_out.astype(BF16), x, gate, seq=seq, tm=tm)


def _hgrn_scan_kernel(q_ref, f_ref, v_ref, lb_ref, o_ref, st_ref, *, reverse, layer):
    c = HG_CHUNK
    n_chunks = q_ref.shape[0] // c

    @pl.when(pl.program_id(1) == 0)
    def _():
        st_ref[...] = jnp.zeros(st_ref.shape, F32)

    lbw = lb_ref[...]
    e = jnp.exp(lbw - jnp.max(lbw, axis=0, keepdims=True))
    lb = jnp.sum(e[1:layer + 1], axis=0, keepdims=True) / jnp.sum(e, axis=0, keepdims=True)

    row = lax.broadcasted_iota(jnp.int32, (c, c), 0)
    col = lax.broadcasted_iota(jnp.int32, (c, c), 1)
    tri = (col >= row) if reverse else (col <= row)
    tri = tri.astype(F32)
    n_sub = c // HG_SUB
    half = HG_SUB // 2
    col8 = lax.broadcasted_iota(jnp.int32, (half, c), 1)
    row8 = lax.broadcasted_iota(jnp.int32, (half, c), 0)

    def key_rows(i, r):
        t0 = i * HG_SUB + r * half
        return range(t0, (i + 1) * HG_SUB) if reverse else range(i * HG_SUB, t0 + half)

    def placement(i, r, s):
        t0 = i * HG_SUB + r * half
        reached = (row8 + t0 <= s) if reverse else (row8 + t0 >= s)
        return jnp.logical_and(col8 == s, reached)

    place = {(i, r, s): placement(i, r, s) for i in range(n_sub) for r in range(2) for s in key_rows(i, r)}

    def chunk(ci, carry):
        cidx = (n_chunks - 1 - ci) if reverse else ci
        r0 = pl.multiple_of(cidx * c, c)
        qs = _silu(q_ref[pl.ds(r0, c), :])
        forget = lb + (1.0 - lb) * _sigmoid(f_ref[pl.ds(r0, c), :])
        kk = 1.0 - forget
        vb = v_ref[pl.ds(r0, c), :].astype(BF16)
        b = jnp.dot(tri, jnp.log2(forget), preferred_element_type=F32, precision=HIGHEST)
        b_edge = b[0:1] if reverse else b[c - 1:c]
        q_in = (qs * jnp.exp2(b)).astype(BF16)
        k_out = (kk * jnp.exp2(b_edge - b)).astype(BF16)
        dec = jnp.exp2(b_edge)
        key_shift = b - jnp.log2(kk)

        pieces = [[jnp.zeros((half, c), F32) for _ in range(2 * n_sub)] for _ in range(HG_HEADS)]
        for i in range(n_sub):
            for r in range(2):
                t0 = i * HG_SUB + r * half
                q8 = qs[t0:t0 + half]
                b8 = b[t0:t0 + half]
                for s in key_rows(i, r):
                    term = q8 * jnp.exp2(b8 - key_shift[s:s + 1])
                    for h in range(HG_HEADS):
                        a_col = jnp.sum(term[:, h * LANES:(h + 1) * LANES], axis=1, keepdims=True)
                        pieces[h][2 * i + r] = jnp.where(place[i, r, s], a_col, pieces[h][2 * i + r])

        for h in range(HG_HEADS):
            sl = slice(h * LANES, (h + 1) * LANES)
            blocks = []
            for i in range(n_sub):
                rows = slice(i * HG_SUB, (i + 1) * HG_SUB)
                passed = slice((i + 1) * HG_SUB, c) if reverse else slice(0, i * HG_SUB)
                n_passed = passed.stop - passed.start
                if n_passed == 0:
                    blocks.append(jnp.zeros((HG_SUB, c), F32))
                    continue
                ref_row = (i + 1) * HG_SUB if reverse else i * HG_SUB - 1
                b_ref_row = b[ref_row:ref_row + 1, sl]
                q_t = (qs[rows, sl] * jnp.exp2(b[rows, sl] - b_ref_row)).astype(BF16)
                k_p = (kk[passed, sl] * jnp.exp2(b_ref_row - b[passed, sl])).astype(BF16)
                rest = jnp.zeros((c - n_passed, LANES), BF16)
                k_t = jnp.concatenate([rest, k_p] if reverse else [k_p, rest], axis=0)
                blocks.append(lax.dot_general(q_t, k_t, (((1,), (1,)), ((), ())),
                                              preferred_element_type=F32))
            a = jnp.concatenate(pieces[h], axis=0) + jnp.concatenate(blocks, axis=0)
            st = st_ref[h]
            o_h = jnp.dot(a.astype(BF16), vb[:, sl], preferred_element_type=F32)
            o_h = o_h + lax.dot_general(q_in[:, sl], st.astype(BF16), (((1,), (1,)), ((), ())),
                                        preferred_element_type=F32)
            o_ref[pl.ds(r0, c), sl] = o_h.astype(o_ref.dtype)
            upd = lax.dot_general(vb[:, sl], k_out[:, sl], (((0,), (0,)), ((), ())),
                                  preferred_element_type=F32)
            st_ref[h] = st * dec[:, sl] + upd
        return carry

    lax.fori_loop(0, n_chunks, chunk, 0, unroll=4)


def _hgrn_scan(proj, lower_bounds, *, batch, seq, layer, reverse, tt):
    t = batch * seq
    width = HG_HEADS * LANES
    nt = seq // tt
    depth = lower_bounds.shape[1]
    d_idx = 1 if reverse else 0

    def rows(b, i):
        return b * nt + ((nt - 1 - i) if reverse else i)

    return pl.pallas_call(
        functools.partial(_hgrn_scan_kernel, reverse=reverse, layer=layer),
        grid=(batch, nt),
        in_specs=[
            pl.BlockSpec((tt, width), lambda b, i: (rows(b, i), 0)),
            pl.BlockSpec((tt, width), lambda b, i: (rows(b, i), 1 + d_idx)),
            pl.BlockSpec((tt, width), lambda b, i: (rows(b, i), 3)),
            pl.BlockSpec((None, depth, width), lambda b, i: (d_idx, 0, 0)),
        ],
        out_specs=pl.BlockSpec((tt, width), lambda b, i: (rows(b, i), 0)),
        out_shape=jax.ShapeDtypeStruct((t, width), BF16),
        scratch_shapes=[pltpu.VMEM((HG_HEADS, LANES, LANES), F32)],
        compiler_params=_params("parallel", "arbitrary"),
        name="hgrn2_scan_bwd" if reverse else "hgrn2_scan_fwd",
    )(proj, proj, proj, lower_bounds.astype(F32))


def _hgrn_prologue(of_ref, ob_ref, g_ref, ng_ref):
    ng = ng_ref[...]
    outs = []
    for h in range(HG_HEADS):
        sl = slice(h * LANES, (h + 1) * LANES)
        o = of_ref[:, sl].astype(F32) + ob_ref[:, sl].astype(F32)
        ms = jnp.mean(o * o, axis=-1, keepdims=True)
        outs.append((o * lax.rsqrt(ms + EPS) * ng * _silu(g_ref[:, sl])).astype(BF16))
    return jnp.concatenate(outs, axis=1)


def _hgrn_layer(x, g, shift, scale, gate, w_in, lower_bounds, norm_g, w_out, layer, *, batch, seq):
    width = HG_HEADS * LANES
    proj = _nm_matmul(x, g, shift, scale, w_in.astype(BF16), seq=seq,
                      tm=min(512, seq), tn=w_in.shape[1], out_dtype=F32)
    tt = min(512, seq)
    o_f = _hgrn_scan(proj, lower_bounds, batch=batch, seq=seq, layer=layer, reverse=False, tt=tt)
    o_b = _hgrn_scan(proj, lower_bounds, batch=batch, seq=seq, layer=layer, reverse=True, tt=tt)
    tm = min(512, seq)
    ins = [o_f, o_b, proj, norm_g.reshape(1, LANES).astype(F32)]
    specs = [
        pl.BlockSpec((tm, width), lambda i: (i, 0)),
        pl.BlockSpec((tm, width), lambda i: (i, 0)),
        pl.BlockSpec((tm, width), lambda i: (i, 4)),
        _const_spec((1, LANES)),
    ]
    return _out_proj(_hgrn_prologue, ins, specs, w_out.astype(BF16), x, gate, seq=seq, tm=tm)


def _lru_scan_kernel(prev_ref, main_ref, next_ref, cw_ref, cb_ref, wg_ref, ba_ref, bx_ref, lam_ref,
                     o_ref, a_ref, u_ref, h_ref, *, batch, nt):
    d = pl.program_id(0)
    i = pl.program_id(1)
    ti = i + d * (nt - 1 - 2 * i)
    rows, width = main_ref.shape
    tt = rows // batch

    @pl.when(i == 0)
    def _():
        h_ref[...] = jnp.zeros(h_ref.shape, F32)

    prev = jnp.where(ti == 0, 0.0, prev_ref[...])
    nxt = jnp.where(ti == nt - 1, 0.0, next_ref[...])
    ext = jnp.concatenate([prev, main_ref[...], nxt], axis=0)
    cw = cw_ref[...]
    xc = cb_ref[...] + cw[0:1] * ext[0:rows]
    for j in range(1, CONV_WIDTH):
        xc = xc + cw[j:j + 1] * ext[j * batch:j * batch + rows]

    neg_softplus = -LRU_C * (jnp.maximum(-lam_ref[...], 0.0) + jnp.log1p(jnp.exp(-jnp.abs(lam_ref[...]))))
    for n in range(width // LRU_BLOCK_W):
        sl = slice(n * LRU_BLOCK_W, (n + 1) * LRU_BLOCK_W)
        xb = xc[:, sl]
        gates = jnp.dot(xb.astype(BF16), wg_ref[n], preferred_element_type=F32)
        r = _sigmoid_tanh(gates[:, :LRU_BLOCK_W] + ba_ref[:, sl])
        ig = _sigmoid_tanh(gates[:, LRU_BLOCK_W:] + bx_ref[:, sl])
        a = jnp.exp(r * neg_softplus[:, sl])
        a_ref[:, sl] = a
        t = 1.0 - a * a
        u_ref[:, sl] = (t * lax.rsqrt(jnp.maximum(t, 1e-30))) * (ig * xb)

    def step(t, h):
        tl = t + d * (tt - 1 - 2 * t)
        r0 = pl.multiple_of(tl * batch, batch)
        h = a_ref[pl.ds(r0, batch), :] * h + u_ref[pl.ds(r0, batch), :]
        o_ref[pl.ds(r0, batch), :] = h
        return h

    h_ref[...] = lax.fori_loop(0, tt, step, h_ref[...], unroll=4)


def _lru_scan(proj, conv_w, conv_b, w_a, b_a, w_x, b_x, lam, *, batch, seq, tt):
    t = batch * seq
    width = conv_w.shape[1]
    n_blocks = width // LRU_BLOCK_W
    nt = seq // tt
    rows = tt * batch
    assert tt % 2 == 0

    def tile(d, i):
        return i + d * (nt - 1 - 2 * i)

    wg = jnp.concatenate([w_a, w_x], axis=-1).astype(BF16)
    vec = lambda a: a.reshape(2, 1, width).astype(F32)
    return pl.pallas_call(
        functools.partial(_lru_scan_kernel, batch=batch, nt=nt),
        grid=(2, nt),
        in_specs=[
            pl.BlockSpec((batch, width), lambda d, i: (jnp.maximum(tile(d, i) * tt - 1, 0), 1)),
            pl.BlockSpec((rows, width), lambda d, i: (tile(d, i), 1)),
            pl.BlockSpec((2 * batch, width),
                         lambda d, i: (jnp.minimum((tile(d, i) + 1) * (tt // 2), seq // 2 - 1), 1)),
            _const_spec((CONV_WIDTH, width)),
            _const_spec((1, width)),
            pl.BlockSpec((None, n_blocks, LRU_BLOCK_W, 2 * LRU_BLOCK_W), lambda d, i: (d, 0, 0, 0)),
            pl.BlockSpec((None, 1, width), lambda d, i: (d, 0, 0)),
            pl.BlockSpec((None, 1, width), lambda d, i: (d, 0, 0)),
            pl.BlockSpec((None, 1, width), lambda d, i: (d, 0, 0)),
        ],
        out_specs=pl.BlockSpec((None, rows, width), lambda d, i: (d, tile(d, i), 0)),
        out_shape=jax.ShapeDtypeStruct((2, t, width), F32),
        scratch_shapes=[
            pltpu.VMEM((rows, width), F32),
            pltpu.VMEM((rows, width), F32),
            pltpu.VMEM((batch, width), F32),
        ],
        compiler_params=_params("arbitrary", "arbitrary"),
        name="rglru_scan",
    )(proj, proj, proj, conv_w.astype(F32), conv_b.reshape(1, width).astype(F32), wg,
      vec(b_a), vec(b_x), vec(lam))


def _gelu_tanh(x):
    return 0.5 * x * (1.0 + jnp.tanh(math.sqrt(2.0 / math.pi) * (x + 0.044715 * (x * x * x))))


def _lru_in_kernel(x_ref, g_ref, sh_ref, sc_ref, w_ref, o_ref, xm_ref, h_ref, *, batch, ts):
    n_lane_blocks = x_ref.shape[2] // LANES
    for b in range(batch):
        xm = _modulated(x_ref[b], g_ref[...], sh_ref[b:b + 1, :], sc_ref[b:b + 1, :])
        for c in range(n_lane_blocks):
            xm_ref[c, b * ts:(b + 1) * ts, :] = xm[:, c * LANES:(c + 1) * LANES]
    for s in range(ts):
        for c in range(n_lane_blocks):
            h_ref[s * batch:(s + 1) * batch, c * LANES:(c + 1) * LANES] = \
                xm_ref[c, pl.ds(s, batch, stride=ts), :]
    o_ref[...] = jnp.dot(h_ref[...].astype(BF16), w_ref[...], preferred_element_type=F32)


def _lru_in_proj(x, g, shift, scale, w, *, batch, seq, ts):
    t, d = x.shape
    n = w.shape[1]
    rows = ts * batch
    return pl.pallas_call(
        functools.partial(_lru_in_kernel, batch=batch, ts=ts),
        grid=(seq // ts,),
        in_specs=[
            pl.BlockSpec((batch, ts, d), lambda i: (0, i, 0)),
            _const_spec((1, d)),
            _const_spec((batch, d)),
            _const_spec((batch, d)),
            _const_spec((d, n)),
        ],
        out_specs=pl.BlockSpec((rows, n), lambda i: (i, 0)),
        out_shape=jax.ShapeDtypeStruct((t, n), F32),
        scratch_shapes=[pltpu.VMEM((d // LANES, rows, LANES), F32), pltpu.VMEM((rows, d), F32)],
        compiler_params=_params("parallel"),
        name="rglru_in_proj",
    )(x.reshape(batch, seq, d), g.reshape(1, d), shift, scale, w)


def _lru_out_kernel(hf_ref, hb_ref, y_ref, w_ref, x_ref, gate_ref, o_ref, mix_ref, *, batch, ts):
    a = ((hf_ref[...] + hb_ref[...]) * _gelu_tanh(y_ref[...])).astype(BF16)
    mix = jnp.dot(a, w_ref[...], preferred_element_type=F32)
    n_lane_blocks = mix.shape[1] // LANES
    for c in range(n_lane_blocks):
        mix_ref[c] = mix[:, c * LANES:(c + 1) * LANES]
    for b in range(batch):
        for c in range(n_lane_blocks):
            sl = slice(c * LANES, (c + 1) * LANES)
            o_ref[b, :, sl] = x_ref[b, :, sl] + gate_ref[b:b + 1, sl] * mix_ref[c, pl.ds(b, ts, stride=batch), :]


def _lru_out_proj(h2, proj, w, x, gate, *, batch, seq, ts):
    t, d = x.shape
    width = w.shape[0]
    rows = ts * batch
    out = pl.pallas_call(
        functools.partial(_lru_out_kernel, batch=batch, ts=ts),
        grid=(seq // ts,),
        in_specs=[
            pl.BlockSpec((None, rows, width), lambda i: (0, i, 0)),
            pl.BlockSpec((None, rows, width), lambda i: (1, i, 0)),
            pl.BlockSpec((rows, width), lambda i: (i, 0)),
            _const_spec((width, d)),
            pl.BlockSpec((batch, ts, d), lambda i: (0, i, 0)),
            _const_spec((batch, d)),
        ],
        out_specs=pl.BlockSpec((batch, ts, d), lambda i: (0, i, 0)),
        out_shape=jax.ShapeDtypeStruct((batch, seq, d), F32),
        scratch_shapes=[pltpu.VMEM((d // LANES, rows, LANES), F32)],
        compiler_params=_params("parallel"),
        name="rglru_out_proj",
    )(h2, h2, proj, w, x.reshape(batch, seq, d), gate)
    return out.reshape(t, d)


def _rglru_layer(x, g, shift, scale, gate, w_in, conv_w, conv_b, w_a, b_a, w_x, b_x, lam, w_out,
                 *, batch, seq):
    width = conv_w.shape[1]
    ts = min(64, seq)
    proj = _lru_in_proj(x, g, shift, scale, w_in.astype(BF16), batch=batch, seq=seq, ts=ts)
    h2 = _lru_scan(proj, conv_w, conv_b, w_a, b_a, w_x, b_x, lam, batch=batch, seq=seq, tt=min(64, seq))
    return _lru_out_proj(h2, proj, w_out.astype(BF16), x, gate, batch=batch, seq=seq, ts=ts)


def _ffn_kernel(x_ref, g_ref, sh_ref, sc_ref, wg_ref, wu_ref, wd_ref, gate_ref, o_ref, h_ref, acc_ref):
    j = pl.program_id(1)

    @pl.when(j == 0)
    def _():
        h_ref[...] = _modulated(x_ref[...], g_ref[...], sh_ref[...], sc_ref[...]).astype(BF16)
        acc_ref[...] = jnp.zeros(acc_ref.shape, F32)

    h = h_ref[...]
    gt = jnp.dot(h, wg_ref[...], preferred_element_type=F32)
    up = jnp.dot(h, wu_ref[...], preferred_element_type=F32)
    act = (_silu(gt) * up).astype(BF16)
    acc_ref[...] += jnp.dot(act, wd_ref[...], preferred_element_type=F32)

    @pl.when(j == pl.num_programs(1) - 1)
    def _():
        o_ref[...] = x_ref[...] + gate_ref[...] * acc_ref[...]


def _dense_ffn(x, g, shift, scale, gate, w_gu, w_down, layer, *, seq, tm, tf):
    t, d = x.shape
    f = w_down.shape[1]
    assert f % tf == 0 and seq % tm == 0
    nf = f // tf
    sh = _RowVec(shift, tm, seq)
    sc = _RowVec(scale, tm, seq)
    gv = _RowVec(gate, tm, seq)
    weight_buffers = pl.Buffered(1) if nf == 1 else pl.Buffered(2)
    return pl.pallas_call(
        _ffn_kernel,
        grid=(t // tm, nf),
        in_specs=[
            pl.BlockSpec((tm, d), lambda i, j: (i, 0)),
            _const_spec((1, d)),
            sh.spec, sc.spec,
            pl.BlockSpec((None, d, tf), lambda i, j: (layer, 0, j), pipeline_mode=weight_buffers),
            pl.BlockSpec((None, d, tf), lambda i, j: (layer, 0, nf + j), pipeline_mode=weight_buffers),
            pl.BlockSpec((None, tf, d), lambda i, j: (layer, j, 0), pipeline_mode=weight_buffers),
            gv.spec,
        ],
        out_specs=pl.BlockSpec((tm, d), lambda i, j: (i, 0)),
        out_shape=jax.ShapeDtypeStruct((t, d), F32),
        scratch_shapes=[pltpu.VMEM((tm, d), BF16), pltpu.VMEM((tm, d), F32)],
        compiler_params=_params("parallel", "arbitrary"),
        name="dense_swiglu",
    )(x, g.reshape(1, d), sh.array, sc.array, w_gu, w_gu, w_down, gv.array)


MOE_BLOCK = 512
MOE_CHUNK = 16
MOE_TILE = 1024


def _route_kernel(x_ref, g_ref, sh_ref, sc_ref, rhi_ref, rlo_ref, xs_ref, meta_ref, cnt_ref):
    tb = x_ref.shape[0]
    local_rows = xs_ref.shape[0]
    hf = _modulated(x_ref[...], g_ref[...], sh_ref[...], sc_ref[...])
    h_hi = hf.astype(BF16)
    h_lo = (hf - h_hi.astype(F32)).astype(BF16)
    logits = (jnp.dot(h_hi, rhi_ref[...], preferred_element_type=F32)
              + jnp.dot(h_lo, rhi_ref[...], preferred_element_type=F32)
              + jnp.dot(h_hi, rlo_ref[...], preferred_element_type=F32))
    lane = lax.broadcasted_iota(jnp.int32, logits.shape, 1).astype(F32)
    logits = jnp.where(lane < N_EXPERTS, logits, -jnp.inf)
    m1 = jnp.max(logits, axis=1, keepdims=True)
    e1 = jnp.min(jnp.where(logits == m1, lane, float(LANES)), axis=1, keepdims=True)
    rest = jnp.where(lane == e1, -jnp.inf, logits)
    m2 = jnp.max(rest, axis=1, keepdims=True)
    e2 = jnp.min(jnp.where(rest == m2, lane, float(LANES)), axis=1, keepdims=True)
    gate2 = 1.0 / (1.0 + jnp.exp(m1 - m2))
    gate1 = 1.0 - gate2
    hot1 = (lane == e1).astype(F32)
    hot2 = (lane == e2).astype(F32)
    sel = hot1 + hot2
    row = lax.broadcasted_iota(jnp.int32, (tb, tb), 0)
    col = lax.broadcasted_iota(jnp.int32, (tb, tb), 1)
    before = (col < row).astype(BF16)
    seen = jnp.dot(before, sel.astype(BF16), preferred_element_type=F32)
    chunks = jnp.floor((jnp.sum(sel, axis=0, keepdims=True) + (MOE_CHUNK - 1.0)) * (1.0 / MOE_CHUNK))
    cnt_ref[...] = chunks
    lrow = lax.broadcasted_iota(jnp.int32, (LANES, LANES), 0)
    lcol = lax.broadcasted_iota(jnp.int32, (LANES, LANES), 1)
    lower_experts = (lrow < lcol).astype(BF16)
    first_chunk = jnp.dot(jnp.broadcast_to(chunks, (8, LANES)).astype(BF16), lower_experts,
                          preferred_element_type=F32)[0:1]
    base = first_chunk * float(MOE_CHUNK) + seen
    ld1 = jnp.sum(hot1 * base, axis=1, keepdims=True)
    ld2 = jnp.sum(hot2 * base, axis=1, keepdims=True)
    meta = jnp.zeros(logits.shape, F32)
    for k, val in enumerate((e1, e2, gate1, gate2, ld1, ld2)):
        meta = jnp.where(lane == float(k), val, meta)
    meta_ref[...] = meta
    eye = row == col
    ld1_row = jnp.sum(jnp.where(eye, ld1, 0.0), axis=0, keepdims=True)
    ld2_row = jnp.sum(jnp.where(eye, ld2, 0.0), axis=0, keepdims=True)
    r_iota = lax.broadcasted_iota(jnp.int32, (local_rows, tb), 0).astype(F32)
    pick = jnp.where(r_iota == ld1_row, 1.0, jnp.where(r_iota == ld2_row, 1.0, 0.0)).astype(BF16)
    xs_ref[...] = jnp.dot(pick, h_hi, preferred_element_type=F32).astype(BF16)


def _route(x, g, shift, scale, router, *, seq):
    t, d = x.shape
    tb = min(MOE_BLOCK, seq)
    local_rows = TOP_K * tb + N_EXPERTS * MOE_CHUNK
    nb = t // tb
    sh = _RowVec(shift, tb, seq)
    sc = _RowVec(scale, tb, seq)
    router_pad = jnp.zeros((d, LANES), F32).at[:, :N_EXPERTS].set(router.astype(F32))
    router_hi = router_pad.astype(BF16)
    router_lo = (router_pad - router_hi.astype(F32)).astype(BF16)
    return pl.pallas_call(
        _route_kernel,
        grid=(nb,),
        in_specs=[
            pl.BlockSpec((tb, d), lambda i: (i, 0)),
            _const_spec((1, d)),
            sh.spec, sc.spec,
            _const_spec((d, LANES)),
            _const_spec((d, LANES)),
        ],
        out_specs=[
            pl.BlockSpec((local_rows, d), lambda i: (i, 0)),
            pl.BlockSpec((tb, LANES), lambda i: (i, 0)),
            pl.BlockSpec((None, 1, LANES), lambda i: (i, 0, 0)),
        ],
        out_shape=[
            jax.ShapeDtypeStruct((nb * local_rows, d), BF16),
            jax.ShapeDtypeStruct((t, LANES), F32),
            jax.ShapeDtypeStruct((nb, 1, LANES), F32),
        ],
        compiler_params=_params("parallel"),
        name="moe_route",
    )(x, g.reshape(1, d), sh.array, sc.array, router_hi, router_lo)


def _start_chunk_gather(row_smem, slot, src_hbm, dst_vmem, sem):
    n = dst_vmem.shape[1] // MOE_CHUNK

    def start(k, c):
        r0 = pl.multiple_of(row_smem[slot, k], MOE_CHUNK)
        pltpu.make_async_copy(src_hbm.at[pl.ds(r0, MOE_CHUNK)],
                              dst_vmem.at[slot, pl.ds(pl.multiple_of(k * MOE_CHUNK, MOE_CHUNK), MOE_CHUNK)],
                              sem.at[slot]).start()
        return c

    lax.fori_loop(0, n, start, 0, unroll=8)


def _wait_chunk_gather(slot, src_hbm, dst_vmem, sem):
    n = dst_vmem.shape[1]
    pltpu.make_async_copy(src_hbm.at[pl.ds(0, n)], dst_vmem.at[slot], sem.at[slot]).wait()


def _expert_kernel(te_ref, nu_ref, src_ref, xs_ref, wgu_ref, wdn_ref, o_ref,
                   idx_ref, xg_ref, wg_ref, wu_ref, wd_ref, acc_ref, sem_idx, sem_rows, sem_w, *, layer, tf):
    i = pl.program_id(0)
    n_used = nu_ref[0]
    used = i < n_used
    slot = i % 2
    f = wdn_ref.shape[2]
    nf = f // tf

    def idx_copy(tile, into):
        return pltpu.make_async_copy(src_ref.at[tile], idx_ref.at[into], sem_idx)

    def weight_copies(tile, j, into):
        e = te_ref[tile]
        c0 = pl.multiple_of(j * tf, tf)
        return (pltpu.make_async_copy(wgu_ref.at[layer, e, :, pl.ds(c0, tf)], wg_ref.at[into], sem_w.at[0, into]),
                pltpu.make_async_copy(wgu_ref.at[layer, e, :, pl.ds(f + c0, tf)], wu_ref.at[into], sem_w.at[1, into]),
                pltpu.make_async_copy(wdn_ref.at[layer, e, pl.ds(c0, tf), :], wd_ref.at[into], sem_w.at[2, into]))

    @pl.when(used)
    def _():
        @pl.when(i == 0)
        def _():
            idx_copy(0, 0).start()
            idx_copy(0, 0).wait()
            _start_chunk_gather(idx_ref, 0, xs_ref, xg_ref, sem_rows)
            for cp in weight_copies(0, 0, 0):
                cp.start()

        has_next = i + 1 < n_used

        @pl.when(has_next)
        def _():
            idx_copy(i + 1, 1 - slot).start()

        _wait_chunk_gather(slot, xs_ref, xg_ref, sem_rows)
        acc_ref[...] = jnp.zeros(acc_ref.shape, F32)

        @pl.when(has_next)
        def _():
            idx_copy(i + 1, 1 - slot).wait()
            _start_chunk_gather(idx_ref, 1 - slot, xs_ref, xg_ref, sem_rows)

        h = xg_ref[slot]

        def hidden_slice(j, carry):
            wslot = (i * nf + j) % 2

            @pl.when(j + 1 < nf)
            def _():
                for cp in weight_copies(i, j + 1, 1 - wslot):
                    cp.start()

            @pl.when(jnp.logical_and(j + 1 == nf, has_next))
            def _():
                for cp in weight_copies(i + 1, 0, 1 - wslot):
                    cp.start()

            for cp in weight_copies(i, j, wslot):
                cp.wait()
            gt = jnp.dot(h, wg_ref[wslot], preferred_element_type=F32)
            up = jnp.dot(h, wu_ref[wslot], preferred_element_type=F32)
            act = (_silu(gt) * up).astype(BF16)
            acc_ref[...] += jnp.dot(act, wd_ref[wslot], preferred_element_type=F32)
            return carry

        lax.fori_loop(0, nf, hidden_slice, 0)

    o_ref[...] = jnp.where(used, acc_ref[...], 0.0).astype(o_ref.dtype)


def _experts(xs, src_tiles, tile_expert, n_used, w_gu, w_down, layer, *, tm, tf):
    d = xs.shape[1]
    n_tiles, chunks_per_tile = src_tiles.shape
    assert chunks_per_tile * MOE_CHUNK == tm
    assert w_down.shape[2] % tf == 0
    grid_spec = pltpu.PrefetchScalarGridSpec(
        num_scalar_prefetch=2,
        grid=(n_tiles,),
        in_specs=[pl.BlockSpec(memory_space=pl.ANY)] * 4,
        out_specs=pl.BlockSpec((tm, d), lambda i, te, nu: (i, 0)),
        scratch_shapes=[
            pltpu.SMEM((2, chunks_per_tile), jnp.int32),
            pltpu.VMEM((2, tm, d), BF16),
            pltpu.VMEM((2, d, tf), BF16),
            pltpu.VMEM((2, d, tf), BF16),
            pltpu.VMEM((2, tf, d), BF16),
            pltpu.VMEM((tm, d), F32),
            pltpu.SemaphoreType.DMA,
            pltpu.SemaphoreType.DMA((2,)),
            pltpu.SemaphoreType.DMA((3, 2)),
        ],
    )
    return pl.pallas_call(
        functools.partial(_expert_kernel, layer=layer, tf=tf),
        grid_spec=grid_spec,
        out_shape=jax.ShapeDtypeStruct((n_tiles * tm, d), BF16),
        compiler_params=_params("arbitrary"),
        name="moe_experts",
    )(tile_expert, n_used, src_tiles, xs, w_gu, w_down)


def _combine_kernel(src_ref, ys_ref, x_ref, meta_ref, gate_ref, o_ref, idx_ref, yl_ref, sem_idx, sem_rows):
    i = pl.program_id(0)
    slot = i % 2
    has_next = i + 1 < pl.num_programs(0)

    def idx_copy(block, into):
        return pltpu.make_async_copy(src_ref.at[block], idx_ref.at[into], sem_idx)

    @pl.when(i == 0)
    def _():
        idx_copy(0, 0).start()
        idx_copy(0, 0).wait()
        _start_chunk_gather(idx_ref, 0, ys_ref, yl_ref, sem_rows)

    @pl.when(has_next)
    def _():
        idx_copy(i + 1, 1 - slot).start()

    _wait_chunk_gather(slot, ys_ref, yl_ref, sem_rows)

    @pl.when(has_next)
    def _():
        idx_copy(i + 1, 1 - slot).wait()
        _start_chunk_gather(idx_ref, 1 - slot, ys_ref, yl_ref, sem_rows)

    meta = meta_ref[...]
    yl = yl_ref[slot]
    lane = lax.broadcasted_iota(jnp.int32, (meta.shape[0], yl.shape[0]), 1).astype(F32)
    y1 = jnp.dot(jnp.where(lane == meta[:, 4:5], 1.0, 0.0).astype(BF16), yl, preferred_element_type=F32)
    y2 = jnp.dot(jnp.where(lane == meta[:, 5:6], 1.0, 0.0).astype(BF16), yl, preferred_element_type=F32)
    o_ref[...] = x_ref[...] + gate_ref[...] * (meta[:, 2:3] * y1 + meta[:, 3:4] * y2)


def _combine(ys, src_blocks, x, meta, gate, *, seq):
    t, d = x.shape
    nb, chunks_per_block = src_blocks.shape
    tb = t // nb
    local_rows = chunks_per_block * MOE_CHUNK
    gv = _RowVec(gate, tb, seq)
    return pl.pallas_call(
        _combine_kernel,
        grid=(nb,),
        in_specs=[
            pl.BlockSpec(memory_space=pl.ANY),
            pl.BlockSpec(memory_space=pl.ANY),
            pl.BlockSpec((tb, d), lambda i: (i, 0)),
            pl.BlockSpec((tb, LANES), lambda i: (i, 0)),
            gv.spec,
        ],
        out_specs=pl.BlockSpec((tb, d), lambda i: (i, 0)),
        out_shape=jax.ShapeDtypeStruct((t, d), F32),
        scratch_shapes=[
            pltpu.SMEM((2, chunks_per_block), jnp.int32),
            pltpu.VMEM((2, local_rows, d), BF16),
            pltpu.SemaphoreType.DMA,
            pltpu.SemaphoreType.DMA((2,)),
        ],
        compiler_params=_params("arbitrary"),
        name="moe_combine",
    )(src_blocks, ys, x, meta, gv.array)


def _moe_ffn(x, g, shift, scale, gate, router, w_gu, w_down, layer, *, seq):
    t, d = x.shape
    tb = min(MOE_BLOCK, seq)
    tm_e = min(MOE_TILE, seq)
    xs, meta, chunks = _route(x, g, shift, scale, router, seq=seq)
    nb = t // tb
    cpb = (TOP_K * tb + N_EXPERTS * MOE_CHUNK) // MOE_CHUNK
    cpt = tm_e // MOE_CHUNK

    nch = chunks[:, 0, :N_EXPERTS].astype(jnp.int32)
    local_first = jnp.cumsum(nch, axis=1) - nch
    per_expert = jnp.sum(nch, axis=0)
    tiles = (per_expert + cpt - 1) // cpt
    sorted_end = jnp.cumsum(tiles) * cpt
    sorted_first = sorted_end - tiles * cpt
    in_expert_end = jnp.cumsum(nch, axis=0)
    in_expert_first = in_expert_end - nch

    n_tiles = -(-(TOP_K * t // MOE_CHUNK + nb * N_EXPERTS) // cpt) + N_EXPERTS
    slot = jnp.arange(n_tiles * cpt, dtype=jnp.int32)
    e_of = jnp.minimum(jnp.sum(slot[:, None] >= sorted_end[None, :], axis=1), N_EXPERTS - 1)
    is_e = e_of[:, None] == jnp.arange(N_EXPERTS)[None, :]
    by_expert = lambda v: jnp.sum(jnp.where(is_e, v[None, :], 0), axis=1)
    off = slot - by_expert(sorted_first)
    valid = off < by_expert(per_expert)
    end_of_blocks = jnp.sum(jnp.where(is_e[:, :, None], in_expert_end.T[None], 0), axis=1)
    b_of = jnp.minimum(jnp.sum(off[:, None] >= end_of_blocks, axis=1), nb - 1)
    is_b = b_of[:, None] == jnp.arange(nb)[None, :]
    shift_tab = (local_first - in_expert_first).T
    shift = jnp.sum(jnp.where(is_b, jnp.sum(jnp.where(is_e[:, :, None], shift_tab[None], 0), axis=1), 0), axis=1)
    src_rows = jnp.where(valid, b_of * cpb + shift + off, 0) * MOE_CHUNK
    n_used = jnp.sum(tiles).astype(jnp.int32)
    tile_id = jnp.arange(n_tiles, dtype=jnp.int32)
    frozen = jnp.minimum(tile_id, jnp.maximum(n_used - 1, 0)) * cpt
    tile_expert = jnp.minimum(jnp.sum(frozen[:, None] >= sorted_end[None, :], axis=1), N_EXPERTS - 1)
    tile_expert = tile_expert.astype(jnp.int32)

    lc = jnp.arange(cpb, dtype=jnp.int32)
    local_end = local_first + nch
    e_loc = jnp.sum(lc[None, :, None] >= local_end[:, None, :], axis=2)
    used_loc = e_loc < N_EXPERTS
    is_e_loc = e_loc[:, :, None] == jnp.arange(N_EXPERTS)[None, None, :]
    back_tab = sorted_first[None, :] + in_expert_first - local_first
    back_chunk = jnp.sum(jnp.where(is_e_loc, back_tab[:, None, :], 0), axis=2) + lc[None, :]
    back_rows = jnp.where(used_loc, back_chunk, 0) * MOE_CHUNK

    ys = _experts(xs, src_rows.reshape(n_tiles, cpt), tile_expert, n_used.reshape(1),
                  w_gu, w_down, layer, tm=tm_e, tf=w_down.shape[2] // 2)
    return _combine(ys, back_rows.astype(jnp.int32), x, meta, gate, seq=seq)


def kernel(x, c, ada_w, ada_b, norm1_g, norm2_g, at_w_in, at_q_norm, at_k_norm, at_lam, at_subln, at_w_out, hg_w_in, hg_lower_bounds, hg_norm_g, hg_w_out, lru_w_in, lru_conv_w, lru_conv_b, lru_w_a, lru_b_a, lru_w_x, lru_b_x, lru_lambda, lru_w_out, ff_w_gu, ff_w_down, moe_router, moe_w_gu, moe_w_down):
    batch, seq, d = x.shape
    depth = ada_w.shape[0]
    t = batch * seq
    mods = _mods(c, ada_w, ada_b)
    ff_w_gu, ff_w_down = ff_w_gu.astype(BF16), ff_w_down.astype(BF16)
    moe_w_gu, moe_w_down = moe_w_gu.astype(BF16), moe_w_down.astype(BF16)
    xr = x.reshape(t, d)
    for i in range(depth):
        sh1, sc1, g1, sh2, sc2, g2 = [mods[i, :, k * d:(k + 1) * d] for k in range(6)]
        kind, j = i % N_MIXERS, i // N_MIXERS
        if kind == 0:
            lambda_init = 0.8 - 0.6 * math.exp(-0.3 * i)
            xr = _diff_attention_layer(xr, norm1_g[i], sh1, sc1, g1, at_w_in[j], at_q_norm[j], at_k_norm[j],
                                       at_lam[j], at_subln[j], at_w_out[j], lambda_init, batch=batch, seq=seq)
        elif kind == 1:
            xr = _hgrn_layer(xr, norm1_g[i], sh1, sc1, g1, hg_w_in[j], hg_lower_bounds, hg_norm_g[j],
                             hg_w_out[j], i, batch=batch, seq=seq)
        else:
            xr = _rglru_layer(xr, norm1_g[i], sh1, sc1, g1, lru_w_in[j], lru_conv_w[j], lru_conv_b[j],
                              lru_w_a[j], lru_b_a[j], lru_w_x[j], lru_b_x[j], lru_lambda[j], lru_w_out[j],
                              batch=batch, seq=seq)
        m = i // 2
        if i % 2 == 0:
            xr = _dense_ffn(xr, norm2_g[i], sh2, sc2, g2, ff_w_gu, ff_w_down, m,
                            seq=seq, tm=min(512, seq), tf=ff_w_down.shape[1])
        else:
            xr = _moe_ffn(xr, norm2_g[i], sh2, sc2, g2, moe_router[m], moe_w_gu, moe_w_down, m, seq=seq)
    return xr.reshape(batch, seq, d)
```
